```python
import jax, jax.numpy as jnp
from jax import lax
import numpy as np

D_MODEL = 4096
BATCH = 2
SEQ = 4096
DEPTH = 2

CTX_LEN = 256
GRID_W = 64
NA_HEADS = 8
NA_DH = 128
NA_W = NA_HEADS * NA_DH
NA_ROWS = 8
NA_COLS = 16
ML_HEADS = 8
ML_DK = 128
ML_DV = 256
ML_QK = ML_HEADS * ML_DK
ML_V = ML_HEADS * ML_DV
ML_CHUNK = 128
ML_CONV = 3
GM_GROUPS = 8
GM_CPG = 128
GM_W = GM_GROUPS * GM_CPG
GM_CHUNK = 128
N_BRANCH = 3
D_MIX = NA_W + ML_V + GM_W
N_EXPERTS = 16
EXPERT_FF = 1024
CAPACITY_FACTOR = 2
ROPE_BASE = 10000.0
EPS = 1e-6
NEG_INF = -1e30
IN_SIZES = (NA_W, NA_W, NA_W, ML_QK, ML_QK, ML_V, ML_V, 4 * ML_HEADS, GM_W, GM_W, N_BRANCH * D_MODEL)
IN_OFFSETS = tuple(int(o) for o in np.cumsum(IN_SIZES)[:-1])
D_IN = int(sum(IN_SIZES))

kernel_name = 'hybrid_na_mlstm_gmlp_ec_moe_diffusion'


def rmsnorm(x, g):
    xf = x.astype(jnp.float32)
    y = xf * lax.rsqrt(jnp.mean(xf * xf, axis=-1, keepdims=True) + EPS)
    return (y * g.astype(jnp.float32)).astype(x.dtype)


def modulate(h, shift, scale):
    return h * (1 + scale) + shift


def depthwise_conv_centred(x, w):
    K, C = w.shape
    return lax.conv_general_dilated(x, w[:, None, :], window_strides=(1,), padding=[(K // 2, K // 2)],
                                    dimension_numbers=('NWC', 'WIO', 'NWC'), feature_group_count=C)


def axial_rope(x):
    B, L, H, d = x.shape
    nf = d // 4
    t = jnp.arange(L)
    pos = jnp.stack([t // GRID_W, t % GRID_W], axis=-1).astype(jnp.float32)
    inv_freq = ROPE_BASE ** (-jnp.arange(nf, dtype=jnp.float32) / nf)
    ang = pos[:, :, None] * inv_freq
    cos = jnp.cos(ang)[None, :, None]
    sin = jnp.sin(ang)[None, :, None]
    xr = x.astype(jnp.float32).reshape(B, L, H, 2, 2, nf)
    x1, x2 = xr[..., 0, :], xr[..., 1, :]
    out = jnp.stack([x1 * cos - x2 * sin, x2 * cos + x1 * sin], axis=-2)
    return out.reshape(B, L, H, d).astype(x.dtype)


def na_heads(p, q_gain, k_gain):
    B, L, _ = p[0].shape
    q = rmsnorm(p[0].reshape(B, L, NA_HEADS, NA_DH), q_gain)
    k = rmsnorm(p[1].reshape(B, L, NA_HEADS, NA_DH), k_gain)
    v = p[2].reshape(B, L, NA_HEADS, NA_DH)
    return q, k, v


def neighbourhood_attention(q, k, v, k_ctx, v_ctx, rel_bias):
    B, L, H, dh = q.shape
    rows = L // GRID_W
    kr = min(NA_ROWS, rows)
    nk = kr * GRID_W
    scale = dh ** -0.5
    qg = q.reshape(B, rows, GRID_W, H, dh)
    kg = k.reshape(B, rows, GRID_W, H, dh)
    vg = v.reshape(B, rows, GRID_W, H, dh)
    r = jnp.arange(rows)
    r0 = jnp.clip(r - kr // 2, 0, rows - kr)
    krow = r0[:, None] + jnp.arange(kr)[None, :]
    kn = kg[:, krow].reshape(B, rows, nk, H, dh)
    vn = vg[:, krow].reshape(B, rows, nk, H, dh)
    col = jnp.arange(GRID_W)
    c0 = jnp.clip(col - NA_COLS // 2, 0, GRID_W - NA_COLS)
    kcol = jnp.tile(col, kr)
    in_win = (kcol[None, :] >= c0[:, None]) & (kcol[None, :] < c0[:, None] + NA_COLS)
    dr = jnp.repeat(krow - r[:, None], GRID_W, axis=1)
    dc = jnp.clip(kcol[None, :] - col[:, None] + NA_COLS - 1, 0, 2 * NA_COLS - 2)
    bias = rel_bias.astype(jnp.float32)[:, dr[:, None, :] + NA_ROWS - 1, dc[None]]
    s_lat = jnp.einsum('brqhd,brkhd->bhrqk', qg, kn).astype(jnp.float32) * scale + bias[None]
    s_lat = jnp.where(in_win, s_lat, NEG_INF)
    s_ctx = jnp.einsum('brqhd,bchd->bhrqc', qg, k_ctx).astype(jnp.float32) * scale
    p = jax.nn.softmax(jnp.concatenate([s_lat, s_ctx], axis=-1), axis=-1).astype(v.dtype)
    out = (jnp.einsum('bhrqk,brkhd->brqhd', p[..., :nk], vn)
           + jnp.einsum('bhrqc,bchd->brqhd', p[..., nk:], v_ctx))
    return out.reshape(B, L, H * dh)


def context_attention(q, k, v):
    B, L, H, dh = q.shape
    s = jnp.einsum('bqhd,bkhd->bhqk', q, k).astype(jnp.float32) * dh ** -0.5
    p = jax.nn.softmax(s, axis=-1).astype(v.dtype)
    return jnp.einsum('bhqk,bkhd->bqhd', p, v).reshape(B, L, H * dh)


def mlstm_inputs(p, conv_w, gate_bias, latent):
    B, L, _ = p[3].shape
    qk = jax.nn.silu(depthwise_conv_centred(jnp.concatenate([p[3], p[4]], axis=-1), conv_w))
    q = qk[..., :ML_QK].reshape(B, L, ML_HEADS, ML_DK)
    k = qk[..., ML_QK:].reshape(B, L, ML_HEADS, ML_DK)
    if latent:
        q, k = axial_rope(q), axial_rope(k)
    k = k * (ML_DK ** -0.5)
    v = p[5].reshape(B, L, ML_HEADS, ML_DV)
    g = p[7].reshape(B, L, 4, ML_HEADS).astype(jnp.float32) + gate_bias.astype(jnp.float32)
    fwd = (g[:, :, 0], jax.nn.log_sigmoid(g[:, :, 1]))
    bwd = (g[:, :, 2], jax.nn.log_sigmoid(g[:, :, 3]))
    return q, k, v, fwd, bwd


def mlstm_scan(q, k, v, li, lf, state):
    B, L, H, dk = q.shape
    dv = v.shape[-1]
    nc = L // ML_CHUNK

    def to_chunks(a):
        a = a.astype(jnp.float32).reshape((B, nc, ML_CHUNK, H) + a.shape[3:])
        return jnp.moveaxis(a, (1, 3), (0, 2))

    tri = jnp.tril(jnp.ones((ML_CHUNK, ML_CHUNK), dtype=bool))

    def step(carry, inp):
        C, n, m = carry
        qc, kc, vc, lic, lfc = inp
        b = jnp.cumsum(lfc, axis=-1)
        a = b + m[..., None]
        dlog = jnp.where(tri, b[..., :, None] - b[..., None, :] + lic[..., None, :], NEG_INF)
        mj = jnp.maximum(a, jnp.max(dlog, axis=-1))
        w_inter = jnp.exp(a - mj)
        s = jnp.einsum('bhjd,bhsd->bhjs', qc, kc) * jnp.exp(dlog - mj[..., None])
        num = w_inter[..., None] * jnp.einsum('bhjd,bhde->bhje', qc, C) + jnp.einsum('bhjs,bhse->bhje', s, vc)
        den = w_inter * jnp.einsum('bhjd,bhd->bhj', qc, n) + jnp.sum(s, axis=-1)
        h = num / jnp.maximum(jnp.abs(den), jnp.exp(-mj))[..., None]
        bl = b[..., -1]
        gl = bl[..., None] - b + lic
        m_new = jnp.maximum(bl + m, jnp.max(gl, axis=-1))
        sc = jnp.exp(bl + m - m_new)
        wg = jnp.exp(gl - m_new[..., None])
        C_new = sc[..., None, None] * C + jnp.einsum('bhs,bhsd,bhse->bhde', wg, kc, vc)
        n_new = sc[..., None] * n + jnp.einsum('bhs,bhsd->bhd', wg, kc)
        return (C_new, n_new, m_new), h

    state, h = lax.scan(step, state, (to_chunks(q), to_chunks(k), to_chunks(v), to_chunks(li), to_chunks(lf)))
    h = jnp.moveaxis(h, (0, 2), (1, 3)).reshape(B, L, H, dv)
    return h.astype(v.dtype), state


def mlstm_bidirectional(ctx_in, lat_in):
    qc, kc, vc, gc_f, gc_b = ctx_in
    ql, kl, vl, gl_f, gl_b = lat_in
    B = qc.shape[0]
    zero = (jnp.zeros((B, ML_HEADS, ML_DK, ML_DV), jnp.float32),
            jnp.zeros((B, ML_HEADS, ML_DK), jnp.float32),
            jnp.zeros((B, ML_HEADS), jnp.float32))
    flip = lambda a: jnp.flip(a, axis=1)
    hc_f, state_f = mlstm_scan(qc, kc, vc, gc_f[0], gc_f[1], zero)
    hc_b, state_b = mlstm_scan(flip(qc), flip(kc), flip(vc), flip(gc_b[0]), flip(gc_b[1]), zero)
    hl_f, _ = mlstm_scan(ql, kl, vl, gl_f[0], gl_f[1], state_f)
    hl_b, _ = mlstm_scan(flip(ql), flip(kl), flip(vl), flip(gl_b[0]), flip(gl_b[1]), state_b)
    return hc_f + flip(hc_b), hl_f + flip(hl_b)


def mlstm_output(h, o, gain):
    B, L = h.shape[:2]
    return rmsnorm(h, gain.reshape(ML_HEADS, ML_DV)).reshape(B, L, ML_V) * jax.nn.sigmoid(o)


def spatial_gating(u, v, v_gain, ws, bias):
    B, L, _ = u.shape
    n = L // GM_CHUNK
    vn = rmsnorm(v, v_gain).reshape(B, n, GM_CHUNK, GM_GROUPS, GM_CPG)
    mixed = jnp.einsum('gpq,bnqgc->bnpgc', ws, vn) + bias.T[:, :, None]
    return u * mixed.reshape(B, L, GM_W)


def merge_branches(p_gate, ya, ym, yg, w_branch, w_out):
    B, L, _ = ya.shape
    g = jax.nn.sigmoid(p_gate).reshape(B, L, N_BRANCH, D_MODEL)
    wa, wm, wg = jnp.split(w_branch, (NA_W, NA_W + ML_V), axis=0)
    z = g[:, :, 0] * (ya @ wa) + g[:, :, 1] * (ym @ wm) + g[:, :, 2] * (yg @ wg)
    return z @ w_out


def mixer_sublayer(hl, hc, w_in, na_q_gain, na_k_gain, na_rel_bias, ml_conv, ml_gate_bias, ml_out_gain,
                   gm_v_gain, gm_ws, gm_bias, w_branch, w_out, with_ctx_out):
    pl = jnp.split(hl @ w_in, IN_OFFSETS, axis=-1)
    pc = jnp.split(hc @ w_in, IN_OFFSETS, axis=-1)
    ql, kl, vl = na_heads(pl, na_q_gain, na_k_gain)
    qc, kc, vc = na_heads(pc, na_q_gain, na_k_gain)
    ya_l = neighbourhood_attention(ql, kl, vl, kc, vc, na_rel_bias)
    ml_l = mlstm_inputs(pl, ml_conv, ml_gate_bias, True)
    ml_c = mlstm_inputs(pc, ml_conv, ml_gate_bias, False)
    hm_c, hm_l = mlstm_bidirectional(ml_c, ml_l)
    ym_l = mlstm_output(hm_l, pl[6], ml_out_gain)
    yg_l = spatial_gating(jax.nn.gelu(pl[8], approximate=False), jax.nn.gelu(pl[9], approximate=False),
                          gm_v_gain, gm_ws, gm_bias)
    y_lat = merge_branches(pl[10], ya_l, ym_l, yg_l, w_branch, w_out)
    if not with_ctx_out:
        return y_lat, None
    ya_c = context_attention(qc, kc, vc)
    ym_c = mlstm_output(hm_c, pc[6], ml_out_gain)
    yg_c = spatial_gating(jax.nn.gelu(pc[8], approximate=False), jax.nn.gelu(pc[9], approximate=False),
                          gm_v_gain, gm_ws, gm_bias)
    y_ctx = merge_branches(pc[10], ya_c, ym_c, yg_c, w_branch, w_out)
    return y_lat, y_ctx


def expert_choice_ffn(h, w_router, w_gate, w_up, w_down):
    B, L, _ = h.shape
    cap = max(1, min(L, (CAPACITY_FACTOR * L) // N_EXPERTS))
    aff = jax.nn.softmax(jnp.einsum('bld,de->ble', h, w_router).astype(jnp.float32), axis=-1)
    g, idx = lax.top_k(jnp.swapaxes(aff, 1, 2), cap)
    bidx = jnp.arange(B)[:, None, None]
    xs = h[bidx, idx]
    hid = jax.nn.silu(jnp.einsum('becd,edf->becf', xs, w_gate)) * jnp.einsum('becd,edf->becf', xs, w_up)
    ye = jnp.einsum('becf,efd->becd', hid, w_down) * g[..., None].astype(h.dtype)
    return jnp.zeros_like(h).at[bidx, idx].add(ye)


def setup_inputs(seed: int = 0) -> dict:
    key = jax.random.key(seed)
    ks = jax.random.split(key, 28)
    f32 = jnp.float32

    def nrm(k, shape, s):
        return jax.random.normal(k, shape, f32) * s

    fb = jnp.linspace(3.0, 6.0, ML_HEADS, dtype=f32)
    zb = jnp.zeros((ML_HEADS,), f32)
    gate_base = jnp.stack([zb, fb, zb, fb])
    w_branch = jnp.concatenate([nrm(ks[19], (DEPTH, NA_W, D_MODEL), NA_W ** -0.5),
                                nrm(ks[20], (DEPTH, ML_V, D_MODEL), ML_V ** -0.5),
                                nrm(ks[21], (DEPTH, GM_W, D_MODEL), GM_W ** -0.5)], axis=1)
    return {
        'x': nrm(ks[0], (BATCH, SEQ, D_MODEL), 1.0),
        'c': nrm(ks[1], (BATCH, D_MODEL), 1.0),
        'ctx': nrm(ks[2], (BATCH, CTX_LEN, D_MODEL), 1.0),
        'c_ctx': nrm(ks[3], (D_MODEL,), 1.0),
        'w_ada': nrm(ks[4], (DEPTH, D_MODEL, 6 * D_MODEL), 0.5 * D_MODEL ** -0.5),
        'b_ada': nrm(ks[5], (DEPTH, 6 * D_MODEL), 0.02),
        'g_norm1': 1.0 + nrm(ks[6], (DEPTH, D_MODEL), 0.1),
        'g_norm2': 1.0 + nrm(ks[7], (DEPTH, D_MODEL), 0.1),
        'w_in': nrm(ks[8], (DEPTH, D_MODEL, D_IN), D_MODEL ** -0.5),
        'na_q_gain': 1.0 + nrm(ks[9], (DEPTH, NA_DH), 0.1),
        'na_k_gain': 1.0 + nrm(ks[10], (DEPTH, NA_DH), 0.1),
        'na_rel_bias': nrm(ks[11], (DEPTH, NA_HEADS, 2 * NA_ROWS - 1, 2 * NA_COLS - 1), 0.2),
        'ml_conv': nrm(ks[12], (DEPTH, ML_CONV, 2 * ML_QK), ML_CONV ** -0.5),
        'ml_gate_bias': gate_base[None] + nrm(ks[13], (DEPTH, 4, ML_HEADS), 0.1),
        'ml_out_gain': 1.0 + nrm(ks[14], (DEPTH, ML_V), 0.1),
        'gm_v_gain': 1.0 + nrm(ks[15], (DEPTH, GM_W), 0.1),
        'gm_ws': nrm(ks[16], (DEPTH, GM_GROUPS, GM_CHUNK, GM_CHUNK), GM_CHUNK ** -0.5),
        'gm_bias': 1.0 + nrm(ks[17], (DEPTH, GM_GROUPS, GM_CHUNK), 0.1),
        'w_branch': w_branch,
        'w_out': nrm(ks[18], (DEPTH, D_MIX, D_MODEL), D_MIX ** -0.5),
        'w_router': nrm(ks[22], (DEPTH, D_MODEL, N_EXPERTS), D_MODEL ** -0.5),
        'w_e_gate': nrm(ks[23], (DEPTH, N_EXPERTS, D_MODEL, EXPERT_FF), D_MODEL ** -0.5),
        'w_e_up': nrm(ks[24], (DEPTH, N_EXPERTS, D_MODEL, EXPERT_FF), D_MODEL ** -0.5),
        'w_e_down': nrm(ks[25], (DEPTH, N_EXPERTS, EXPERT_FF, D_MODEL), EXPERT_FF ** -0.5),
    }


def reference(x, c, ctx, c_ctx, w_ada, b_ada, g_norm1, g_norm2, w_in, na_q_gain, na_k_gain, na_rel_bias,
              ml_conv, ml_gate_bias, ml_out_gain, gm_v_gain, gm_ws, gm_bias, w_branch, w_out, w_router,
              w_e_gate, w_e_up, w_e_down):
    for l in range(DEPTH):
        last = l == DEPTH - 1
        mod_l = (jax.nn.silu(c) @ w_ada[l] + b_ada[l])[:, None, :]
        mod_c = jax.nn.silu(c_ctx) @ w_ada[l] + b_ada[l]
        sh1, sc1, gt1, sh2, sc2, gt2 = jnp.split(mod_l, 6, axis=-1)
        csh1, csc1, cgt1, csh2, csc2, cgt2 = jnp.split(mod_c, 6, axis=-1)
        hl = modulate(rmsnorm(x, g_norm1[l]), sh1, sc1)
        hc = modulate(rmsnorm(ctx, g_norm1[l]), csh1, csc1)
        y_lat, y_ctx = mixer_sublayer(hl, hc, w_in[l], na_q_gain[l], na_k_gain[l], na_rel_bias[l], ml_conv[l],
                                      ml_gate_bias[l], ml_out_gain[l], gm_v_gain[l], gm_ws[l], gm_bias[l],
                                      w_branch[l], w_out[l], not last)
        x = x + gt1 * y_lat
        x = x + gt2 * expert_choice_ffn(modulate(rmsnorm(x, g_norm2[l]), sh2, sc2),
                                        w_router[l], w_e_gate[l], w_e_up[l], w_e_down[l])
        if not last:
            ctx = ctx + cgt1 * y_ctx
            ctx = ctx + cgt2 * expert_choice_ffn(modulate(rmsnorm(ctx, g_norm2[l]), csh2, csc2),
                                                 w_router[l], w_e_gate[l], w_e_up[l], w_e_down[l])
    return x
```

```python
import functools

import numpy as np
import jax
import jax.numpy as jnp
from jax import lax
from jax.experimental import pallas as pl
from jax.experimental.pallas import tpu as pltpu

F32 = jnp.float32
BF16 = jnp.bfloat16
I32 = jnp.int32

GRID_W = 64
NA_HEADS, NA_DH, NA_ROWS, NA_COLS = 8, 128, 8, 16
NA_W = NA_HEADS * NA_DH
ML_HEADS, ML_DK, ML_DV, ML_CHUNK = 8, 128, 256, 128
ML_QK, ML_V = ML_HEADS * ML_DK, ML_HEADS * ML_DV
GM_GROUPS, GM_CPG, GM_CHUNK = 8, 128, 128
GM_W = GM_GROUPS * GM_CPG
N_BRANCH = 3
CAPACITY_FACTOR = 2
ROPE_BASE = 10000.0
EPS = 1e-6
NEG_INF = -1e30

VMEM_LIMIT_BYTES = 56 * 1024 * 1024
LANES = 128
ROW_TILE = 256
MM_TM = 512

OFF_NA_Q, OFF_NA_K, OFF_NA_V = 0, NA_W, 2 * NA_W
OFF_GM_U = 3 * NA_W
OFF_ML_V = OFF_GM_U + GM_W
OFF_ML_O = OFF_ML_V + ML_V
OFF_ML_QK = OFF_ML_O + ML_V
OFF_GM_V = OFF_ML_QK + 2 * ML_QK
OFF_GATE = OFF_GM_V + GM_W


def _cparams(*sem):
    return pltpu.CompilerParams(dimension_semantics=sem, vmem_limit_bytes=VMEM_LIMIT_BYTES)


def _nt_dot(a, b):
    return lax.dot_general(a, b, (((1,), (1,)), ((), ())), preferred_element_type=F32)


def _ada_kernel(c_ref, w_ref, b_ref, o_ref):
    a = jax.nn.silu(c_ref[...]).astype(BF16)
    o_ref[...] = jnp.dot(a, w_ref[...].astype(BF16), preferred_element_type=F32) + b_ref[...]


def _ada(c8, w_ada, b_ada):
    depth, d, n = w_ada.shape
    tn = 512
    return pl.pallas_call(
        _ada_kernel,
        grid=(depth, n // tn),
        in_specs=[pl.BlockSpec((8, d), lambda l, j: (0, 0)),
                  pl.BlockSpec((None, d, tn), lambda l, j: (l, 0, j)),
                  pl.BlockSpec((None, 1, tn), lambda l, j: (l, 0, j))],
        out_specs=pl.BlockSpec((None, 8, tn), lambda l, j: (l, 0, j)),
        out_shape=jax.ShapeDtypeStruct((depth, 8, n), F32),
        compiler_params=_cparams("parallel", "parallel"),
        name="ada_mod",
    )(c8, w_ada, b_ada.reshape(depth, 1, n))


def _mod_row(i, tm, n_lat, seq, n_batch):
    r = i * tm
    return jnp.where(r >= n_lat, n_batch, r // seq)


def _prenorm_kernel(x_ref, g_ref, sh_ref, sc_ref, o_ref):
    x = x_ref[...]
    y = x * lax.rsqrt(jnp.mean(x * x, axis=-1, keepdims=True) + EPS)
    y = y * g_ref[...]
    o_ref[...] = (y * (1 + sc_ref[...]) + sh_ref[...]).astype(o_ref.dtype)


def _prenorm(x, g, mod3, k_shift, k_scale, dims, rows, out_dtype):
    d = x.shape[1]
    tm = ROW_TILE
    mrow = functools.partial(_mod_row, tm=tm, n_lat=dims["n_lat"], seq=dims["L"], n_batch=dims["B"])
    return pl.pallas_call(
        _prenorm_kernel,
        grid=(rows // tm,),
        in_specs=[pl.BlockSpec((tm, d), lambda i: (i, 0)),
                  pl.BlockSpec((1, d), lambda i: (0, 0)),
                  pl.BlockSpec((None, 1, d), lambda i: (mrow(i), 0, k_shift)),
                  pl.BlockSpec((None, 1, d), lambda i: (mrow(i), 0, k_scale))],
        out_specs=pl.BlockSpec((tm, d), lambda i: (i, 0)),
        out_shape=jax.ShapeDtypeStruct((rows, d), out_dtype),
        compiler_params=_cparams("parallel"),
        name="prenorm",
    )(x, g.reshape(1, d), mod3, mod3)


def _mm_kernel(a_ref, w_ref, o_ref):
    o_ref[...] = jnp.dot(a_ref[...], w_ref[...], preferred_element_type=F32).astype(o_ref.dtype)


def _matmul(a, w, tn, out_dtype, name):
    m, k = a.shape
    n = w.shape[1]
    tm = MM_TM
    return pl.pallas_call(
        _mm_kernel,
        grid=(n // tn, m // tm),
        in_specs=[pl.BlockSpec((tm, k), lambda j, i: (i, 0)),
                  pl.BlockSpec((k, tn), lambda j, i: (0, j))],
        out_specs=pl.BlockSpec((tm, tn), lambda j, i: (i, j)),
        out_shape=jax.ShapeDtypeStruct((m, n), out_dtype),
        compiler_params=_cparams("parallel", "parallel"),
        name=name,
    )(a, w)


def _gates_kernel(a_ref, w_ref, b_ref, o_ref):
    g = jnp.dot(a_ref[...], w_ref[...], preferred_element_type=F32) + b_ref[...]
    col = lax.broadcasted_iota(I32, g.shape, 1)
    is_forget = ((col // ML_HEADS) % 2) == 1
    o_ref[...] = jnp.where(is_forget, jax.nn.log_sigmoid(g), g)


def _ml_gates(h, w_g, b_g):
    m, k = h.shape
    tm = MM_TM
    return pl.pallas_call(
        _gates_kernel,
        grid=(m // tm,),
        in_specs=[pl.BlockSpec((tm, k), lambda i: (i, 0)),
                  pl.BlockSpec((k, LANES), lambda i: (0, 0)),
                  pl.BlockSpec((1, LANES), lambda i: (0, 0))],
        out_specs=pl.BlockSpec((tm, LANES), lambda i: (i, 0)),
        out_shape=jax.ShapeDtypeStruct((m, LANES), F32),
        compiler_params=_cparams("parallel"),
        name="ml_gates",
    )(h, w_g, b_g)


def _head_rms(x, g):
    return x * lax.rsqrt(jnp.mean(x * x, axis=-1, keepdims=True) + EPS) * g


def _na_kernel(q_ref, k_ref, v_ref, kc_ref, vc_ref, qg_ref, kg_ref, bias_ref, o_ref,
               qn_ref, kn_ref, vn_ref, *, rows, kr):
    w = GRID_W
    nk = kr * w
    scale = NA_DH ** -0.5
    qn_ref[...] = _head_rms(q_ref[...], qg_ref[...]).astype(BF16)
    kn_ref[...] = _head_rms(k_ref[...], kg_ref[...]).astype(BF16)
    vn_ref[...] = v_ref[...].astype(BF16)
    kc = _head_rms(kc_ref[...], kg_ref[...]).astype(BF16)
    vc = vc_ref[...].astype(BF16)
    qcol = lax.broadcasted_iota(I32, (w, nk), 0)
    kcol = lax.broadcasted_iota(I32, (w, nk), 1) % w
    c0 = jnp.clip(qcol - NA_COLS // 2, 0, w - NA_COLS)
    in_win = (kcol >= c0) & (kcol < c0 + NA_COLS)

    def body(r, carry):
        r0 = jnp.clip(r - kr // 2, 0, rows - kr)
        off = r0 - r + (kr - 1)
        qr = qn_ref[pl.ds(pl.multiple_of(r * w, w), w), :]
        kw = kn_ref[pl.ds(pl.multiple_of(r0 * w, w), nk), :]
        vw = vn_ref[pl.ds(pl.multiple_of(r0 * w, w), nk), :]
        s_lat = _nt_dot(qr, kw) * scale + bias_ref[off]
        s_lat = jnp.where(in_win, s_lat, NEG_INF)
        s_ctx = _nt_dot(qr, kc) * scale
        m = jnp.maximum(jnp.max(s_lat, axis=-1, keepdims=True), jnp.max(s_ctx, axis=-1, keepdims=True))
        p_lat = jnp.exp(s_lat - m)
        p_ctx = jnp.exp(s_ctx - m)
        den = jnp.sum(p_lat, axis=-1, keepdims=True) + jnp.sum(p_ctx, axis=-1, keepdims=True)
        out = (jnp.dot(p_lat.astype(BF16), vw, preferred_element_type=F32)
               + jnp.dot(p_ctx.astype(BF16), vc, preferred_element_type=F32))
        o_ref[pl.ds(pl.multiple_of(r * w, w), w), :] = (out / den).astype(o_ref.dtype)
        return carry

    lax.fori_loop(0, rows, body, 0)


def _na_bias_table(rel_bias, kr):
    w = GRID_W
    n_dc = 2 * NA_COLS - 1
    dc = np.clip(np.arange(w)[None, :] - np.arange(w)[:, None] + NA_COLS - 1, 0, n_dc - 1)
    onehot = jnp.asarray((dc[:, :, None] == np.arange(n_dc)).astype(np.float32))
    toep = jnp.einsum("hrd,ckd->hrck", rel_bias.astype(F32), onehot, precision=lax.Precision.HIGHEST)
    r_lo = NA_ROWS - kr
    tab = jnp.stack([toep[:, r_lo + off:r_lo + off + kr] for off in range(kr)], axis=1)
    return jnp.transpose(tab, (0, 1, 3, 2, 4)).reshape(rel_bias.shape[0], kr, w, kr * w)


def _na_latent(p, q_gain, k_gain, bias_tab, dims):
    b, l, lc = dims["B"], dims["L"], dims["Lc"]
    rows = l // GRID_W
    kr = min(NA_ROWS, rows)
    nk = kr * GRID_W
    hd = NA_DH
    kern = functools.partial(_na_kernel, rows=rows, kr=kr)
    ctx_blk = (b * l) // lc
    return pl.pallas_call(
        kern,
        grid=(b, NA_HEADS),
        in_specs=[pl.BlockSpec((l, hd), lambda i, h: (i, OFF_NA_Q // hd + h)),
                  pl.BlockSpec((l, hd), lambda i, h: (i, OFF_NA_K // hd + h)),
                  pl.BlockSpec((l, hd), lambda i, h: (i, OFF_NA_V // hd + h)),
                  pl.BlockSpec((lc, hd), lambda i, h: (ctx_blk + i, OFF_NA_K // hd + h)),
                  pl.BlockSpec((lc, hd), lambda i, h: (ctx_blk + i, OFF_NA_V // hd + h)),
                  pl.BlockSpec((1, hd), lambda i, h: (0, 0)),
                  pl.BlockSpec((1, hd), lambda i, h: (0, 0)),
                  pl.BlockSpec((None, kr, GRID_W, nk), lambda i, h: (h, 0, 0, 0))],
        out_specs=pl.BlockSpec((l, hd), lambda i, h: (i, h)),
        out_shape=jax.ShapeDtypeStruct((b * l, NA_W), BF16),
        scratch_shapes=[pltpu.VMEM((l, hd), BF16), pltpu.VMEM((l, hd), BF16), pltpu.VMEM((l, hd), BF16)],
        compiler_params=_cparams("parallel", "parallel"),
        name="na_latent",
    )(p, p, p, p, p, q_gain.reshape(1, hd), k_gain.reshape(1, hd), bias_tab)


def _ctx_attn_kernel(q_ref, k_ref, v_ref, qg_ref, kg_ref, o_ref):
    q = _head_rms(q_ref[...], qg_ref[...]).astype(BF16)
    k = _head_rms(k_ref[...], kg_ref[...]).astype(BF16)
    s = _nt_dot(q, k) * (NA_DH ** -0.5)
    m = jnp.max(s, axis=-1, keepdims=True)
    pr = jnp.exp(s - m)
    den = jnp.sum(pr, axis=-1, keepdims=True)
    out = jnp.dot(pr.astype(BF16), v_ref[...].astype(BF16), preferred_element_type=F32)
    o_ref[...] = (out / den).astype(o_ref.dtype)


def _na_context(p, q_gain, k_gain, dims):
    b, l, lc = dims["B"], dims["L"], dims["Lc"]
    hd = NA_DH
    ctx_blk = (b * l) // lc
    return pl.pallas_call(
        _ctx_attn_kernel,
        grid=(b, NA_HEADS),
        in_specs=[pl.BlockSpec((lc, hd), lambda i, h: (ctx_blk + i, OFF_NA_Q // hd + h)),
                  pl.BlockSpec((lc, hd), lambda i, h: (ctx_blk + i, OFF_NA_K // hd + h)),
                  pl.BlockSpec((lc, hd), lambda i, h: (ctx_blk + i, OFF_NA_V // hd + h)),
                  pl.BlockSpec((1, hd), lambda i, h: (0, 0)),
                  pl.BlockSpec((1, hd), lambda i, h: (0, 0))],
        out_specs=pl.BlockSpec((lc, hd), lambda i, h: (i, h)),
        out_shape=jax.ShapeDtypeStruct((b * lc, NA_W), BF16),
        compiler_params=_cparams("parallel", "parallel"),
        name="na_context",
    )(p, p, p, q_gain.reshape(1, hd), k_gain.reshape(1, hd))


def _rope_tables(seq, pad_rows):
    nf = ML_DK // 4
    t = np.arange(seq)
    pos = np.stack([t // GRID_W, t % GRID_W], axis=-1).astype(np.float32)
    inv_freq = (ROPE_BASE ** (-np.arange(nf, dtype=np.float32) / nf)).astype(np.float32)
    lane = np.arange(ML_DK)
    axis, pair, f = lane // (2 * nf), (lane // nf) % 2, lane % nf
    ang = jnp.asarray(pos[:, axis]) * jnp.asarray(inv_freq[f])[None, :]
    cos, sin = jnp.cos(ang), jnp.sin(ang)
    s_lo = jnp.where(jnp.asarray(pair == 0)[None, :], -sin, 0.0)
    s_hi = jnp.where(jnp.asarray(pair == 1)[None, :], sin, 0.0)
    pad = lambda a, v: jnp.concatenate([a, jnp.full((pad_rows, ML_DK), v, F32)], axis=0)
    return pad(cos, 1.0), pad(s_lo, 0.0), pad(s_hi, 0.0)


def _mlprep_kernel(x_ref, xp_ref, xn_ref, w_ref, cos_ref, slo_ref, shi_ref, o_ref, *, n_lat, seq, seq_ctx):
    i, j = pl.program_id(0), pl.program_id(1)
    x = x_ref[...]
    t = x.shape[0]
    nf = ML_DK // 4
    r0 = i * t
    is_lat = r0 < n_lat
    starts = jnp.where(is_lat, r0 % seq == 0, (r0 - n_lat) % seq_ctx == 0)
    ends = jnp.where(is_lat, (r0 + t) % seq == 0, (r0 + t - n_lat) % seq_ctx == 0)
    row = lax.broadcasted_iota(I32, x.shape, 0)
    prev_row = jnp.where(starts, 0.0, xp_ref[7:8, :])
    next_row = jnp.where(ends, 0.0, xn_ref[0:1, :])
    x_prev = jnp.where(row == 0, prev_row, pltpu.roll(x, 1, 0))
    x_next = jnp.where(row == t - 1, next_row, pltpu.roll(x, t - 1, 0))
    w = w_ref[...]
    y = x_prev * w[0:1, :] + x * w[1:2, :] + x_next * w[2:3, :]
    y = jax.nn.silu(y)
    y = y * cos_ref[...] + pltpu.roll(y, ML_DK - nf, 1) * slo_ref[...] + pltpu.roll(y, nf, 1) * shi_ref[...]
    scale = jnp.where(j >= ML_HEADS, ML_DK ** -0.5, 1.0)
    o_ref[...] = (y * scale).astype(o_ref.dtype)


def _ml_prep(p, conv_w, tables, dims):
    m = p.shape[0]
    t = ROW_TILE
    n_lat, l, lc = dims["n_lat"], dims["L"], dims["Lc"]
    hd = ML_DK
    col0 = OFF_ML_QK // hd
    nblk8 = m // 8
    lat_blocks = l // t

    def tab_idx(i, j):
        return (jnp.where(i * t < n_lat, i % lat_blocks, lat_blocks), 0)

    kern = functools.partial(_mlprep_kernel, n_lat=n_lat, seq=l, seq_ctx=lc)
    return pl.pallas_call(
        kern,
        grid=(m // t, 2 * ML_HEADS),
        in_specs=[pl.BlockSpec((t, hd), lambda i, j: (i, col0 + j)),
                  pl.BlockSpec((8, hd), lambda i, j: (jnp.maximum(i * (t // 8) - 1, 0), col0 + j)),
                  pl.BlockSpec((8, hd), lambda i, j: (jnp.minimum((i + 1) * (t // 8), nblk8 - 1), col0 + j)),
                  pl.BlockSpec((conv_w.shape[0], hd), lambda i, j: (0, j)),
                  pl.BlockSpec((t, hd), tab_idx),
                  pl.BlockSpec((t, hd), tab_idx),
                  pl.BlockSpec((t, hd), tab_idx)],
        out_specs=pl.BlockSpec((t, hd), lambda i, j: (i, j)),
        out_shape=jax.ShapeDtypeStruct((m, 2 * ML_QK), BF16),
        compiler_params=_cparams("parallel", "parallel"),
        name="ml_prep",
    )(p, p, p, conv_w, *tables)


def _mlstm_kernel(q_ref, k_ref, v_ref, gc_ref, gr_ref, o_ref, c_ref, n_ref, m_ref):
    d, s = pl.program_id(1), pl.program_id(2)
    t = ML_CHUNK
    nh = ML_HEADS

    @pl.when(s == 0)
    def _():
        c_ref[...] = jnp.zeros_like(c_ref)
        n_ref[...] = jnp.zeros_like(n_ref)
        m_ref[...] = jnp.zeros_like(m_ref)

    fwd = d == 0
    ri = lax.broadcasted_iota(I32, (t, t), 0)
    ci = lax.broadcasted_iota(I32, (t, t), 1)
    tri = (ci - ri) * jnp.where(fwd, 1, -1) <= 0
    trif = tri.astype(F32)
    gc = gc_ref[...]
    gr = gr_ref[...]
    hp = lax.Precision.HIGHEST
    cum_c = jnp.dot(trif, gc, preferred_element_type=F32, precision=hp)
    cum_r = lax.dot_general(gr, trif, (((1,), (1,)), ((), ())), preferred_element_type=F32, precision=hp)
    li_c = jnp.where(fwd, gc[:, 0:nh], gc[:, 2 * nh:3 * nh])
    li_r = jnp.where(fwd, gr[0:nh, :], gr[2 * nh:3 * nh, :])
    lf_r = jnp.where(fwd, gr[nh:2 * nh, :], gr[3 * nh:4 * nh, :])
    b_c = jnp.where(fwd, cum_c[:, nh:2 * nh], cum_c[:, 3 * nh:4 * nh])
    b_r = jnp.where(fwd, cum_r[nh:2 * nh, :], cum_r[3 * nh:4 * nh, :])
    b_l = jnp.sum(lf_r, axis=1, keepdims=True)

    for h in range(nh):
        bc = b_c[:, h:h + 1]
        br = b_r[h:h + 1, :]
        lir = li_r[h:h + 1, :]
        m_h = m_ref[h:h + 1, 0:1]
        a = bc + m_h
        dlog = jnp.where(tri, bc - br + lir, NEG_INF)
        mj = jnp.maximum(a, jnp.max(dlog, axis=-1, keepdims=True))
        w_inter = jnp.exp(a - mj)
        qh = q_ref[:, h * ML_DK:(h + 1) * ML_DK]
        kh = k_ref[:, h * ML_DK:(h + 1) * ML_DK]
        vh = v_ref[:, h * ML_DV:(h + 1) * ML_DV].astype(BF16)
        sm = _nt_dot(qh, kh) * jnp.exp(dlog - mj)
        c_h = c_ref[h]
        n_h = n_ref[h:h + 1, :]
        num = (w_inter * jnp.dot(qh, c_h.astype(BF16), preferred_element_type=F32)
               + jnp.dot(sm.astype(BF16), vh, preferred_element_type=F32))
        den = (w_inter * jnp.sum(qh.astype(F32) * n_h, axis=-1, keepdims=True)
               + jnp.sum(sm, axis=-1, keepdims=True))
        o_ref[:, h * ML_DV:(h + 1) * ML_DV] = num / jnp.maximum(jnp.abs(den), jnp.exp(-mj))
        blh = b_l[h:h + 1, :]
        gl_r = blh - br + lir
        gl_c = blh - bc + li_c[:, h:h + 1]
        m_new = jnp.maximum(blh + m_h, jnp.max(gl_r, axis=-1, keepdims=True))
        sc = jnp.exp(blh + m_h - m_new)
        kw = kh.astype(F32) * jnp.exp(gl_c - m_new)
        c_ref[h] = sc * c_h + lax.dot_general(kw.astype(BF16), vh, (((0,), (0,)), ((), ())),
                                              preferred_element_type=F32)
        n_ref[h:h + 1, :] = sc * n_h + jnp.sum(kw, axis=0, keepdims=True)
        m_ref[h:h + 1, :] = jnp.broadcast_to(m_new, (1, LANES))


def _mlstm_scan(qk, p, gates, gates_t, dims):
    m = qk.shape[0]
    t = ML_CHUNK
    b, l, lc, n_lat = dims["B"], dims["L"], dims["Lc"], dims["n_lat"]
    ncl, ncc = l // t, lc // t

    def chunk(i, d, s):
        in_ctx = s < ncc
        c_ctx = jnp.where(d == 0, s, ncc - 1 - s)
        s_lat = s - ncc
        c_lat = jnp.where(d == 0, s_lat, ncl - 1 - s_lat)
        return jnp.where(in_ctx, n_lat // t + i * ncc + c_ctx, i * ncl + c_lat)

    return pl.pallas_call(
        _mlstm_kernel,
        grid=(b, 2, ncc + ncl),
        in_specs=[pl.BlockSpec((t, ML_QK), lambda i, d, s: (chunk(i, d, s), 0)),
                  pl.BlockSpec((t, ML_QK), lambda i, d, s: (chunk(i, d, s), 1)),
                  pl.BlockSpec((t, ML_V), lambda i, d, s: (chunk(i, d, s), OFF_ML_V // ML_V)),
                  pl.BlockSpec((t, LANES), lambda i, d, s: (chunk(i, d, s), 0)),
                  pl.BlockSpec((4 * ML_HEADS, t), lambda i, d, s: (0, chunk(i, d, s)))],
        out_specs=pl.BlockSpec((None, t, ML_V), lambda i, d, s: (d, chunk(i, d, s), 0)),
        out_shape=jax.ShapeDtypeStruct((2, m, ML_V), F32),
        scratch_shapes=[pltpu.VMEM((ML_HEADS, ML_DK, ML_DV), F32),
                        pltpu.VMEM((ML_HEADS, ML_DK), F32),
                        pltpu.VMEM((ML_HEADS, LANES), F32)],
        compiler_params=_cparams("parallel", "parallel", "arbitrary"),
        name="mlstm_scan",
    )(qk, qk, p, gates, gates_t)


def _mlout_kernel(hf_ref, hb_ref, o_ref, g_ref, y_ref):
    for h in range(ML_HEADS):
        sl = slice(h * ML_DV, (h + 1) * ML_DV)
        x = hf_ref[:, sl] + hb_ref[:, sl]
        y = x * lax.rsqrt(jnp.mean(x * x, axis=-1, keepdims=True) + EPS) * g_ref[:, sl]
        y_ref[:, sl] = (y * jax.nn.sigmoid(o_ref[:, sl])).astype(y_ref.dtype)


def _ml_out(hs, p, gain, rows):
    tm = ROW_TILE
    return pl.pallas_call(
        _mlout_kernel,
        grid=(rows // tm,),
        in_specs=[pl.BlockSpec((None, tm, ML_V), lambda i: (0, i, 0)),
                  pl.BlockSpec((None, tm, ML_V), lambda i: (1, i, 0)),
                  pl.BlockSpec((tm, ML_V), lambda i: (i, OFF_ML_O // ML_V)),
                  pl.BlockSpec((1, ML_V), lambda i: (0, 0))],
        out_specs=pl.BlockSpec((tm, ML_V), lambda i: (i, 0)),
        out_shape=jax.ShapeDtypeStruct((rows, ML_V), BF16),
        compiler_params=_cparams("parallel"),
        name="ml_out",
    )(hs, hs, p, gain.reshape(1, ML_V))


def _gelu_exact(x):
    return 0.5 * x * (1.0 + lax.erf(x * np.float32(np.sqrt(0.5))))


def _gmlp_kernel(u_ref, v_ref, g_ref, ws_ref, bt_ref, o_ref):
    u = _gelu_exact(u_ref[...])
    v = _gelu_exact(v_ref[...])
    vn = (v * lax.rsqrt(jnp.mean(v * v, axis=-1, keepdims=True) + EPS) * g_ref[...]).astype(BF16)
    bt = bt_ref[...]
    for c in range(u.shape[0] // GM_CHUNK):
        rs = slice(c * GM_CHUNK, (c + 1) * GM_CHUNK)
        for g in range(GM_GROUPS):
            cs = slice(g * GM_CPG, (g + 1) * GM_CPG)
            mixed = jnp.dot(ws_ref[g].astype(BF16), vn[rs, cs], preferred_element_type=F32) + bt[:, g:g + 1]
            o_ref[rs, cs] = (u[rs, cs] * mixed).astype(o_ref.dtype)


def _gmlp(p, v_gain, ws, bias, rows):
    tm = ROW_TILE
    return pl.pallas_call(
        _gmlp_kernel,
        grid=(rows // tm,),
        in_specs=[pl.BlockSpec((tm, GM_W), lambda i: (i, OFF_GM_U // GM_W)),
                  pl.BlockSpec((tm, GM_W), lambda i: (i, OFF_GM_V // GM_W)),
                  pl.BlockSpec((1, GM_W), lambda i: (0, 0)),
                  pl.BlockSpec((GM_GROUPS, GM_CHUNK, GM_CHUNK), lambda i: (0, 0, 0)),
                  pl.BlockSpec((GM_CHUNK, GM_GROUPS), lambda i: (0, 0))],
        out_specs=pl.BlockSpec((tm, GM_W), lambda i: (i, 0)),
        out_shape=jax.ShapeDtypeStruct((rows, GM_W), BF16),
        compiler_params=_cparams("parallel"),
        name="gmlp",
    )(p, p, v_gain.reshape(1, GM_W), ws, bias.T)


def _branch_kernel(ya_ref, ym0_ref, ym1_ref, yg_ref, w0_ref, w1_ref, w2_ref, w3_ref,
                   g0_ref, g1_ref, g2_ref, o_ref):
    dot = functools.partial(jnp.dot, preferred_element_type=F32)
    za = dot(ya_ref[...], w0_ref[...])
    zm = dot(ym0_ref[...], w1_ref[...]) + dot(ym1_ref[...], w2_ref[...])
    zg = dot(yg_ref[...], w3_ref[...])
    z = (jax.nn.sigmoid(g0_ref[...]) * za + jax.nn.sigmoid(g1_ref[...]) * zm
         + jax.nn.sigmoid(g2_ref[...]) * zg)
    o_ref[...] = z.astype(o_ref.dtype)


def _branch(ya, ym, yg, w_branch, p, d_model, rows):
    tm, tn = MM_TM, 512
    kb = NA_W
    gate_blk = lambda k: (OFF_GATE + k * d_model) // tn
    return pl.pallas_call(
        _branch_kernel,
        grid=(d_model // tn, rows // tm),
        in_specs=[pl.BlockSpec((tm, kb), lambda j, i: (i, 0)),
                  pl.BlockSpec((tm, kb), lambda j, i: (i, 0)),
                  pl.BlockSpec((tm, kb), lambda j, i: (i, 1)),
                  pl.BlockSpec((tm, kb), lambda j, i: (i, 0)),
                  pl.BlockSpec((kb, tn), lambda j, i: (0, j)),
                  pl.BlockSpec((kb, tn), lambda j, i: (1, j)),
                  pl.BlockSpec((kb, tn), lambda j, i: (2, j)),
                  pl.BlockSpec((kb, tn), lambda j, i: (3, j)),
                  pl.BlockSpec((tm, tn), lambda j, i: (i, gate_blk(0) + j)),
                  pl.BlockSpec((tm, tn), lambda j, i: (i, gate_blk(1) + j)),
                  pl.BlockSpec((tm, tn), lambda j, i: (i, gate_blk(2) + j))],
        out_specs=pl.BlockSpec((tm, tn), lambda j, i: (i, j)),
        out_shape=jax.ShapeDtypeStruct((rows, d_model), BF16),
        compiler_params=_cparams("parallel", "parallel"),
        name="branch_merge",
    )(ya, ym, ym, yg, w_branch, w_branch, w_branch, w_branch, p, p, p)


def _outproj_kernel(z_ref, w_ref, x_ref, gt_ref, o_ref):
    y = jnp.dot(z_ref[...], w_ref[...], preferred_element_type=F32)
    o_ref[...] = x_ref[...] + gt_ref[...] * y


def _outproj(z, w_out, x, mod3, k_gate, dims, rows):
    k = z.shape[1]
    d = w_out.shape[1]
    tm, tn = MM_TM, 512
    mrow = functools.partial(_mod_row, tm=tm, n_lat=dims["n_lat"], seq=dims["L"], n_batch=dims["B"])
    gblk = k_gate * (d // tn)
    return pl.pallas_call(
        _outproj_kernel,
        grid=(d // tn, rows // tm),
        in_specs=[pl.BlockSpec((tm, k), lambda j, i: (i, 0)),
                  pl.BlockSpec((k, tn), lambda j, i: (0, j)),
                  pl.BlockSpec((tm, tn), lambda j, i: (i, j)),
                  pl.BlockSpec((None, 1, tn), lambda j, i: (mrow(i), 0, gblk + j))],
        out_specs=pl.BlockSpec((tm, tn), lambda j, i: (i, j)),
        out_shape=jax.ShapeDtypeStruct((rows, d), F32),
        compiler_params=_cparams("parallel", "parallel"),
        name="out_proj",
    )(z, w_out, x, mod3)


def _affinity_kernel(h_ref, w_ref, o_ref, *, n_exp):
    logits = jnp.dot(h_ref[...].astype(BF16), w_ref[...], preferred_element_type=F32)
    col = lax.broadcasted_iota(I32, logits.shape, 1)
    logits = jnp.where(col < n_exp, logits, NEG_INF)
    o_ref[...] = jax.nn.softmax(logits, axis=-1)


def _affinity(h2, w_router, rows):
    d, e = w_router.shape
    tm = MM_TM
    w_pad = jnp.pad(w_router, ((0, 0), (0, LANES - e))).astype(BF16)
    return pl.pallas_call(
        functools.partial(_affinity_kernel, n_exp=e),
        grid=(rows // tm,),
        in_specs=[pl.BlockSpec((tm, d), lambda i: (i, 0)),
                  pl.BlockSpec((d, LANES), lambda i: (0, 0))],
        out_specs=pl.BlockSpec((tm, LANES), lambda i: (i, 0)),
        out_shape=jax.ShapeDtypeStruct((rows, LANES), F32),
        compiler_params=_cparams("parallel"),
        name="router_affinity",
    )(h2, w_pad)


def _cumsum_rows(x01, blk):
    n = x01.shape[0]
    ri = lax.broadcasted_iota(I32, (blk, blk), 0)
    ci = lax.broadcasted_iota(I32, (blk, blk), 1)
    tril = (ci <= ri).astype(BF16)
    parts = []
    carry = jnp.zeros((1, x01.shape[1]), F32)
    for j in range(n // blk):
        cs = jnp.dot(tril, x01[j * blk:(j + 1) * blk, :].astype(BF16), preferred_element_type=F32) + carry
        parts.append(cs)
        carry = cs[blk - 1:blk, :]
    return jnp.concatenate(parts, axis=0) if len(parts) > 1 else parts[0]


def _select_kernel(aff_ref, idx_ref, gate_ref, rank_ref, *, cap, tb, n_exp):
    aff = aff_ref[...]
    n = aff.shape[0]
    key = lax.bitcast_convert_type(aff, I32)
    capf = jnp.float32(cap)

    def search(i, thr):
        cand = thr | jnp.left_shift(jnp.int32(1), 30 - i)
        cnt = jnp.sum((key >= cand).astype(F32), axis=0, keepdims=True)
        return jnp.where(cnt >= capf, cand, thr)

    thr = lax.fori_loop(0, 31, search, jnp.zeros((1, LANES), I32))
    above = key > thr
    tied = key == thr
    need = capf - jnp.sum(above.astype(F32), axis=0, keepdims=True)
    tied_f = tied.astype(F32)
    tie_rank = _cumsum_rows(tied_f, tb) - tied_f
    sel = above | (tied & (tie_rank < need))
    incl = _cumsum_rows(sel.astype(F32), tb)
    rank_ref[0] = incl
    rank_ref[1] = jnp.where(sel, incl, -1.0)
    rank_ref[2] = aff
    slot = lax.broadcasted_iota(I32, (tb, cap), 1).astype(F32)

    for ex in range(n_exp):
        def block(j, acc):
            acc_i, acc_g = acc
            rows = pl.ds(pl.multiple_of(j * tb, tb), tb)
            inc = rank_ref[0, rows, ex:ex + 1]
            inc_sel = rank_ref[1, rows, ex:ex + 1]
            a = rank_ref[2, rows, ex:ex + 1]
            acc_i = acc_i + jnp.sum(jnp.where(inc <= slot, 1.0, 0.0), axis=0, keepdims=True)
            acc_g = acc_g + jnp.sum(jnp.where(inc_sel == slot + 1.0, a, 0.0), axis=0, keepdims=True)
            return acc_i, acc_g

        zero = jnp.zeros((1, cap), F32)
        acc_i, acc_g = lax.fori_loop(0, n // tb, block, (zero, zero))
        idx_ref[ex:ex + 1, :] = acc_i.astype(I32)
        gate_ref[ex:ex + 1, :] = acc_g


def _select(aff, e, n_groups, group_len, blk0, cap):
    tb = min(256, group_len)
    kern = functools.partial(_select_kernel, cap=cap, tb=tb, n_exp=e)
    return pl.pallas_call(
        kern,
        grid=(n_groups,),
        in_specs=[pl.BlockSpec((group_len, LANES), lambda g: (blk0 + g, 0))],
        out_specs=[pl.BlockSpec((None, e, cap), lambda g: (g, 0, 0)),
                   pl.BlockSpec((None, e, cap), lambda g: (g, 0, 0))],
        out_shape=[jax.ShapeDtypeStruct((n_groups, e, cap), I32),
                   jax.ShapeDtypeStruct((n_groups, e, cap), F32)],
        scratch_shapes=[pltpu.VMEM((3, group_len, LANES), F32)],
        compiler_params=_cparams("parallel"),
        name="router_select",
    )(aff)


def _moe_up_kernel(idx_ref, h_hbm, wg_ref, wu_ref, o_ref, xs_ref, xb_ref, sem, *, n_rows):
    e, f = pl.program_id(0), pl.program_id(1)

    def row_copy(i):
        src = idx_ref[e * n_rows + i]
        return pltpu.make_async_copy(h_hbm.at[pl.ds(src, 1), :], xs_ref.at[pl.ds(i, 1), :], sem)

    @pl.when(f == 0)
    def _():
        def start(i, c):
            row_copy(i).start()
            return c

        def wait(i, c):
            row_copy(i).wait()
            return c

        lax.fori_loop(0, n_rows, start, 0)
        lax.fori_loop(0, n_rows, wait, 0)
        xb_ref[...] = xs_ref[...].astype(BF16)

    xb = xb_ref[...]
    a = jnp.dot(xb, wg_ref[...].astype(BF16), preferred_element_type=F32)
    u = jnp.dot(xb, wu_ref[...].astype(BF16), preferred_element_type=F32)
    o_ref[...] = (jax.nn.silu(a) * u).astype(o_ref.dtype)


def _moe_up(idx_flat, h2, w_gate, w_up, n_rows):
    e, d, ff = w_gate.shape
    tf = 256
    kern = functools.partial(_moe_up_kernel, n_rows=n_rows)
    return pl.pallas_call(
        kern,
        grid_spec=pltpu.PrefetchScalarGridSpec(
            num_scalar_prefetch=1,
            grid=(e, ff // tf),
            in_specs=[pl.BlockSpec(memory_space=pl.ANY),
                      pl.BlockSpec((None, d, tf), lambda i, f, idx: (i, 0, f)),
                      pl.BlockSpec((None, d, tf), lambda i, f, idx: (i, 0, f))],
            out_specs=pl.BlockSpec((None, n_rows, tf), lambda i, f, idx: (i, 0, f)),
            scratch_shapes=[pltpu.VMEM((n_rows, d), F32), pltpu.VMEM((n_rows, d), BF16),
                            pltpu.SemaphoreType.DMA(())]),
        out_shape=jax.ShapeDtypeStruct((e, n_rows, ff), BF16),
        compiler_params=_cparams("arbitrary", "arbitrary"),
        name="moe_up",
    )(idx_flat, h2, w_gate, w_up)


def _moe_down_kernel(idx_ref, hid_ref, gcol_ref, wd_ref, x_ref, gt_ref, o_ref, y_ref, *, cap, n_exp):
    g, e = pl.program_id(0), pl.program_id(2)

    @pl.when(e == 0)
    def _():
        o_ref[...] = x_ref[...]

    y = jnp.dot(hid_ref[...], wd_ref[...].astype(BF16), preferred_element_type=F32)
    y_ref[...] = y * gcol_ref[...] * gt_ref[...]
    base = (g * n_exp + e) * cap

    def add_row(i, c):
        t = idx_ref[base + i]
        o_ref[pl.ds(t, 1), :] = o_ref[pl.ds(t, 1), :] + y_ref[pl.ds(i, 1), :]
        return c

    lax.fori_loop(0, cap, add_row, 0, unroll=8)


def _moe_down(idx_flat, hid, gate_col, w_down, x, mod3, k_gate, n_groups, group_len, cap,
              hid_row0, x_row0, mod_row_of_group, out_rows):
    e, ff, d = w_down.shape
    dc = 512
    kern = functools.partial(_moe_down_kernel, cap=cap, n_exp=e)
    hblk0 = hid_row0 // cap
    xblk0 = x_row0 // group_len
    gblk = k_gate * (d // dc)
    return pl.pallas_call(
        kern,
        grid_spec=pltpu.PrefetchScalarGridSpec(
            num_scalar_prefetch=1,
            grid=(n_groups, d // dc, e),
            in_specs=[pl.BlockSpec((None, cap, ff), lambda g, j, i, idx: (i, hblk0 + g, 0)),
                      pl.BlockSpec((None, cap, 1), lambda g, j, i, idx: (i, hblk0 + g, 0)),
                      pl.BlockSpec((None, ff, dc), lambda g, j, i, idx: (i, 0, j)),
                      pl.BlockSpec((group_len, dc), lambda g, j, i, idx: (xblk0 + g, j)),
                      pl.BlockSpec((None, 1, dc), lambda g, j, i, idx: (mod_row_of_group(g), 0, gblk + j))],
            out_specs=pl.BlockSpec((group_len, dc), lambda g, j, i, idx: (xblk0 + g, j)),
            scratch_shapes=[pltpu.VMEM((cap, dc), F32)]),
        out_shape=jax.ShapeDtypeStruct((out_rows, d), F32),
        input_output_aliases={4: 0},
        compiler_params=_cparams("parallel", "parallel", "arbitrary"),
        name="moe_down",
    )(idx_flat, hid, gate_col, w_down, x, mod3)


def _pack_in_proj(w_in_l, d_model):
    sizes = (NA_W, NA_W, NA_W, ML_QK, ML_QK, ML_V, ML_V, 4 * ML_HEADS, GM_W, GM_W, N_BRANCH * d_model)
    offs = np.concatenate([[0], np.cumsum(sizes)])
    seg = lambda k: w_in_l[:, offs[k]:offs[k + 1]]
    order = (0, 1, 2, 8, 5, 6, 3, 4, 9, 10)
    w_main = jnp.concatenate([seg(k).astype(BF16) for k in order], axis=1)
    w_g = jnp.pad(seg(7), ((0, 0), (0, LANES - 4 * ML_HEADS))).astype(BF16)
    return w_main, w_g


def _layer(x, mod_l, last, dims, w_in, na_q_gain, na_k_gain, na_rel_bias, ml_conv, ml_gate_bias, ml_out_gain,
           gm_v_gain, gm_ws, gm_bias, w_branch, w_out, g_norm1, g_norm2, w_router, w_e_gate, w_e_up, w_e_down,
           rope_tabs):
    b, l, lc, n_lat = dims["B"], dims["L"], dims["Lc"], dims["n_lat"]
    m, d = x.shape
    rows = n_lat if last else m
    mod3 = mod_l.reshape(mod_l.shape[0], 1, mod_l.shape[1])
    n_exp = w_router.shape[1]

    h = _prenorm(x, g_norm1, mod3, 0, 1, dims, m, BF16)
    w_main, w_g = _pack_in_proj(w_in, d)
    tn = 1024 if w_main.shape[1] % 1024 == 0 else 512
    p = _matmul(h, w_main, tn, F32, "in_proj")
    b_g = jnp.pad(ml_gate_bias.reshape(1, 4 * ML_HEADS), ((0, 0), (0, LANES - 4 * ML_HEADS)))
    gates = _ml_gates(h, w_g, b_g)
    gates_t = gates[:, :4 * ML_HEADS].T

    kr = min(NA_ROWS, l // GRID_W)
    ya = _na_latent(p, na_q_gain, na_k_gain, _na_bias_table(na_rel_bias, kr), dims)
    if not last:
        ya = jnp.concatenate([ya, _na_context(p, na_q_gain, na_k_gain, dims)], axis=0)

    qk = _ml_prep(p, ml_conv, rope_tabs, dims)
    hs = _mlstm_scan(qk, p, gates, gates_t, dims)
    ym = _ml_out(hs, p, ml_out_gain, rows)
    yg = _gmlp(p, gm_v_gain, gm_ws, gm_bias, rows)

    z = _branch(ya, ym, yg, w_branch.astype(BF16), p, d, rows)
    x1 = _outproj(z, w_out.astype(BF16), x, mod3, 2, dims, rows)

    h2 = _prenorm(x1, g_norm2, mod3, 3, 4, dims, rows, F32)
    aff = _affinity(h2, w_router, rows)
    cap_l = max(1, min(l, (CAPACITY_FACTOR * l) // n_exp))
    idx_l, gate_l = _select(aff, n_exp, b, l, 0, cap_l)
    row_l = idx_l + (jnp.arange(b, dtype=I32) * l)[:, None, None]
    src = [jnp.swapaxes(row_l, 0, 1).reshape(n_exp, b * cap_l)]
    gsel = [jnp.swapaxes(gate_l, 0, 1).reshape(n_exp, b * cap_l)]
    if not last:
        cap_c = max(1, min(lc, (CAPACITY_FACTOR * lc) // n_exp))
        idx_c, gate_c = _select(aff, n_exp, b, lc, n_lat // lc, cap_c)
        row_c = idx_c + (n_lat + jnp.arange(b, dtype=I32) * lc)[:, None, None]
        src.append(jnp.swapaxes(row_c, 0, 1).reshape(n_exp, b * cap_c))
        gsel.append(jnp.swapaxes(gate_c, 0, 1).reshape(n_exp, b * cap_c))
    src = jnp.concatenate(src, axis=1)
    n_rows = src.shape[1]
    gate_col = jnp.concatenate(gsel, axis=1).reshape(n_exp, n_rows, 1)
    hid = _moe_up(src.reshape(-1), h2, w_e_gate, w_e_up, n_rows)
    x2 = _moe_down(idx_l.reshape(-1), hid, gate_col, w_e_down, x1, mod3, 5, b, l, cap_l,
                   0, 0, lambda g: g, rows)
    if not last:
        x2 = _moe_down(idx_c.reshape(-1), hid, gate_col, w_e_down, x2, mod3, 5, b, lc, cap_c,
                       b * cap_l, n_lat, lambda g: b, rows)
    return x2


def kernel(x, c, ctx, c_ctx, w_ada, b_ada, g_norm1, g_norm2, w_in, na_q_gain, na_k_gain, na_rel_bias, ml_conv,
           ml_gate_bias, ml_out_gain, gm_v_gain, gm_ws, gm_bias, w_branch, w_out, w_router, w_e_gate, w_e_up,
           w_e_down):
    b, l, d = x.shape
    lc = ctx.shape[1]
    depth = w_ada.shape[0]
    dims = {"B": b, "L": l, "Lc": lc, "n_lat": b * l}
    assert b + 1 <= 8 and l % ROW_TILE == 0 and lc % ROW_TILE == 0 and (b * l) % MM_TM == 0 and (b * lc) % MM_TM == 0

    c8 = jnp.concatenate([c, c_ctx[None, :], jnp.zeros((8 - b - 1, d), F32)], axis=0)
    mod = _ada(c8, w_ada, b_ada)
    xs = jnp.concatenate([x.reshape(b * l, d), ctx.reshape(b * lc, d)], axis=0)
    rope_tabs = _rope_tables(l, ROW_TILE)
    for i in range(depth):
        xs = _layer(xs, mod[i], i == depth - 1, dims, w_in[i], na_q_gain[i], na_k_gain[i], na_rel_bias[i],
                    ml_conv[i], ml_gate_bias[i], ml_out_gain[i], gm_v_gain[i], gm_ws[i], gm_bias[i], w_branch[i],
                    w_out[i], g_norm1[i], g_norm2[i], w_router[i], w_e_gate[i], w_e_up[i], w_e_down[i], rope_tabs)
    return xs[:b * l].reshape(b, l, d)
```

```python
import functools

import numpy as np
import jax
import jax.numpy as jnp
from jax import lax
from jax.experimental import pallas as pl
from jax.experimental.pallas import tpu as pltpu

F32 = jnp.float32
BF16 = jnp.bfloat16
I32 = jnp.int32

GRID_W = 64
NA_HEADS, NA_DH, NA_ROWS, NA_COLS = 8, 128, 8, 16
NA_W = NA_HEADS * NA_DH
ML_HEADS, ML_DK, ML_DV, ML_CHUNK = 8, 128, 256, 128
ML_QK, ML_V = ML_HEADS * ML_DK, ML_HEADS * ML_DV
GM_GROUPS, GM_CPG, GM_CHUNK = 8, 128, 128
GM_W = GM_GROUPS * GM_CPG
N_BRANCH = 3
CAPACITY_FACTOR = 2
ROPE_BASE = 10000.0
EPS = 1e-6
NEG_INF = -1e30

VMEM_LIMIT_BYTES = 56 * 1024 * 1024
LANES = 128
ROW_TILE = 256
MM_TM = 512
SCATTER_BATCH = 8

OFF_NA_Q, OFF_NA_K, OFF_NA_V = 0, NA_W, 2 * NA_W
OFF_ML_QK = 3 * NA_W
OFF_ML_V = OFF_ML_QK + 2 * ML_QK
OFF_ML_O = OFF_ML_V + ML_V
OFF_ML_GATES = OFF_ML_O + ML_V
N_ML_GATES = 4 * ML_HEADS
OFF_GM_U = OFF_ML_GATES
OFF_GM_V = OFF_GM_U + GM_W
OFF_GATE = OFF_GM_V + GM_W
HALF_W = 1024


def _cparams(*sem):
    return pltpu.CompilerParams(dimension_semantics=sem, vmem_limit_bytes=VMEM_LIMIT_BYTES)


def _nt_dot(a, b):
    return lax.dot_general(a, b, (((1,), (1,)), ((), ())), preferred_element_type=F32)


def _ada_kernel(c_ref, w_ref, b_ref, o_ref):
    a = jax.nn.silu(c_ref[...]).astype(BF16)
    o_ref[...] = jnp.dot(a, w_ref[...].astype(BF16), preferred_element_type=F32) + b_ref[...]


def _ada(c8, w_ada, b_ada):
    depth, d, n = w_ada.shape
    tn = 512
    return pl.pallas_call(
        _ada_kernel,
        grid=(depth, n // tn),
        in_specs=[pl.BlockSpec((8, d), lambda l, j: (0, 0)),
                  pl.BlockSpec((None, d, tn), lambda l, j: (l, 0, j)),
                  pl.BlockSpec((None, 1, tn), lambda l, j: (l, 0, j))],
        out_specs=pl.BlockSpec((None, 8, tn), lambda l, j: (l, 0, j)),
        out_shape=jax.ShapeDtypeStruct((depth, 8, n), F32),
        compiler_params=_cparams("parallel", "parallel"),
        name="ada_mod",
    )(c8, w_ada, b_ada.reshape(depth, 1, n))


def _mod_row(i, tm, n_lat, seq, n_batch):
    r = i * tm
    return jnp.where(r >= n_lat, n_batch, r // seq)


HI16 = np.uint32(0xFFFF0000)


def _pack_bf16_pairs(y):
    half = y.shape[1] // 2
    bits = lax.bitcast_convert_type(y.astype(BF16).astype(F32), jnp.uint32)
    return (bits[:, half:] & HI16) | (bits[:, :half] >> 16)


def _unpack_bf16_pairs(packed):
    lo = lax.bitcast_convert_type(packed << 16, F32).astype(BF16)
    hi = lax.bitcast_convert_type(packed & HI16, F32).astype(BF16)
    return jnp.concatenate([lo, hi], axis=1)


def _prenorm_kernel(x_ref, g_ref, sh_ref, sc_ref, o_ref, *, packed):
    x = x_ref[...]
    y = x * lax.rsqrt(jnp.mean(x * x, axis=-1, keepdims=True) + EPS)
    y = y * g_ref[...]
    y = y * (1 + sc_ref[...]) + sh_ref[...]
    o_ref[...] = _pack_bf16_pairs(y) if packed else y.astype(o_ref.dtype)


def _prenorm(x, g, mod3, k_shift, k_scale, dims, rows, packed):
    d = x.shape[1]
    tm = ROW_TILE
    d_out = d // 2 if packed else d
    mrow = functools.partial(_mod_row, tm=tm, n_lat=dims["n_lat"], seq=dims["L"], n_batch=dims["B"])
    return pl.pallas_call(
        functools.partial(_prenorm_kernel, packed=packed),
        grid=(rows // tm,),
        in_specs=[pl.BlockSpec((tm, d), lambda i: (i, 0)),
                  pl.BlockSpec((1, d), lambda i: (0, 0)),
                  pl.BlockSpec((None, 1, d), lambda i: (mrow(i), 0, k_shift)),
                  pl.BlockSpec((None, 1, d), lambda i: (mrow(i), 0, k_scale))],
        out_specs=pl.BlockSpec((tm, d_out), lambda i: (i, 0)),
        out_shape=jax.ShapeDtypeStruct((rows, d_out), jnp.uint32 if packed else BF16),
        compiler_params=_cparams("parallel"),
        name="prenorm",
    )(x, g.reshape(1, d), mod3, mod3)


IN_TN = 512
K_CHUNK = 512


def _inproj_kernel(a_ref, wm_ref, wt_ref, o_ref, wp_ref, *, n_aligned, shift):
    j, i = pl.program_id(0), pl.program_id(1)
    k, tn = wm_ref.shape

    @pl.when(jnp.logical_and(i == 0, j < n_aligned))
    def _():
        for c in range(k // K_CHUNK):
            rows = slice(c * K_CHUNK, (c + 1) * K_CHUNK)
            wp_ref[rows, :] = wm_ref[rows, :].astype(BF16)

    @pl.when(jnp.logical_and(i == 0, j >= n_aligned))
    def _():
        for c in range(k // K_CHUNK):
            rows = slice(c * K_CHUNK, (c + 1) * K_CHUNK)
            w = jnp.concatenate([wm_ref[rows, :], wt_ref[rows, :]], axis=1)
            wp_ref[rows, :] = pltpu.roll(w, tn + LANES - shift, 1)[:, :tn].astype(BF16)

    o_ref[...] = jnp.dot(a_ref[...], wp_ref[...], preferred_element_type=F32)


def _in_proj(h, w_in, layer):
    m, k = h.shape
    n_w = w_in.shape[2]
    n_out = n_w - N_ML_GATES
    tm, tn = MM_TM, IN_TN
    assert OFF_ML_GATES % tn == 0 and n_out % tn == 0 and N_ML_GATES < LANES
    kern = functools.partial(_inproj_kernel, n_aligned=OFF_ML_GATES // tn, shift=N_ML_GATES)
    return pl.pallas_call(
        kern,
        grid=(n_out // tn, m // tm),
        in_specs=[pl.BlockSpec((tm, k), lambda j, i: (i, 0)),
                  pl.BlockSpec((None, k, tn), lambda j, i: (layer, 0, j)),
                  pl.BlockSpec((None, k, LANES), lambda j, i: (layer, 0, (tn // LANES) * (j + 1)))],
        out_specs=pl.BlockSpec((tm, tn), lambda j, i: (i, j)),
        out_shape=jax.ShapeDtypeStruct((m, n_out), F32),
        scratch_shapes=[pltpu.VMEM((k, tn), BF16)],
        compiler_params=_cparams("parallel", "arbitrary"),
        name="in_proj",
    )(h, w_in, w_in)


def _gates_kernel(a_ref, w_ref, b_ref, o_ref):
    g = jnp.dot(a_ref[...], w_ref[...].astype(BF16), preferred_element_type=F32) + b_ref[...]
    col = lax.broadcasted_iota(I32, g.shape, 1)
    is_forget = ((col // ML_HEADS) % 2) == 1
    o_ref[...] = jnp.where(is_forget, jax.nn.log_sigmoid(g), g)


def _ml_gates(h, w_in, layer, b_g):
    m, k = h.shape
    tm = MM_TM
    return pl.pallas_call(
        _gates_kernel,
        grid=(m // tm,),
        in_specs=[pl.BlockSpec((tm, k), lambda i: (i, 0)),
                  pl.BlockSpec((None, k, LANES), lambda i: (layer, 0, OFF_ML_GATES // LANES)),
                  pl.BlockSpec((1, LANES), lambda i: (0, 0))],
        out_specs=pl.BlockSpec((tm, LANES), lambda i: (i, 0)),
        out_shape=jax.ShapeDtypeStruct((m, LANES), F32),
        compiler_params=_cparams("parallel"),
        name="ml_gates",
    )(h, w_in, b_g)


def _head_rms(x, g):
    return x * lax.rsqrt(jnp.mean(x * x, axis=-1, keepdims=True) + EPS) * g


def _na_kernel(q_ref, k_ref, v_ref, kc_ref, vc_ref, qg_ref, kg_ref, bias_ref, o_ref,
               qn_ref, kn_ref, vn_ref, *, rows, kr):
    w = GRID_W
    nk = kr * w
    scale = NA_DH ** -0.5
    qn_ref[...] = _head_rms(q_ref[...], qg_ref[...]).astype(BF16)
    kn_ref[...] = _head_rms(k_ref[...], kg_ref[...]).astype(BF16)
    vn_ref[...] = v_ref[...].astype(BF16)
    kc = _head_rms(kc_ref[...], kg_ref[...]).astype(BF16)
    vc = vc_ref[...].astype(BF16)
    qcol = lax.broadcasted_iota(I32, (w, nk), 0)
    kcol = lax.broadcasted_iota(I32, (w, nk), 1) % w
    c0 = jnp.clip(qcol - NA_COLS // 2, 0, w - NA_COLS)
    in_win = (kcol >= c0) & (kcol < c0 + NA_COLS)

    def body(r, carry):
        r0 = jnp.clip(r - kr // 2, 0, rows - kr)
        off = r0 - r + (kr - 1)
        qr = qn_ref[pl.ds(pl.multiple_of(r * w, w), w), :]
        kw = kn_ref[pl.ds(pl.multiple_of(r0 * w, w), nk), :]
        vw = vn_ref[pl.ds(pl.multiple_of(r0 * w, w), nk), :]
        s_lat = _nt_dot(qr, kw) * scale + bias_ref[off]
        s_lat = jnp.where(in_win, s_lat, NEG_INF)
        s_ctx = _nt_dot(qr, kc) * scale
        m = jnp.maximum(jnp.max(s_lat, axis=-1, keepdims=True), jnp.max(s_ctx, axis=-1, keepdims=True))
        p_lat = jnp.exp(s_lat - m)
        p_ctx = jnp.exp(s_ctx - m)
        den = jnp.sum(p_lat, axis=-1, keepdims=True) + jnp.sum(p_ctx, axis=-1, keepdims=True)
        out = (jnp.dot(p_lat.astype(BF16), vw, preferred_element_type=F32)
               + jnp.dot(p_ctx.astype(BF16), vc, preferred_element_type=F32))
        o_ref[pl.ds(pl.multiple_of(r * w, w), w), :] = (out / den).astype(o_ref.dtype)
        return carry

    lax.fori_loop(0, rows, body, 0, unroll=2)


def _na_bias_table(rel_bias, kr):
    w = GRID_W
    n_dc = 2 * NA_COLS - 1
    dc = np.clip(np.arange(w)[None, :] - np.arange(w)[:, None] + NA_COLS - 1, 0, n_dc - 1)
    onehot = jnp.asarray((dc[:, :, None] == np.arange(n_dc)).astype(np.float32))
    toep = jnp.einsum("hrd,ckd->hrck", rel_bias.astype(F32), onehot, precision=lax.Precision.HIGHEST)
    r_lo = NA_ROWS - kr
    tab = jnp.stack([toep[:, r_lo + off:r_lo + off + kr] for off in range(kr)], axis=1)
    return jnp.transpose(tab, (0, 1, 3, 2, 4)).reshape(rel_bias.shape[0], kr, w, kr * w)


def _na_latent(p, q_gain, k_gain, bias_tab, dims):
    b, l, lc = dims["B"], dims["L"], dims["Lc"]
    rows = l // GRID_W
    kr = min(NA_ROWS, rows)
    nk = kr * GRID_W
    hd = NA_DH
    kern = functools.partial(_na_kernel, rows=rows, kr=kr)
    ctx_blk = (b * l) // lc
    return pl.pallas_call(
        kern,
        grid=(b, NA_HEADS),
        in_specs=[pl.BlockSpec((l, hd), lambda i, h: (i, OFF_NA_Q // hd + h)),
                  pl.BlockSpec((l, hd), lambda i, h: (i, OFF_NA_K // hd + h)),
                  pl.BlockSpec((l, hd), lambda i, h: (i, OFF_NA_V // hd + h)),
                  pl.BlockSpec((lc, hd), lambda i, h: (ctx_blk + i, OFF_NA_K // hd + h)),
                  pl.BlockSpec((lc, hd), lambda i, h: (ctx_blk + i, OFF_NA_V // hd + h)),
                  pl.BlockSpec((1, hd), lambda i, h: (0, 0)),
                  pl.BlockSpec((1, hd), lambda i, h: (0, 0)),
                  pl.BlockSpec((None, kr, GRID_W, nk), lambda i, h: (h, 0, 0, 0))],
        out_specs=pl.BlockSpec((l, hd), lambda i, h: (i, h)),
        out_shape=jax.ShapeDtypeStruct((b * l, NA_W), BF16),
        scratch_shapes=[pltpu.VMEM((l, hd), BF16), pltpu.VMEM((l, hd), BF16), pltpu.VMEM((l, hd), BF16)],
        compiler_params=_cparams("parallel", "parallel"),
        name="na_latent",
    )(p, p, p, p, p, q_gain.reshape(1, hd), k_gain.reshape(1, hd), bias_tab)


def _ctx_attn_kernel(q_ref, k_ref, v_ref, qg_ref, kg_ref, o_ref):
    q = _head_rms(q_ref[...], qg_ref[...]).astype(BF16)
    k = _head_rms(k_ref[...], kg_ref[...]).astype(BF16)
    s = _nt_dot(q, k) * (NA_DH ** -0.5)
    m = jnp.max(s, axis=-1, keepdims=True)
    pr = jnp.exp(s - m)
    den = jnp.sum(pr, axis=-1, keepdims=True)
    out = jnp.dot(pr.astype(BF16), v_ref[...].astype(BF16), preferred_element_type=F32)
    o_ref[...] = (out / den).astype(o_ref.dtype)


def _na_context(p, q_gain, k_gain, dims):
    b, l, lc = dims["B"], dims["L"], dims["Lc"]
    hd = NA_DH
    ctx_blk = (b * l) // lc
    return pl.pallas_call(
        _ctx_attn_kernel,
        grid=(b, NA_HEADS),
        in_specs=[pl.BlockSpec((lc, hd), lambda i, h: (ctx_blk + i, OFF_NA_Q // hd + h)),
                  pl.BlockSpec((lc, hd), lambda i, h: (ctx_blk + i, OFF_NA_K // hd + h)),
                  pl.BlockSpec((lc, hd), lambda i, h: (ctx_blk + i, OFF_NA_V // hd + h)),
                  pl.BlockSpec((1, hd), lambda i, h: (0, 0)),
                  pl.BlockSpec((1, hd), lambda i, h: (0, 0))],
        out_specs=pl.BlockSpec((lc, hd), lambda i, h: (i, h)),
        out_shape=jax.ShapeDtypeStruct((b * lc, NA_W), BF16),
        compiler_params=_cparams("parallel", "parallel"),
        name="na_context",
    )(p, p, p, q_gain.reshape(1, hd), k_gain.reshape(1, hd))


def _rope_tables(seq, pad_rows):
    nf = ML_DK // 4
    t = np.arange(seq)
    pos = np.stack([t // GRID_W, t % GRID_W], axis=-1).astype(np.float32)
    inv_freq = (ROPE_BASE ** (-np.arange(nf, dtype=np.float32) / nf)).astype(np.float32)
    lane = np.arange(ML_DK)
    axis, pair, f = lane // (2 * nf), (lane // nf) % 2, lane % nf
    ang = jnp.asarray(pos[:, axis]) * jnp.asarray(inv_freq[f])[None, :]
    cos, sin = jnp.cos(ang), jnp.sin(ang)
    s_lo = jnp.where(jnp.asarray(pair == 0)[None, :], -sin, 0.0)
    s_hi = jnp.where(jnp.asarray(pair == 1)[None, :], sin, 0.0)
    pad = lambda a, v: jnp.concatenate([a, jnp.full((pad_rows, ML_DK), v, F32)], axis=0)
    return pad(cos, 1.0), pad(s_lo, 0.0), pad(s_hi, 0.0)


def _mlprep_kernel(x_ref, xp_ref, xn_ref, w_ref, cos_ref, slo_ref, shi_ref, o_ref, *, n_lat, seq, seq_ctx):
    i, j = pl.program_id(0), pl.program_id(1)
    x = x_ref[...]
    t = x.shape[0]
    nf = ML_DK // 4
    r0 = i * t
    is_lat = r0 < n_lat
    starts = jnp.where(is_lat, r0 % seq == 0, (r0 - n_lat) % seq_ctx == 0)
    ends = jnp.where(is_lat, (r0 + t) % seq == 0, (r0 + t - n_lat) % seq_ctx == 0)
    row = lax.broadcasted_iota(I32, x.shape, 0)
    prev_row = jnp.where(starts, 0.0, xp_ref[7:8, :])
    next_row = jnp.where(ends, 0.0, xn_ref[0:1, :])
    x_prev = jnp.where(row == 0, prev_row, pltpu.roll(x, 1, 0))
    x_next = jnp.where(row == t - 1, next_row, pltpu.roll(x, t - 1, 0))
    w = w_ref[...]
    y = x_prev * w[0:1, :] + x * w[1:2, :] + x_next * w[2:3, :]
    y = jax.nn.silu(y)
    scale = jnp.where(j >= 1, ML_DK ** -0.5, 1.0)
    cos, s_lo, s_hi = cos_ref[...], slo_ref[...], shi_ref[...]
    for h in range(ML_HEADS):
        cs = slice(h * ML_DK, (h + 1) * ML_DK)
        yh = y[:, cs]
        yh = yh * cos + pltpu.roll(yh, ML_DK - nf, 1) * s_lo + pltpu.roll(yh, nf, 1) * s_hi
        o_ref[:, cs] = (yh * scale).astype(o_ref.dtype)


def _ml_prep(p, conv_w, tables, dims):
    m = p.shape[0]
    t = ROW_TILE
    n_lat, l, lc = dims["n_lat"], dims["L"], dims["Lc"]
    hd = ML_DK
    wb = ML_QK
    col0 = OFF_ML_QK // wb
    nblk8 = m // 8
    lat_blocks = l // t

    def tab_idx(i, j):
        return (jnp.where(i * t < n_lat, i % lat_blocks, lat_blocks), 0)

    kern = functools.partial(_mlprep_kernel, n_lat=n_lat, seq=l, seq_ctx=lc)
    return pl.pallas_call(
        kern,
        grid=(m // t, 2),
        in_specs=[pl.BlockSpec((t, wb), lambda i, j: (i, col0 + j)),
                  pl.BlockSpec((8, wb), lambda i, j: (jnp.maximum(i * (t // 8) - 1, 0), col0 + j)),
                  pl.BlockSpec((8, wb), lambda i, j: (jnp.minimum((i + 1) * (t // 8), nblk8 - 1), col0 + j)),
                  pl.BlockSpec((conv_w.shape[0], wb), lambda i, j: (0, j)),
                  pl.BlockSpec((t, hd), tab_idx),
                  pl.BlockSpec((t, hd), tab_idx),
                  pl.BlockSpec((t, hd), tab_idx)],
        out_specs=pl.BlockSpec((t, wb), lambda i, j: (i, j)),
        out_shape=jax.ShapeDtypeStruct((m, 2 * ML_QK), BF16),
        compiler_params=_cparams("parallel", "parallel"),
        name="ml_prep",
    )(p, p, p, conv_w, *tables)


def _mlstm_kernel(q_ref, k_ref, v0_ref, v1_ref, gc_ref, gr_ref, o_ref, c_ref, n_ref, m_ref):
    d, s = pl.program_id(1), pl.program_id(2)
    t = ML_CHUNK
    nh = ML_HEADS

    @pl.when(s == 0)
    def _():
        c_ref[...] = jnp.zeros_like(c_ref)
        n_ref[...] = jnp.zeros_like(n_ref)
        m_ref[...] = jnp.zeros_like(m_ref)

    fwd = d == 0
    ri = lax.broadcasted_iota(I32, (t, t), 0)
    ci = lax.broadcasted_iota(I32, (t, t), 1)
    tri = (ci - ri) * jnp.where(fwd, 1, -1) <= 0
    trif = tri.astype(F32)
    gc = gc_ref[...]
    gr = gr_ref[...]
    hp = lax.Precision.HIGHEST
    cum_c = jnp.dot(trif, gc, preferred_element_type=F32, precision=hp)
    cum_r = lax.dot_general(gr, trif, (((1,), (1,)), ((), ())), preferred_element_type=F32, precision=hp)
    li_c = jnp.where(fwd, gc[:, 0:nh], gc[:, 2 * nh:3 * nh])
    li_r = jnp.where(fwd, gr[0:nh, :], gr[2 * nh:3 * nh, :])
    lf_r = jnp.where(fwd, gr[nh:2 * nh, :], gr[3 * nh:4 * nh, :])
    b_c = jnp.where(fwd, cum_c[:, nh:2 * nh], cum_c[:, 3 * nh:4 * nh])
    b_r = jnp.where(fwd, cum_r[nh:2 * nh, :], cum_r[3 * nh:4 * nh, :])
    b_l = jnp.sum(lf_r, axis=1, keepdims=True)

    for h in range(nh):
        bc = b_c[:, h:h + 1]
        br = b_r[h:h + 1, :]
        lir = li_r[h:h + 1, :]
        m_h = m_ref[h:h + 1, 0:1]
        a = bc + m_h
        dlog = jnp.where(tri, bc - br + lir, NEG_INF)
        mj = jnp.maximum(a, jnp.max(dlog, axis=-1, keepdims=True))
        w_inter = jnp.exp(a - mj)
        qh = q_ref[:, h * ML_DK:(h + 1) * ML_DK]
        kh = k_ref[:, h * ML_DK:(h + 1) * ML_DK]
        hv = h % (nh // 2)
        vh = (v0_ref if h < nh // 2 else v1_ref)[:, hv * ML_DV:(hv + 1) * ML_DV].astype(BF16)
        sm = _nt_dot(qh, kh) * jnp.exp(dlog - mj)
        c_h = c_ref[h]
        n_h = n_ref[h:h + 1, :]
        num = (w_inter * jnp.dot(qh, c_h.astype(BF16), preferred_element_type=F32)
               + jnp.dot(sm.astype(BF16), vh, preferred_element_type=F32))
        den = (w_inter * jnp.sum(qh.astype(F32) * n_h, axis=-1, keepdims=True)
               + jnp.sum(sm, axis=-1, keepdims=True))
        o_ref[:, h * ML_DV:(h + 1) * ML_DV] = num / jnp.maximum(jnp.abs(den), jnp.exp(-mj))
        blh = b_l[h:h + 1, :]
        gl_r = blh - br + lir
        gl_c = blh - bc + li_c[:, h:h + 1]
        m_new = jnp.maximum(blh + m_h, jnp.max(gl_r, axis=-1, keepdims=True))
        sc = jnp.exp(blh + m_h - m_new)
        kw = kh.astype(F32) * jnp.exp(gl_c - m_new)
        c_ref[h] = sc * c_h + lax.dot_general(kw.astype(BF16), vh, (((0,), (0,)), ((), ())),
                                              preferred_element_type=F32)
        n_ref[h:h + 1, :] = sc * n_h + jnp.sum(kw, axis=0, keepdims=True)
        m_ref[h:h + 1, :] = jnp.broadcast_to(m_new, (1, LANES))


def _mlstm_scan(qk, p, gates, gates_t, dims):
    m = qk.shape[0]
    t = ML_CHUNK
    b, l, lc, n_lat = dims["B"], dims["L"], dims["Lc"], dims["n_lat"]
    ncl, ncc = l // t, lc // t

    def chunk(i, d, s):
        in_ctx = s < ncc
        c_ctx = jnp.where(d == 0, s, ncc - 1 - s)
        s_lat = s - ncc
        c_lat = jnp.where(d == 0, s_lat, ncl - 1 - s_lat)
        return jnp.where(in_ctx, n_lat // t + i * ncc + c_ctx, i * ncl + c_lat)

    return pl.pallas_call(
        _mlstm_kernel,
        grid=(b, 2, ncc + ncl),
        in_specs=[pl.BlockSpec((t, ML_QK), lambda i, d, s: (chunk(i, d, s), 0)),
                  pl.BlockSpec((t, ML_QK), lambda i, d, s: (chunk(i, d, s), 1)),
                  pl.BlockSpec((t, HALF_W), lambda i, d, s: (chunk(i, d, s), OFF_ML_V // HALF_W)),
                  pl.BlockSpec((t, HALF_W), lambda i, d, s: (chunk(i, d, s), OFF_ML_V // HALF_W + 1)),
                  pl.BlockSpec((t, LANES), lambda i, d, s: (chunk(i, d, s), 0)),
                  pl.BlockSpec((4 * ML_HEADS, t), lambda i, d, s: (0, chunk(i, d, s)))],
        out_specs=pl.BlockSpec((None, t, ML_V), lambda i, d, s: (d, chunk(i, d, s), 0)),
        out_shape=jax.ShapeDtypeStruct((2, m, ML_V), F32),
        scratch_shapes=[pltpu.VMEM((ML_HEADS, ML_DK, ML_DV), F32),
                        pltpu.VMEM((ML_HEADS, ML_DK), F32),
                        pltpu.VMEM((ML_HEADS, LANES), F32)],
        compiler_params=_cparams("parallel", "parallel", "arbitrary"),
        name="mlstm_scan",
    )(qk, qk, p, p, gates, gates_t)


def _mlout_kernel(hf_ref, hb_ref, og_ref, g_ref, y_ref):
    for h in range(HALF_W // ML_DV):
        sl = slice(h * ML_DV, (h + 1) * ML_DV)
        x = hf_ref[:, sl] + hb_ref[:, sl]
        y = x * lax.rsqrt(jnp.mean(x * x, axis=-1, keepdims=True) + EPS) * g_ref[:, sl]
        y_ref[:, sl] = (y * jax.nn.sigmoid(og_ref[:, sl])).astype(y_ref.dtype)


def _ml_out(hs, p, gain, rows):
    tm = ROW_TILE
    wb = HALF_W
    return pl.pallas_call(
        _mlout_kernel,
        grid=(rows // tm, ML_V // wb),
        in_specs=[pl.BlockSpec((None, tm, wb), lambda i, j: (0, i, j)),
                  pl.BlockSpec((None, tm, wb), lambda i, j: (1, i, j)),
                  pl.BlockSpec((tm, wb), lambda i, j: (i, OFF_ML_O // wb + j)),
                  pl.BlockSpec((1, wb), lambda i, j: (0, j))],
        out_specs=pl.BlockSpec((tm, wb), lambda i, j: (i, j)),
        out_shape=jax.ShapeDtypeStruct((rows, ML_V), BF16),
        compiler_params=_cparams("parallel", "parallel"),
        name="ml_out",
    )(hs, hs, p, gain.reshape(1, ML_V))


def _gelu_exact(x):
    return 0.5 * x * (1.0 + lax.erf(x * np.float32(np.sqrt(0.5))))


def _gmlp_kernel(u_ref, v_ref, g_ref, ws_ref, bt_ref, o_ref):
    u = _gelu_exact(u_ref[...])
    v = _gelu_exact(v_ref[...])
    vn = (v * lax.rsqrt(jnp.mean(v * v, axis=-1, keepdims=True) + EPS) * g_ref[...]).astype(BF16)
    bt = bt_ref[...]
    for c in range(u.shape[0] // GM_CHUNK):
        rs = slice(c * GM_CHUNK, (c + 1) * GM_CHUNK)
        for g in range(GM_GROUPS):
            cs = slice(g * GM_CPG, (g + 1) * GM_CPG)
            mixed = jnp.dot(ws_ref[g].astype(BF16), vn[rs, cs], preferred_element_type=F32) + bt[:, g:g + 1]
            o_ref[rs, cs] = (u[rs, cs] * mixed).astype(o_ref.dtype)


def _gmlp(p, v_gain, ws, bias, rows):
    tm = ROW_TILE
    return pl.pallas_call(
        _gmlp_kernel,
        grid=(rows // tm,),
        in_specs=[pl.BlockSpec((tm, GM_W), lambda i: (i, OFF_GM_U // GM_W)),
                  pl.BlockSpec((tm, GM_W), lambda i: (i, OFF_GM_V // GM_W)),
                  pl.BlockSpec((1, GM_W), lambda i: (0, 0)),
                  pl.BlockSpec((GM_GROUPS, GM_CHUNK, GM_CHUNK), lambda i: (0, 0, 0)),
                  pl.BlockSpec((GM_CHUNK, GM_GROUPS), lambda i: (0, 0))],
        out_specs=pl.BlockSpec((tm, GM_W), lambda i: (i, 0)),
        out_shape=jax.ShapeDtypeStruct((rows, GM_W), BF16),
        compiler_params=_cparams("parallel"),
        name="gmlp",
    )(p, p, v_gain.reshape(1, GM_W), ws, bias.T)


def _branch_kernel(ya_ref, ym_ref, yg_ref, w_ref, g0_ref, g1_ref, g2_ref, o_ref, wp_ref):
    @pl.when(pl.program_id(1) == 0)
    def _():
        wp_ref[...] = w_ref[...].astype(BF16)

    dot = functools.partial(jnp.dot, preferred_element_type=F32)
    za = dot(ya_ref[...], wp_ref[0:NA_W, :])
    zm = dot(ym_ref[...], wp_ref[NA_W:NA_W + ML_V, :])
    zg = dot(yg_ref[...], wp_ref[NA_W + ML_V:, :])
    z = (jax.nn.sigmoid(g0_ref[...]) * za + jax.nn.sigmoid(g1_ref[...]) * zm
         + jax.nn.sigmoid(g2_ref[...]) * zg)
    o_ref[...] = z.astype(o_ref.dtype)


def _branch(ya, ym, yg, w_branch, layer, p, d_model, rows):
    tm, tn = MM_TM, 512
    k = w_branch.shape[1]
    gate_blk = lambda b: (OFF_GATE + b * d_model) // tn
    return pl.pallas_call(
        _branch_kernel,
        grid=(d_model // tn, rows // tm),
        in_specs=[pl.BlockSpec((tm, NA_W), lambda j, i: (i, 0)),
                  pl.BlockSpec((tm, ML_V), lambda j, i: (i, 0)),
                  pl.BlockSpec((tm, GM_W), lambda j, i: (i, 0)),
                  pl.BlockSpec((None, k, tn), lambda j, i: (layer, 0, j)),
                  pl.BlockSpec((tm, tn), lambda j, i: (i, gate_blk(0) + j)),
                  pl.BlockSpec((tm, tn), lambda j, i: (i, gate_blk(1) + j)),
                  pl.BlockSpec((tm, tn), lambda j, i: (i, gate_blk(2) + j))],
        out_specs=pl.BlockSpec((tm, tn), lambda j, i: (i, j)),
        out_shape=jax.ShapeDtypeStruct((rows, d_model), BF16),
        scratch_shapes=[pltpu.VMEM((k, tn), BF16)],
        compiler_params=_cparams("parallel", "arbitrary"),
        name="branch_merge",
    )(ya, ym, yg, w_branch, p, p, p)


def _outproj_kernel(z_ref, w_ref, x_ref, gt_ref, o_ref, wp_ref):
    @pl.when(pl.program_id(1) == 0)
    def _():
        wp_ref[...] = w_ref[...].astype(BF16)

    y = jnp.dot(z_ref[...], wp_ref[...], preferred_element_type=F32)
    o_ref[...] = x_ref[...] + gt_ref[...] * y


def _outproj(z, w_out, layer, x, mod3, k_gate, dims, rows):
    k = z.shape[1]
    d = w_out.shape[2]
    tm, tn = MM_TM, 512
    mrow = functools.partial(_mod_row, tm=tm, n_lat=dims["n_lat"], seq=dims["L"], n_batch=dims["B"])
    gblk = k_gate * (d // tn)
    return pl.pallas_call(
        _outproj_kernel,
        grid=(d // tn, rows // tm),
        in_specs=[pl.BlockSpec((tm, k), lambda j, i: (i, 0)),
                  pl.BlockSpec((None, k, tn), lambda j, i: (layer, 0, j)),
                  pl.BlockSpec((tm, tn), lambda j, i: (i, j)),
                  pl.BlockSpec((None, 1, tn), lambda j, i: (mrow(i), 0, gblk + j))],
        out_specs=pl.BlockSpec((tm, tn), lambda j, i: (i, j)),
        out_shape=jax.ShapeDtypeStruct((rows, d), F32),
        scratch_shapes=[pltpu.VMEM((k, tn), BF16)],
        compiler_params=_cparams("parallel", "arbitrary"),
        name="out_proj",
    )(z, w_out, x, mod3)


def _affinity_kernel(h_ref, w_ref, o_ref, *, n_exp):
    logits = jnp.dot(_unpack_bf16_pairs(h_ref[...]), w_ref[...], preferred_element_type=F32)
    col = lax.broadcasted_iota(I32, logits.shape, 1)
    logits = jnp.where(col < n_exp, logits, NEG_INF)
    o_ref[...] = jax.nn.softmax(logits, axis=-1)


def _affinity(h2, w_router, rows):
    d, e = w_router.shape
    tm = MM_TM
    w_pad = jnp.pad(w_router, ((0, 0), (0, LANES - e))).astype(BF16)
    return pl.pallas_call(
        functools.partial(_affinity_kernel, n_exp=e),
        grid=(rows // tm,),
        in_specs=[pl.BlockSpec((tm, d // 2), lambda i: (i, 0)),
                  pl.BlockSpec((d, LANES), lambda i: (0, 0))],
        out_specs=pl.BlockSpec((tm, LANES), lambda i: (i, 0)),
        out_shape=jax.ShapeDtypeStruct((rows, LANES), F32),
        compiler_params=_cparams("parallel"),
        name="router_affinity",
    )(h2, w_pad)


def _cumsum_rows(x01, blk):
    n = x01.shape[0]
    ri = lax.broadcasted_iota(I32, (blk, blk), 0)
    ci = lax.broadcasted_iota(I32, (blk, blk), 1)
    tril = (ci <= ri).astype(BF16)
    parts = []
    carry = jnp.zeros((1, x01.shape[1]), F32)
    for j in range(n // blk):
        cs = jnp.dot(tril, x01[j * blk:(j + 1) * blk, :].astype(BF16), preferred_element_type=F32) + carry
        parts.append(cs)
        carry = cs[blk - 1:blk, :]
    return jnp.concatenate(parts, axis=0) if len(parts) > 1 else parts[0]


def _select_kernel(aff_ref, idx_ref, gate_ref, rank_ref, *, cap, tb, n_exp):
    aff = aff_ref[...]
    n = aff.shape[0]
    key = lax.bitcast_convert_type(aff, I32)
    capf = jnp.float32(cap)

    def search(i, thr):
        cand = thr | jnp.left_shift(jnp.int32(1), 30 - i)
        cnt = jnp.sum((key >= cand).astype(F32), axis=0, keepdims=True)
        return jnp.where(cnt >= capf, cand, thr)

    thr = lax.fori_loop(0, 31, search, jnp.zeros((1, LANES), I32))
    above = key > thr
    tied = key == thr
    need = capf - jnp.sum(above.astype(F32), axis=0, keepdims=True)
    tied_f = tied.astype(F32)
    tie_rank = _cumsum_rows(tied_f, tb) - tied_f
    sel = above | (tied & (tie_rank < need))
    incl = _cumsum_rows(sel.astype(F32), tb)
    rank_ref[0] = incl
    rank_ref[1] = jnp.where(sel, incl, -1.0)
    rank_ref[2] = aff
    slot = lax.broadcasted_iota(I32, (tb, cap), 1).astype(F32)

    for ex in range(n_exp):
        def block(j, acc):
            acc_i, acc_g = acc
            rows = pl.ds(pl.multiple_of(j * tb, tb), tb)
            inc = rank_ref[0, rows, ex:ex + 1]
            inc_sel = rank_ref[1, rows, ex:ex + 1]
            a = rank_ref[2, rows, ex:ex + 1]
            acc_i = acc_i + jnp.sum(jnp.where(inc <= slot, 1.0, 0.0), axis=0, keepdims=True)
            acc_g = acc_g + jnp.sum(jnp.where(inc_sel == slot + 1.0, a, 0.0), axis=0, keepdims=True)
            return acc_i, acc_g

        zero = jnp.zeros((1, cap), F32)
        acc_i, acc_g = lax.fori_loop(0, n // tb, block, (zero, zero))
        idx_ref[ex:ex + 1, :] = acc_i.astype(I32)
        gate_ref[ex:ex + 1, :] = acc_g


def _select(aff, e, n_groups, group_len, blk0, cap):
    tb = min(256, group_len)
    kern = functools.partial(_select_kernel, cap=cap, tb=tb, n_exp=e)
    return pl.pallas_call(
        kern,
        grid=(n_groups,),
        in_specs=[pl.BlockSpec((group_len, LANES), lambda g: (blk0 + g, 0))],
        out_specs=[pl.BlockSpec((None, e, cap), lambda g: (g, 0, 0)),
                   pl.BlockSpec((None, e, cap), lambda g: (g, 0, 0))],
        out_shape=[jax.ShapeDtypeStruct((n_groups, e, cap), I32),
                   jax.ShapeDtypeStruct((n_groups, e, cap), F32)],
        scratch_shapes=[pltpu.VMEM((3, group_len, LANES), F32)],
        compiler_params=_cparams("parallel"),
        name="router_select",
    )(aff)


def _moe_up_kernel(idx_ref, h_hbm, wg_ref, wu_ref, o_ref, xs_ref, xb_ref, sem, *, n_rows, n_exp, n_f):
    e, f = pl.program_id(0), pl.program_id(1)
    per_step = n_rows // n_f

    def row_copy(ex, i):
        slot = ex % 2
        src = idx_ref[ex * n_rows + i]
        return pltpu.make_async_copy(h_hbm.at[pl.ds(src, 1), :], xs_ref.at[slot, pl.ds(i, 1), :], sem.at[slot])

    def start_rows(ex, lo, n):
        def start(i, c):
            row_copy(ex, lo + i).start()
            return c
        lax.fori_loop(0, n, start, 0, unroll=4)

    @pl.when(jnp.logical_and(e == 0, f == 0))
    def _():
        start_rows(0, 0, n_rows)

    @pl.when(f == 0)
    def _():
        def wait(i, c):
            row_copy(e, i).wait()
            return c
        lax.fori_loop(0, n_rows, wait, 0, unroll=4)
        xb_ref[...] = _unpack_bf16_pairs(xs_ref[e % 2])

    @pl.when(e + 1 < n_exp)
    def _():
        start_rows(e + 1, f * per_step, per_step)

    xb = xb_ref[...]
    a = jnp.dot(xb, wg_ref[...].astype(BF16), preferred_element_type=F32)
    u = jnp.dot(xb, wu_ref[...].astype(BF16), preferred_element_type=F32)
    o_ref[...] = (jax.nn.silu(a) * u).astype(o_ref.dtype)


def _moe_up(idx_flat, h2_packed, w_gate, w_up, layer, n_rows):
    _, e, d, ff = w_gate.shape
    tf = 256
    n_f = ff // tf
    assert n_rows % n_f == 0
    kern = functools.partial(_moe_up_kernel, n_rows=n_rows, n_exp=e, n_f=n_f)
    return pl.pallas_call(
        kern,
        grid_spec=pltpu.PrefetchScalarGridSpec(
            num_scalar_prefetch=1,
            grid=(e, n_f),
            in_specs=[pl.BlockSpec(memory_space=pl.ANY),
                      pl.BlockSpec((None, None, d, tf), lambda i, f, idx: (layer, i, 0, f)),
                      pl.BlockSpec((None, None, d, tf), lambda i, f, idx: (layer, i, 0, f))],
            out_specs=pl.BlockSpec((None, n_rows, tf), lambda i, f, idx: (i, 0, f)),
            scratch_shapes=[pltpu.VMEM((2, n_rows, d // 2), jnp.uint32), pltpu.VMEM((n_rows, d), BF16),
                            pltpu.SemaphoreType.DMA((2,))]),
        out_shape=jax.ShapeDtypeStruct((e, n_rows, ff), BF16),
        compiler_params=_cparams("arbitrary", "arbitrary"),
        name="moe_up",
    )(idx_flat, h2_packed, w_gate, w_up)


def _moe_down_kernel(idx_ref, hid_ref, gcol_ref, wd_ref, gt_ref, x_hbm, o_hbm, acc_ref, y_ref, sem, *,
                     cap, n_exp, group_len, row0, dc):
    g, j, e = pl.program_id(0), pl.program_id(1), pl.program_id(2)
    rows = pl.ds(pl.multiple_of(row0 + g * group_len, group_len), group_len)
    cols = pl.ds(pl.multiple_of(j * dc, dc), dc)
    load = pltpu.make_async_copy(x_hbm.at[rows, cols], acc_ref, sem.at[0])
    store = pltpu.make_async_copy(acc_ref, o_hbm.at[rows, cols], sem.at[1])

    @pl.when(e == 0)
    def _():
        load.start()

    y = jnp.dot(hid_ref[...], wd_ref[...].astype(BF16), preferred_element_type=F32)
    y_ref[...] = y * gcol_ref[...] * gt_ref[...]

    @pl.when(e == 0)
    def _():
        load.wait()

    base = (g * n_exp + e) * cap

    def add_rows(bi, c):
        i0 = pl.multiple_of(bi * SCATTER_BATCH, SCATTER_BATCH)
        toks = [idx_ref[base + i0 + r] for r in range(SCATTER_BATCH)]
        acc = [acc_ref[pl.ds(t, 1), :] for t in toks]
        for r in range(SCATTER_BATCH):
            acc_ref[pl.ds(toks[r], 1), :] = acc[r] + y_ref[pl.ds(i0 + r, 1), :]
        return c

    lax.fori_loop(0, cap // SCATTER_BATCH, add_rows, 0)

    @pl.when(e == n_exp - 1)
    def _():
        store.start()
        store.wait()


def _moe_down(idx_flat, hid, gate_col, w_down, layer, x, mod3, k_gate, n_groups, group_len, cap,
              hid_row0, x_row0, mod_row_of_group, out_rows):
    _, e, ff, d = w_down.shape
    assert cap % SCATTER_BATCH == 0 and x_row0 % group_len == 0
    dc = 1024
    kern = functools.partial(_moe_down_kernel, cap=cap, n_exp=e, group_len=group_len, row0=x_row0, dc=dc)
    hblk0 = hid_row0 // cap
    gblk = k_gate * (d // dc)
    return pl.pallas_call(
        kern,
        grid_spec=pltpu.PrefetchScalarGridSpec(
            num_scalar_prefetch=1,
            grid=(n_groups, d // dc, e),
            in_specs=[pl.BlockSpec((None, cap, ff), lambda g, j, i, idx: (i, hblk0 + g, 0)),
                      pl.BlockSpec((None, cap, 1), lambda g, j, i, idx: (i, hblk0 + g, 0)),
                      pl.BlockSpec((None, None, ff, dc), lambda g, j, i, idx: (layer, i, 0, j)),
                      pl.BlockSpec((None, 1, dc), lambda g, j, i, idx: (mod_row_of_group(g), 0, gblk + j)),
                      pl.BlockSpec(memory_space=pl.ANY)],
            out_specs=pl.BlockSpec(memory_space=pl.ANY),
            scratch_shapes=[pltpu.VMEM((group_len, dc), F32), pltpu.VMEM((cap, dc), F32),
                            pltpu.SemaphoreType.DMA((2,))]),
        out_shape=jax.ShapeDtypeStruct((out_rows, d), F32),
        input_output_aliases={5: 0},
        compiler_params=_cparams("arbitrary", "arbitrary", "arbitrary"),
        name="moe_down",
    )(idx_flat, hid, gate_col, w_down, mod3, x)


def _layer(x, mod_l, layer, last, dims, w_in, na_q_gain, na_k_gain, na_rel_bias, ml_conv, ml_gate_bias,
           ml_out_gain, gm_v_gain, gm_ws, gm_bias, w_branch, w_out, g_norm1, g_norm2, w_router, w_e_gate, w_e_up,
           w_e_down, rope_tabs):
    b, l, lc, n_lat = dims["B"], dims["L"], dims["Lc"], dims["n_lat"]
    m, d = x.shape
    rows = n_lat if last else m
    mod3 = mod_l.reshape(mod_l.shape[0], 1, mod_l.shape[1])
    n_exp = w_router.shape[1]

    h = _prenorm(x, g_norm1, mod3, 0, 1, dims, m, False)
    p = _in_proj(h, w_in, layer)
    b_g = jnp.pad(ml_gate_bias.reshape(1, N_ML_GATES), ((0, 0), (0, LANES - N_ML_GATES)))
    gates = _ml_gates(h, w_in, layer, b_g)
    gates_t = gates[:, :N_ML_GATES].T

    kr = min(NA_ROWS, l // GRID_W)
    ya = _na_latent(p, na_q_gain, na_k_gain, _na_bias_table(na_rel_bias, kr), dims)
    if not last:
        ya = jnp.concatenate([ya, _na_context(p, na_q_gain, na_k_gain, dims)], axis=0)

    qk = _ml_prep(p, ml_conv, rope_tabs, dims)
    hs = _mlstm_scan(qk, p, gates, gates_t, dims)
    ym = _ml_out(hs, p, ml_out_gain, rows)
    yg = _gmlp(p, gm_v_gain, gm_ws, gm_bias, rows)

    z = _branch(ya, ym, yg, w_branch, layer, p, d, rows)
    x1 = _outproj(z, w_out, layer, x, mod3, 2, dims, rows)

    h2 = _prenorm(x1, g_norm2, mod3, 3, 4, dims, rows, True)
    aff = _affinity(h2, w_router, rows)
    cap_l = max(1, min(l, (CAPACITY_FACTOR * l) // n_exp))
    idx_l, gate_l = _select(aff, n_exp, b, l, 0, cap_l)
    row_l = idx_l + (jnp.arange(b, dtype=I32) * l)[:, None, None]
    src = [jnp.swapaxes(row_l, 0, 1).reshape(n_exp, b * cap_l)]
    gsel = [jnp.swapaxes(gate_l, 0, 1).reshape(n_exp, b * cap_l)]
    if not last:
        cap_c = max(1, min(lc, (CAPACITY_FACTOR * lc) // n_exp))
        idx_c, gate_c = _select(aff, n_exp, b, lc, n_lat // lc, cap_c)
        row_c = idx_c + (n_lat + jnp.arange(b, dtype=I32) * lc)[:, None, None]
        src.append(jnp.swapaxes(row_c, 0, 1).reshape(n_exp, b * cap_c))
        gsel.append(jnp.swapaxes(gate_c, 0, 1).reshape(n_exp, b * cap_c))
    src = jnp.concatenate(src, axis=1)
    n_rows = src.shape[1]
    gate_col = jnp.concatenate(gsel, axis=1).reshape(n_exp, n_rows, 1)
    hid = _moe_up(src.reshape(-1), h2, w_e_gate, w_e_up, layer, n_rows)
    x2 = _moe_down(idx_l.reshape(-1), hid, gate_col, w_e_down, layer, x1, mod3, 5, b, l, cap_l,
                   0, 0, lambda g: g, rows)
    if not last:
        x2 = _moe_down(idx_c.reshape(-1), hid, gate_col, w_e_down, layer, x2, mod3, 5, b, lc, cap_c,
                       b * cap_l, n_lat, lambda g: b, rows)
    return x2


def kernel(x, c, ctx, c_ctx, w_ada, b_ada, g_norm1, g_norm2, w_in, na_q_gain, na_k_gain, na_rel_bias, ml_conv,
           ml_gate_bias, ml_out_gain, gm_v_gain, gm_ws, gm_bias, w_branch, w_out, w_router, w_e_gate, w_e_up,
           w_e_down):
    b, l, d = x.shape
    lc = ctx.shape[1]
    depth = w_ada.shape[0]
    dims = {"B": b, "L": l, "Lc": lc, "n_lat": b * l}
    assert b + 1 <= 8 and l % ROW_TILE == 0 and lc % ROW_TILE == 0 and (b * l) % MM_TM == 0 and (b * lc) % MM_TM == 0

    c8 = jnp.concatenate([c, c_ctx[None, :], jnp.zeros((8 - b - 1, d), F32)], axis=0)
    mod = _ada(c8, w_ada, b_ada)
    xs = jnp.concatenate([x.reshape(b * l, d), ctx.reshape(b * lc, d)], axis=0)
    rope_tabs = _rope_tables(l, ROW_TILE)
    for i in range(depth):
        xs = _layer(xs, mod[i], i, i == depth - 1, dims, w_in, na_q_gain[i], na_k_gain[i], na_rel_bias[i],
                    ml_conv[i], ml_gate_bias[i], ml_out_gain[i], gm_v_gain[i], gm_ws[i], gm_bias[i], w_branch,
                    w_out, g_norm1[i], g_norm2[i], w_router[i], w_e_gate, w_e_up, w_e_down, rope_tabs)
    return xs[:b * l].reshape(b, l, d)
```

```python
import functools

import numpy as np
import jax
import jax.numpy as jnp
from jax import lax
from jax.experimental import pallas as pl
from jax.experimental.pallas import tpu as pltpu

F32 = jnp.float32
BF16 = jnp.bfloat16
I32 = jnp.int32

GRID_W = 64
NA_HEADS, NA_DH, NA_ROWS, NA_COLS = 8, 128, 8, 16
NA_W = NA_HEADS * NA_DH
ML_HEADS, ML_DK, ML_DV, ML_CHUNK = 8, 128, 256, 128
ML_QK, ML_V = ML_HEADS * ML_DK, ML_HEADS * ML_DV
GM_GROUPS, GM_CPG, GM_CHUNK = 8, 128, 128
GM_W = GM_GROUPS * GM_CPG
N_BRANCH = 3
CAPACITY_FACTOR = 2
ROPE_BASE = 10000.0
EPS = 1e-6
NEG_INF = -1e30

VMEM_LIMIT_BYTES = 56 * 1024 * 1024
LANES = 128
ROW_TILE = 256
MM_TM = 512
SCATTER_BATCH = 8

OFF_NA_Q, OFF_NA_K, OFF_NA_V = 0, NA_W, 2 * NA_W
OFF_ML_QK = 3 * NA_W
OFF_ML_V = OFF_ML_QK + 2 * ML_QK
OFF_ML_O = OFF_ML_V + ML_V
OFF_ML_GATES = OFF_ML_O + ML_V
N_ML_GATES = 4 * ML_HEADS
OFF_GM_U = OFF_ML_GATES
OFF_GM_V = OFF_GM_U + GM_W
OFF_GATE = OFF_GM_V + GM_W
HALF_W = 1024


def _cparams(*sem):
    return pltpu.CompilerParams(dimension_semantics=sem, vmem_limit_bytes=VMEM_LIMIT_BYTES)


def _nt_dot(a, b):
    return lax.dot_general(a, b, (((1,), (1,)), ((), ())), preferred_element_type=F32)


def _ada_kernel(c_ref, w_ref, b_ref, o_ref):
    a = jax.nn.silu(c_ref[...]).astype(BF16)
    o_ref[...] = jnp.dot(a, w_ref[...].astype(BF16), preferred_element_type=F32) + b_ref[...]


def _ada(c8, w_ada, b_ada):
    depth, d, n = w_ada.shape
    tn = 512
    return pl.pallas_call(
        _ada_kernel,
        grid=(depth, n // tn),
        in_specs=[pl.BlockSpec((8, d), lambda l, j: (0, 0)),
                  pl.BlockSpec((None, d, tn), lambda l, j: (l, 0, j)),
                  pl.BlockSpec((None, 1, tn), lambda l, j: (l, 0, j))],
        out_specs=pl.BlockSpec((None, 8, tn), lambda l, j: (l, 0, j)),
        out_shape=jax.ShapeDtypeStruct((depth, 8, n), F32),
        compiler_params=_cparams("parallel", "parallel"),
        name="ada_mod",
    )(c8, w_ada, b_ada.reshape(depth, 1, n))


def _mod_row(i, tm, n_lat, seq, n_batch):
    r = i * tm
    return jnp.where(r >= n_lat, n_batch, r // seq)


HI16 = np.uint32(0xFFFF0000)


def _pack_bf16_pairs(y):
    half = y.shape[1] // 2
    bits = lax.bitcast_convert_type(y.astype(BF16).astype(F32), jnp.uint32)
    return (bits[:, half:] & HI16) | (bits[:, :half] >> 16)


def _unpack_bf16_pairs(packed):
    lo = lax.bitcast_convert_type(packed << 16, F32).astype(BF16)
    hi = lax.bitcast_convert_type(packed & HI16, F32).astype(BF16)
    return jnp.concatenate([lo, hi], axis=1)


def _prenorm_kernel(x_ref, g_ref, sh_ref, sc_ref, o_ref, *, packed):
    x = x_ref[...]
    y = x * lax.rsqrt(jnp.mean(x * x, axis=-1, keepdims=True) + EPS)
    y = y * g_ref[...]
    y = y * (1 + sc_ref[...]) + sh_ref[...]
    o_ref[...] = _pack_bf16_pairs(y) if packed else y.astype(o_ref.dtype)


def _prenorm(x, g, mod3, k_shift, k_scale, dims, rows, packed):
    d = x.shape[1]
    tm = ROW_TILE
    d_out = d // 2 if packed else d
    mrow = functools.partial(_mod_row, tm=tm, n_lat=dims["n_lat"], seq=dims["L"], n_batch=dims["B"])
    return pl.pallas_call(
        functools.partial(_prenorm_kernel, packed=packed),
        grid=(rows // tm,),
        in_specs=[pl.BlockSpec((tm, d), lambda i: (i, 0)),
                  pl.BlockSpec((1, d), lambda i: (0, 0)),
                  pl.BlockSpec((None, 1, d), lambda i: (mrow(i), 0, k_shift)),
                  pl.BlockSpec((None, 1, d), lambda i: (mrow(i), 0, k_scale))],
        out_specs=pl.BlockSpec((tm, d_out), lambda i: (i, 0)),
        out_shape=jax.ShapeDtypeStruct((rows, d_out), jnp.uint32 if packed else BF16),
        compiler_params=_cparams("parallel"),
        name="prenorm",
    )(x, g.reshape(1, d), mod3, mod3)


IN_TN = 1024
K_CHUNK = 512


def _inproj_kernel(a_ref, wm_ref, wt_ref, o_ref, wp_ref, *, n_aligned, shift):
    j, i = pl.program_id(0), pl.program_id(1)
    k, tn = wm_ref.shape

    @pl.when(jnp.logical_and(i == 0, j < n_aligned))
    def _():
        for c in range(k // K_CHUNK):
            rows = slice(c * K_CHUNK, (c + 1) * K_CHUNK)
            wp_ref[rows, :] = wm_ref[rows, :].astype(BF16)

    @pl.when(jnp.logical_and(i == 0, j >= n_aligned))
    def _():
        for c in range(k // K_CHUNK):
            rows = slice(c * K_CHUNK, (c + 1) * K_CHUNK)
            w = jnp.concatenate([wm_ref[rows, :], wt_ref[rows, :]], axis=1).astype(F32)
            wp_ref[rows, :] = pltpu.roll(w, tn + LANES - shift, 1)[:, :tn].astype(BF16)

    o_ref[...] = jnp.dot(a_ref[...], wp_ref[...], preferred_element_type=F32)


def _in_proj(h, w_in, layer):
    m, k = h.shape
    n_w = w_in.shape[2]
    n_out = n_w - N_ML_GATES
    tm, tn = MM_TM, IN_TN
    assert OFF_ML_GATES % tn == 0 and n_out % tn == 0 and N_ML_GATES < LANES
    kern = functools.partial(_inproj_kernel, n_aligned=OFF_ML_GATES // tn, shift=N_ML_GATES)
    return pl.pallas_call(
        kern,
        grid=(n_out // tn, m // tm),
        in_specs=[pl.BlockSpec((tm, k), lambda j, i: (i, 0)),
                  pl.BlockSpec((None, k, tn), lambda j, i: (layer, 0, j)),
                  pl.BlockSpec((None, k, LANES), lambda j, i: (layer, 0, (tn // LANES) * (j + 1)))],
        out_specs=pl.BlockSpec((tm, tn), lambda j, i: (i, j)),
        out_shape=jax.ShapeDtypeStruct((m, n_out), F32),
        scratch_shapes=[pltpu.VMEM((k, tn), BF16)],
        compiler_params=_cparams("parallel", "arbitrary"),
        name="in_proj",
    )(h, w_in, w_in)


def _gates_kernel(a_ref, w_ref, b_ref, o_ref):
    g = jnp.dot(a_ref[...], w_ref[...].astype(BF16), preferred_element_type=F32) + b_ref[...]
    col = lax.broadcasted_iota(I32, g.shape, 1)
    is_forget = ((col // ML_HEADS) % 2) == 1
    o_ref[...] = jnp.where(is_forget, jax.nn.log_sigmoid(g), g)


def _ml_gates(h, w_in, layer, b_g):
    m, k = h.shape
    tm = MM_TM
    return pl.pallas_call(
        _gates_kernel,
        grid=(m // tm,),
        in_specs=[pl.BlockSpec((tm, k), lambda i: (i, 0)),
                  pl.BlockSpec((None, k, LANES), lambda i: (layer, 0, OFF_ML_GATES // LANES)),
                  pl.BlockSpec((1, LANES), lambda i: (0, 0))],
        out_specs=pl.BlockSpec((tm, LANES), lambda i: (i, 0)),
        out_shape=jax.ShapeDtypeStruct((m, LANES), F32),
        compiler_params=_cparams("parallel"),
        name="ml_gates",
    )(h, w_in, b_g)


def _head_rms(x, g):
    return x * lax.rsqrt(jnp.mean(x * x, axis=-1, keepdims=True) + EPS) * g


def _na_kernel(q_ref, k_ref, v_ref, kc_ref, vc_ref, qg_ref, kg_ref, bias_ref, o_ref,
               qn_ref, kn_ref, vn_ref, *, rows, kr):
    w = GRID_W
    nk = kr * w
    scale = NA_DH ** -0.5
    qn_ref[...] = _head_rms(q_ref[...], qg_ref[...]).astype(BF16)
    kn_ref[...] = _head_rms(k_ref[...], kg_ref[...]).astype(BF16)
    vn_ref[...] = v_ref[...].astype(BF16)
    kc = _head_rms(kc_ref[...], kg_ref[...]).astype(BF16)
    vc = vc_ref[...].astype(BF16)
    qcol = lax.broadcasted_iota(I32, (w, nk), 0)
    kcol = lax.broadcasted_iota(I32, (w, nk), 1) % w
    c0 = jnp.clip(qcol - NA_COLS // 2, 0, w - NA_COLS)
    in_win = (kcol >= c0) & (kcol < c0 + NA_COLS)

    def body(r, carry):
        r0 = jnp.clip(r - kr // 2, 0, rows - kr)
        off = r0 - r + (kr - 1)
        qr = qn_ref[pl.ds(pl.multiple_of(r * w, w), w), :]
        kw = kn_ref[pl.ds(pl.multiple_of(r0 * w, w), nk), :]
        vw = vn_ref[pl.ds(pl.multiple_of(r0 * w, w), nk), :]
        s_lat = _nt_dot(qr, kw) * scale + bias_ref[off]
        s_lat = jnp.where(in_win, s_lat, NEG_INF)
        s_ctx = _nt_dot(qr, kc) * scale
        m = jnp.maximum(jnp.max(s_lat, axis=-1, keepdims=True), jnp.max(s_ctx, axis=-1, keepdims=True))
        p_lat = jnp.exp(s_lat - m)
        p_ctx = jnp.exp(s_ctx - m)
        den = jnp.sum(p_lat, axis=-1, keepdims=True) + jnp.sum(p_ctx, axis=-1, keepdims=True)
        out = (jnp.dot(p_lat.astype(BF16), vw, preferred_element_type=F32)
               + jnp.dot(p_ctx.astype(BF16), vc, preferred_element_type=F32))
        o_ref[pl.ds(pl.multiple_of(r * w, w), w), :] = (out / den).astype(o_ref.dtype)
        return carry

    lax.fori_loop(0, rows, body, 0, unroll=4)


def _na_bias_table(rel_bias, kr):
    w = GRID_W
    n_dc = 2 * NA_COLS - 1
    dc = np.clip(np.arange(w)[None, :] - np.arange(w)[:, None] + NA_COLS - 1, 0, n_dc - 1)
    onehot = jnp.asarray((dc[:, :, None] == np.arange(n_dc)).astype(np.float32))
    toep = jnp.einsum("hrd,ckd->hrck", rel_bias.astype(F32), onehot, precision=lax.Precision.HIGHEST)
    r_lo = NA_ROWS - kr
    tab = jnp.stack([toep[:, r_lo + off:r_lo + off + kr] for off in range(kr)], axis=1)
    return jnp.transpose(tab, (0, 1, 3, 2, 4)).reshape(rel_bias.shape[0], kr, w, kr * w)


def _na_latent(p, q_gain, k_gain, bias_tab, dims):
    b, l, lc = dims["B"], dims["L"], dims["Lc"]
    rows = l // GRID_W
    kr = min(NA_ROWS, rows)
    nk = kr * GRID_W
    hd = NA_DH
    kern = functools.partial(_na_kernel, rows=rows, kr=kr)
    ctx_blk = (b * l) // lc
    return pl.pallas_call(
        kern,
        grid=(b, NA_HEADS),
        in_specs=[pl.BlockSpec((l, hd), lambda i, h: (i, OFF_NA_Q // hd + h)),
                  pl.BlockSpec((l, hd), lambda i, h: (i, OFF_NA_K // hd + h)),
                  pl.BlockSpec((l, hd), lambda i, h: (i, OFF_NA_V // hd + h)),
                  pl.BlockSpec((lc, hd), lambda i, h: (ctx_blk + i, OFF_NA_K // hd + h)),
                  pl.BlockSpec((lc, hd), lambda i, h: (ctx_blk + i, OFF_NA_V // hd + h)),
                  pl.BlockSpec((1, hd), lambda i, h: (0, 0)),
                  pl.BlockSpec((1, hd), lambda i, h: (0, 0)),
                  pl.BlockSpec((None, kr, GRID_W, nk), lambda i, h: (h, 0, 0, 0))],
        out_specs=pl.BlockSpec((l, hd), lambda i, h: (i, h)),
        out_shape=jax.ShapeDtypeStruct((b * l, NA_W), BF16),
        scratch_shapes=[pltpu.VMEM((l, hd), BF16), pltpu.VMEM((l, hd), BF16), pltpu.VMEM((l, hd), BF16)],
        compiler_params=_cparams("parallel", "parallel"),
        name="na_latent",
    )(p, p, p, p, p, q_gain.reshape(1, hd), k_gain.reshape(1, hd), bias_tab)


def _ctx_attn_kernel(q_ref, k_ref, v_ref, qg_ref, kg_ref, o_ref):
    q = _head_rms(q_ref[...], qg_ref[...]).astype(BF16)
    k = _head_rms(k_ref[...], kg_ref[...]).astype(BF16)
    s = _nt_dot(q, k) * (NA_DH ** -0.5)
    m = jnp.max(s, axis=-1, keepdims=True)
    pr = jnp.exp(s - m)
    den = jnp.sum(pr, axis=-1, keepdims=True)
    out = jnp.dot(pr.astype(BF16), v_ref[...].astype(BF16), preferred_element_type=F32)
    o_ref[...] = (out / den).astype(o_ref.dtype)


def _na_context(p, q_gain, k_gain, dims):
    b, l, lc = dims["B"], dims["L"], dims["Lc"]
    hd = NA_DH
    ctx_blk = (b * l) // lc
    return pl.pallas_call(
        _ctx_attn_kernel,
        grid=(b, NA_HEADS),
        in_specs=[pl.BlockSpec((lc, hd), lambda i, h: (ctx_blk + i, OFF_NA_Q // hd + h)),
                  pl.BlockSpec((lc, hd), lambda i, h: (ctx_blk + i, OFF_NA_K // hd + h)),
                  pl.BlockSpec((lc, hd), lambda i, h: (ctx_blk + i, OFF_NA_V // hd + h)),
                  pl.BlockSpec((1, hd), lambda i, h: (0, 0)),
                  pl.BlockSpec((1, hd), lambda i, h: (0, 0))],
        out_specs=pl.BlockSpec((lc, hd), lambda i, h: (i, h)),
        out_shape=jax.ShapeDtypeStruct((b * lc, NA_W), BF16),
        compiler_params=_cparams("parallel", "parallel"),
        name="na_context",
    )(p, p, p, q_gain.reshape(1, hd), k_gain.reshape(1, hd))


def _rope_tables(seq, pad_rows):
    nf = ML_DK // 4
    t = np.arange(seq)
    pos = np.stack([t // GRID_W, t % GRID_W], axis=-1).astype(np.float32)
    inv_freq = (ROPE_BASE ** (-np.arange(nf, dtype=np.float32) / nf)).astype(np.float32)
    lane = np.arange(ML_DK)
    axis, pair, f = lane // (2 * nf), (lane // nf) % 2, lane % nf
    ang = jnp.asarray(pos[:, axis]) * jnp.asarray(inv_freq[f])[None, :]
    cos, sin = jnp.cos(ang), jnp.sin(ang)
    s_lo = jnp.where(jnp.asarray(pair == 0)[None, :], -sin, 0.0)
    s_hi = jnp.where(jnp.asarray(pair == 1)[None, :], sin, 0.0)
    pad = lambda a, v: jnp.concatenate([a, jnp.full((pad_rows, ML_DK), v, F32)], axis=0)
    return pad(cos, 1.0), pad(s_lo, 0.0), pad(s_hi, 0.0)


def _mlprep_kernel(x_ref, xp_ref, xn_ref, w_ref, cos_ref, slo_ref, shi_ref, o_ref, *, n_lat, seq, seq_ctx):
    i, j = pl.program_id(0), pl.program_id(1)
    x = x_ref[...]
    t = x.shape[0]
    nf = ML_DK // 4
    r0 = i * t
    is_lat = r0 < n_lat
    starts = jnp.where(is_lat, r0 % seq == 0, (r0 - n_lat) % seq_ctx == 0)
    ends = jnp.where(is_lat, (r0 + t) % seq == 0, (r0 + t - n_lat) % seq_ctx == 0)
    row = lax.broadcasted_iota(I32, x.shape, 0)
    prev_row = jnp.where(starts, 0.0, xp_ref[7:8, :])
    next_row = jnp.where(ends, 0.0, xn_ref[0:1, :])
    x_prev = jnp.where(row == 0, prev_row, pltpu.roll(x, 1, 0))
    x_next = jnp.where(row == t - 1, next_row, pltpu.roll(x, t - 1, 0))
    w = w_ref[...]
    y = x_prev * w[0:1, :] + x * w[1:2, :] + x_next * w[2:3, :]
    y = jax.nn.silu(y)
    scale = jnp.where(j >= 1, ML_DK ** -0.5, 1.0)
    cos, s_lo, s_hi = cos_ref[...], slo_ref[...], shi_ref[...]
    for h in range(ML_HEADS):
        cs = slice(h * ML_DK, (h + 1) * ML_DK)
        yh = y[:, cs]
        yh = yh * cos + pltpu.roll(yh, ML_DK - nf, 1) * s_lo + pltpu.roll(yh, nf, 1) * s_hi
        o_ref[:, cs] = (yh * scale).astype(o_ref.dtype)


def _ml_prep(p, conv_w, tables, dims):
    m = p.shape[0]
    t = ROW_TILE
    n_lat, l, lc = dims["n_lat"], dims["L"], dims["Lc"]
    hd = ML_DK
    wb = ML_QK
    col0 = OFF_ML_QK // wb
    nblk8 = m // 8
    lat_blocks = l // t

    def tab_idx(i, j):
        return (jnp.where(i * t < n_lat, i % lat_blocks, lat_blocks), 0)

    kern = functools.partial(_mlprep_kernel, n_lat=n_lat, seq=l, seq_ctx=lc)
    return pl.pallas_call(
        kern,
        grid=(m // t, 2),
        in_specs=[pl.BlockSpec((t, wb), lambda i, j: (i, col0 + j)),
                  pl.BlockSpec((8, wb), lambda i, j: (jnp.maximum(i * (t // 8) - 1, 0), col0 + j)),
                  pl.BlockSpec((8, wb), lambda i, j: (jnp.minimum((i + 1) * (t // 8), nblk8 - 1), col0 + j)),
                  pl.BlockSpec((conv_w.shape[0], wb), lambda i, j: (0, j)),
                  pl.BlockSpec((t, hd), tab_idx),
                  pl.BlockSpec((t, hd), tab_idx),
                  pl.BlockSpec((t, hd), tab_idx)],
        out_specs=pl.BlockSpec((t, wb), lambda i, j: (i, j)),
        out_shape=jax.ShapeDtypeStruct((m, 2 * ML_QK), BF16),
        compiler_params=_cparams("parallel", "parallel"),
        name="ml_prep",
    )(p, p, p, conv_w, *tables)


def _mlstm_kernel(q_ref, k_ref, v0_ref, v1_ref, gc_ref, gr_ref, o_ref, c_ref, n_ref, m_ref):
    d, s = pl.program_id(1), pl.program_id(2)
    t = ML_CHUNK
    nh = ML_HEADS

    @pl.when(s == 0)
    def _():
        c_ref[...] = jnp.zeros_like(c_ref)
        n_ref[...] = jnp.zeros_like(n_ref)
        m_ref[...] = jnp.zeros_like(m_ref)

    fwd = d == 0
    ri = lax.broadcasted_iota(I32, (t, t), 0)
    ci = lax.broadcasted_iota(I32, (t, t), 1)
    tri = (ci - ri) * jnp.where(fwd, 1, -1) <= 0
    trif = tri.astype(F32)
    gc = gc_ref[...]
    gr = gr_ref[...]
    hp = lax.Precision.HIGHEST
    cum_c = jnp.dot(trif, gc, preferred_element_type=F32, precision=hp)
    cum_r = lax.dot_general(gr, trif, (((1,), (1,)), ((), ())), preferred_element_type=F32, precision=hp)
    li_c = jnp.where(fwd, gc[:, 0:nh], gc[:, 2 * nh:3 * nh])
    li_r = jnp.where(fwd, gr[0:nh, :], gr[2 * nh:3 * nh, :])
    lf_r = jnp.where(fwd, gr[nh:2 * nh, :], gr[3 * nh:4 * nh, :])
    b_c = jnp.where(fwd, cum_c[:, nh:2 * nh], cum_c[:, 3 * nh:4 * nh])
    b_r = jnp.where(fwd, cum_r[nh:2 * nh, :], cum_r[3 * nh:4 * nh, :])
    b_l = jnp.sum(lf_r, axis=1, keepdims=True)

    for h in range(nh):
        bc = b_c[:, h:h + 1]
        br = b_r[h:h + 1, :]
        lir = li_r[h:h + 1, :]
        m_h = m_ref[h:h + 1, 0:1]
        a = bc + m_h
        dlog = jnp.where(tri, bc - br + lir, NEG_INF)
        mj = jnp.maximum(a, jnp.max(dlog, axis=-1, keepdims=True))
        w_inter = jnp.exp(a - mj)
        qh = q_ref[:, h * ML_DK:(h + 1) * ML_DK]
        kh = k_ref[:, h * ML_DK:(h + 1) * ML_DK]
        hv = h % (nh // 2)
        vh = (v0_ref if h < nh // 2 else v1_ref)[:, hv * ML_DV:(hv + 1) * ML_DV].astype(BF16)
        sm = _nt_dot(qh, kh) * jnp.exp(dlog - mj)
        c_h = c_ref[h]
        n_h = n_ref[h:h + 1, :]
        num = (w_inter * jnp.dot(qh, c_h.astype(BF16), preferred_element_type=F32)
               + jnp.dot(sm.astype(BF16), vh, preferred_element_type=F32))
        den = (w_inter * jnp.sum(qh.astype(F32) * n_h, axis=-1, keepdims=True)
               + jnp.sum(sm, axis=-1, keepdims=True))
        o_ref[:, h * ML_DV:(h + 1) * ML_DV] = num / jnp.maximum(jnp.abs(den), jnp.exp(-mj))
        blh = b_l[h:h + 1, :]
        gl_r = blh - br + lir
        gl_c = blh - bc + li_c[:, h:h + 1]
        m_new = jnp.maximum(blh + m_h, jnp.max(gl_r, axis=-1, keepdims=True))
        sc = jnp.exp(blh + m_h - m_new)
        kw = kh.astype(F32) * jnp.exp(gl_c - m_new)
        c_ref[h] = sc * c_h + lax.dot_general(kw.astype(BF16), vh, (((0,), (0,)), ((), ())),
                                              preferred_element_type=F32)
        n_ref[h:h + 1, :] = sc * n_h + jnp.sum(kw, axis=0, keepdims=True)
        m_ref[h:h + 1, :] = jnp.broadcast_to(m_new, (1, LANES))


def _mlstm_scan(qk, p, gates, gates_t, dims):
    m = qk.shape[0]
    t = ML_CHUNK
    b, l, lc, n_lat = dims["B"], dims["L"], dims["Lc"], dims["n_lat"]
    ncl, ncc = l // t, lc // t

    def chunk(i, d, s):
        in_ctx = s < ncc
        c_ctx = jnp.where(d == 0, s, ncc - 1 - s)
        s_lat = s - ncc
        c_lat = jnp.where(d == 0, s_lat, ncl - 1 - s_lat)
        return jnp.where(in_ctx, n_lat // t + i * ncc + c_ctx, i * ncl + c_lat)

    return pl.pallas_call(
        _mlstm_kernel,
        grid=(b, 2, ncc + ncl),
        in_specs=[pl.BlockSpec((t, ML_QK), lambda i, d, s: (chunk(i, d, s), 0)),
                  pl.BlockSpec((t, ML_QK), lambda i, d, s: (chunk(i, d, s), 1)),
                  pl.BlockSpec((t, HALF_W), lambda i, d, s: (chunk(i, d, s), OFF_ML_V // HALF_W)),
                  pl.BlockSpec((t, HALF_W), lambda i, d, s: (chunk(i, d, s), OFF_ML_V // HALF_W + 1)),
                  pl.BlockSpec((t, LANES), lambda i, d, s: (chunk(i, d, s), 0)),
                  pl.BlockSpec((4 * ML_HEADS, t), lambda i, d, s: (0, chunk(i, d, s)))],
        out_specs=pl.BlockSpec((None, t, ML_V), lambda i, d, s: (d, chunk(i, d, s), 0)),
        out_shape=jax.ShapeDtypeStruct((2, m, ML_V), F32),
        scratch_shapes=[pltpu.VMEM((ML_HEADS, ML_DK, ML_DV), F32),
                        pltpu.VMEM((ML_HEADS, ML_DK), F32),
                        pltpu.VMEM((ML_HEADS, LANES), F32)],
        compiler_params=_cparams("parallel", "parallel", "arbitrary"),
        name="mlstm_scan",
    )(qk, qk, p, p, gates, gates_t)


def _mlout_kernel(hf_ref, hb_ref, og_ref, g_ref, y_ref):
    for h in range(HALF_W // ML_DV):
        sl = slice(h * ML_DV, (h + 1) * ML_DV)
        x = hf_ref[:, sl] + hb_ref[:, sl]
        y = x * lax.rsqrt(jnp.mean(x * x, axis=-1, keepdims=True) + EPS) * g_ref[:, sl]
        y_ref[:, sl] = (y * jax.nn.sigmoid(og_ref[:, sl])).astype(y_ref.dtype)


def _ml_out(hs, p, gain, rows):
    tm = ROW_TILE
    wb = HALF_W
    return pl.pallas_call(
        _mlout_kernel,
        grid=(rows // tm, ML_V // wb),
        in_specs=[pl.BlockSpec((None, tm, wb), lambda i, j: (0, i, j)),
                  pl.BlockSpec((None, tm, wb), lambda i, j: (1, i, j)),
                  pl.BlockSpec((tm, wb), lambda i, j: (i, OFF_ML_O // wb + j)),
                  pl.BlockSpec((1, wb), lambda i, j: (0, j))],
        out_specs=pl.BlockSpec((tm, wb), lambda i, j: (i, j)),
        out_shape=jax.ShapeDtypeStruct((rows, ML_V), BF16),
        compiler_params=_cparams("parallel", "parallel"),
        name="ml_out",
    )(hs, hs, p, gain.reshape(1, ML_V))


def _gelu_exact(x):
    return 0.5 * x * (1.0 + lax.erf(x * np.float32(np.sqrt(0.5))))


def _gmlp_kernel(u_ref, v_ref, g_ref, ws_ref, bt_ref, o_ref):
    u = _gelu_exact(u_ref[...])
    v = _gelu_exact(v_ref[...])
    vn = (v * lax.rsqrt(jnp.mean(v * v, axis=-1, keepdims=True) + EPS) * g_ref[...]).astype(BF16)
    bt = bt_ref[...]
    for c in range(u.shape[0] // GM_CHUNK):
        rs = slice(c * GM_CHUNK, (c + 1) * GM_CHUNK)
        for g in range(GM_GROUPS):
            cs = slice(g * GM_CPG, (g + 1) * GM_CPG)
            mixed = jnp.dot(ws_ref[g].astype(BF16), vn[rs, cs], preferred_element_type=F32) + bt[:, g:g + 1]
            o_ref[rs, cs] = (u[rs, cs] * mixed).astype(o_ref.dtype)


def _gmlp(p, v_gain, ws, bias, rows):
    tm = ROW_TILE
    return pl.pallas_call(
        _gmlp_kernel,
        grid=(rows // tm,),
        in_specs=[pl.BlockSpec((tm, GM_W), lambda i: (i, OFF_GM_U // GM_W)),
                  pl.BlockSpec((tm, GM_W), lambda i: (i, OFF_GM_V // GM_W)),
                  pl.BlockSpec((1, GM_W), lambda i: (0, 0)),
                  pl.BlockSpec((GM_GROUPS, GM_CHUNK, GM_CHUNK), lambda i: (0, 0, 0)),
                  pl.BlockSpec((GM_CHUNK, GM_GROUPS), lambda i: (0, 0))],
        out_specs=pl.BlockSpec((tm, GM_W), lambda i: (i, 0)),
        out_shape=jax.ShapeDtypeStruct((rows, GM_W), BF16),
        compiler_params=_cparams("parallel"),
        name="gmlp",
    )(p, p, v_gain.reshape(1, GM_W), ws, bias.T)


def _branch_kernel(ya_ref, ym_ref, yg_ref, w_ref, g0_ref, g1_ref, g2_ref, o_ref, wp_ref):
    @pl.when(pl.program_id(1) == 0)
    def _():
        wp_ref[...] = w_ref[...].astype(BF16)

    dot = functools.partial(jnp.dot, preferred_element_type=F32)
    za = dot(ya_ref[...], wp_ref[0:NA_W, :])
    zm = dot(ym_ref[...], wp_ref[NA_W:NA_W + ML_V, :])
    zg = dot(yg_ref[...], wp_ref[NA_W + ML_V:, :])
    z = (jax.nn.sigmoid(g0_ref[...]) * za + jax.nn.sigmoid(g1_ref[...]) * zm
         + jax.nn.sigmoid(g2_ref[...]) * zg)
    o_ref[...] = z.astype(o_ref.dtype)


def _branch(ya, ym, yg, w_branch, layer, p, d_model, rows):
    tm, tn = MM_TM, 512
    k = w_branch.shape[1]
    gate_blk = lambda b: (OFF_GATE + b * d_model) // tn
    return pl.pallas_call(
        _branch_kernel,
        grid=(d_model // tn, rows // tm),
        in_specs=[pl.BlockSpec((tm, NA_W), lambda j, i: (i, 0)),
                  pl.BlockSpec((tm, ML_V), lambda j, i: (i, 0)),
                  pl.BlockSpec((tm, GM_W), lambda j, i: (i, 0)),
                  pl.BlockSpec((None, k, tn), lambda j, i: (layer, 0, j)),
                  pl.BlockSpec((tm, tn), lambda j, i: (i, gate_blk(0) + j)),
                  pl.BlockSpec((tm, tn), lambda j, i: (i, gate_blk(1) + j)),
                  pl.BlockSpec((tm, tn), lambda j, i: (i, gate_blk(2) + j))],
        out_specs=pl.BlockSpec((tm, tn), lambda j, i: (i, j)),
        out_shape=jax.ShapeDtypeStruct((rows, d_model), BF16),
        scratch_shapes=[pltpu.VMEM((k, tn), BF16)],
        compiler_params=_cparams("parallel", "arbitrary"),
        name="branch_merge",
    )(ya, ym, yg, w_branch, p, p, p)


def _outproj_kernel(z_ref, w_ref, x_ref, gt_ref, o_ref, wp_ref):
    @pl.when(pl.program_id(1) == 0)
    def _():
        wp_ref[...] = w_ref[...].astype(BF16)

    y = jnp.dot(z_ref[...], wp_ref[...], preferred_element_type=F32)
    o_ref[...] = x_ref[...] + gt_ref[...] * y


def _outproj(z, w_out, layer, x, mod3, k_gate, dims, rows):
    k = z.shape[1]
    d = w_out.shape[2]
    tm, tn = MM_TM, 512
    mrow = functools.partial(_mod_row, tm=tm, n_lat=dims["n_lat"], seq=dims["L"], n_batch=dims["B"])
    gblk = k_gate * (d // tn)
    return pl.pallas_call(
        _outproj_kernel,
        grid=(d // tn, rows // tm),
        in_specs=[pl.BlockSpec((tm, k), lambda j, i: (i, 0)),
                  pl.BlockSpec((None, k, tn), lambda j, i: (layer, 0, j)),
                  pl.BlockSpec((tm, tn), lambda j, i: (i, j)),
                  pl.BlockSpec((None, 1, tn), lambda j, i: (mrow(i), 0, gblk + j))],
        out_specs=pl.BlockSpec((tm, tn), lambda j, i: (i, j)),
        out_shape=jax.ShapeDtypeStruct((rows, d), F32),
        scratch_shapes=[pltpu.VMEM((k, tn), BF16)],
        compiler_params=_cparams("parallel", "arbitrary"),
        name="out_proj",
    )(z, w_out, x, mod3)


def _affinity_kernel(h_ref, w_ref, o_ref, *, n_exp):
    logits = jnp.dot(_unpack_bf16_pairs(h_ref[...]), w_ref[...], preferred_element_type=F32)
    col = lax.broadcasted_iota(I32, logits.shape, 1)
    logits = jnp.where(col < n_exp, logits, NEG_INF)
    o_ref[...] = jax.nn.softmax(logits, axis=-1)


def _affinity(h2, w_router, rows):
    d, e = w_router.shape
    tm = MM_TM
    w_pad = jnp.pad(w_router, ((0, 0), (0, LANES - e))).astype(BF16)
    return pl.pallas_call(
        functools.partial(_affinity_kernel, n_exp=e),
        grid=(rows // tm,),
        in_specs=[pl.BlockSpec((tm, d // 2), lambda i: (i, 0)),
                  pl.BlockSpec((d, LANES), lambda i: (0, 0))],
        out_specs=pl.BlockSpec((tm, LANES), lambda i: (i, 0)),
        out_shape=jax.ShapeDtypeStruct((rows, LANES), F32),
        compiler_params=_cparams("parallel"),
        name="router_affinity",
    )(h2, w_pad)


def _cumsum_rows(x01, blk):
    n = x01.shape[0]
    ri = lax.broadcasted_iota(I32, (blk, blk), 0)
    ci = lax.broadcasted_iota(I32, (blk, blk), 1)
    tril = (ci <= ri).astype(BF16)
    parts = []
    carry = jnp.zeros((1, x01.shape[1]), F32)
    for j in range(n // blk):
        cs = jnp.dot(tril, x01[j * blk:(j + 1) * blk, :].astype(BF16), preferred_element_type=F32) + carry
        parts.append(cs)
        carry = cs[blk - 1:blk, :]
    return jnp.concatenate(parts, axis=0) if len(parts) > 1 else parts[0]


def _select_kernel(aff_ref, idx_ref, gate_ref, rank_ref, *, cap, tb, n_exp):
    aff = aff_ref[...]
    n = aff.shape[0]
    key = lax.bitcast_convert_type(aff, I32)
    capf = jnp.float32(cap)

    def search(i, thr):
        cand = thr | jnp.left_shift(jnp.int32(1), 30 - i)
        cnt = jnp.sum((key >= cand).astype(F32), axis=0, keepdims=True)
        return jnp.where(cnt >= capf, cand, thr)

    thr = lax.fori_loop(0, 31, search, jnp.zeros((1, LANES), I32))
    above = key > thr
    tied = key == thr
    need = capf - jnp.sum(above.astype(F32), axis=0, keepdims=True)
    tied_f = tied.astype(F32)
    tie_rank = _cumsum_rows(tied_f, tb) - tied_f
    sel = above | (tied & (tie_rank < need))
    incl = _cumsum_rows(sel.astype(F32), tb)
    rank_ref[0] = incl
    rank_ref[1] = jnp.where(sel, incl, -1.0)
    rank_ref[2] = aff
    slot = lax.broadcasted_iota(I32, (tb, cap), 1).astype(F32)

    for ex in range(n_exp):
        def block(j, acc):
            acc_i, acc_g = acc
            rows = pl.ds(pl.multiple_of(j * tb, tb), tb)
            inc = rank_ref[0, rows, ex:ex + 1]
            inc_sel = rank_ref[1, rows, ex:ex + 1]
            a = rank_ref[2, rows, ex:ex + 1]
            acc_i = acc_i + jnp.sum(jnp.where(inc <= slot, 1.0, 0.0), axis=0, keepdims=True)
            acc_g = acc_g + jnp.sum(jnp.where(inc_sel == slot + 1.0, a, 0.0), axis=0, keepdims=True)
            return acc_i, acc_g

        zero = jnp.zeros((1, cap), F32)
        acc_i, acc_g = lax.fori_loop(0, n // tb, block, (zero, zero))
        idx_ref[ex:ex + 1, :] = acc_i.astype(I32)
        gate_ref[ex:ex + 1, :] = acc_g


def _select(aff, e, n_groups, group_len, blk0, cap):
    tb = min(256, group_len)
    kern = functools.partial(_select_kernel, cap=cap, tb=tb, n_exp=e)
    return pl.pallas_call(
        kern,
        grid=(n_groups,),
        in_specs=[pl.BlockSpec((group_len, LANES), lambda g: (blk0 + g, 0))],
        out_specs=[pl.BlockSpec((None, e, cap), lambda g: (g, 0, 0)),
                   pl.BlockSpec((None, e, cap), lambda g: (g, 0, 0))],
        out_shape=[jax.ShapeDtypeStruct((n_groups, e, cap), I32),
                   jax.ShapeDtypeStruct((n_groups, e, cap), F32)],
        scratch_shapes=[pltpu.VMEM((3, group_len, LANES), F32)],
        compiler_params=_cparams("parallel"),
        name="router_select",
    )(aff)


def _moe_up_kernel(idx_ref, h_hbm, wg_ref, wu_ref, o_ref, xs_ref, xb_ref, sem, *, n_rows, n_exp, n_f):
    e, f = pl.program_id(0), pl.program_id(1)
    per_step = n_rows // n_f

    def row_copy(ex, i):
        slot = ex % 2
        src = idx_ref[ex * n_rows + i]
        return pltpu.make_async_copy(h_hbm.at[pl.ds(src, 1), :], xs_ref.at[slot, pl.ds(i, 1), :], sem.at[slot])

    def start_rows(ex, lo, n):
        def start(i, c):
            row_copy(ex, lo + i).start()
            return c
        lax.fori_loop(0, n, start, 0, unroll=4)

    @pl.when(jnp.logical_and(e == 0, f == 0))
    def _():
        start_rows(0, 0, n_rows)

    @pl.when(f == 0)
    def _():
        def wait(i, c):
            row_copy(e, i).wait()
            return c
        lax.fori_loop(0, n_rows, wait, 0, unroll=4)
        xb_ref[...] = _unpack_bf16_pairs(xs_ref[e % 2])

    @pl.when(e + 1 < n_exp)
    def _():
        start_rows(e + 1, f * per_step, per_step)

    xb = xb_ref[...]
    a = jnp.dot(xb, wg_ref[...].astype(BF16), preferred_element_type=F32)
    u = jnp.dot(xb, wu_ref[...].astype(BF16), preferred_element_type=F32)
    o_ref[...] = (jax.nn.silu(a) * u).astype(o_ref.dtype)


def _moe_up(idx_flat, h2_packed, w_gate, w_up, layer, n_rows):
    _, e, d, ff = w_gate.shape
    tf = 256
    n_f = ff // tf
    assert n_rows % n_f == 0
    kern = functools.partial(_moe_up_kernel, n_rows=n_rows, n_exp=e, n_f=n_f)
    return pl.pallas_call(
        kern,
        grid_spec=pltpu.PrefetchScalarGridSpec(
            num_scalar_prefetch=1,
            grid=(e, n_f),
            in_specs=[pl.BlockSpec(memory_space=pl.ANY),
                      pl.BlockSpec((None, None, d, tf), lambda i, f, idx: (layer, i, 0, f)),
                      pl.BlockSpec((None, None, d, tf), lambda i, f, idx: (layer, i, 0, f))],
            out_specs=pl.BlockSpec((None, n_rows, tf), lambda i, f, idx: (i, 0, f)),
            scratch_shapes=[pltpu.VMEM((2, n_rows, d // 2), jnp.uint32), pltpu.VMEM((n_rows, d), BF16),
                            pltpu.SemaphoreType.DMA((2,))]),
        out_shape=jax.ShapeDtypeStruct((e, n_rows, ff), BF16),
        compiler_params=_cparams("arbitrary", "arbitrary"),
        name="moe_up",
    )(idx_flat, h2_packed, w_gate, w_up)


def _moe_down_kernel(idx_ref, hid_ref, gcol_ref, wd_ref, gt_ref, x_hbm, o_hbm, acc_ref, y_ref, sem, *,
                     cap, n_exp, group_len, row0, dc):
    g, j, e = pl.program_id(0), pl.program_id(1), pl.program_id(2)
    nlt = dc // LANES
    sub = 8
    rows = pl.ds(pl.multiple_of((row0 + g * group_len) // sub, group_len // sub), group_len // sub)

    def tile_copy(c, to_vmem):
        hbm = (x_hbm if to_vmem else o_hbm).at[rows, :, pl.ds(pl.multiple_of(j * dc + c * LANES, LANES), LANES)]
        vmem = acc_ref.at[:, c]
        return pltpu.make_async_copy(hbm, vmem, sem.at[0]) if to_vmem else pltpu.make_async_copy(vmem, hbm, sem.at[1])

    @pl.when(e == 0)
    def _():
        for c in range(nlt):
            tile_copy(c, True).start()

    chunk = min(cap, 128)
    w = wd_ref[...].astype(BF16)
    gt = gt_ref[...]
    for rc in range(cap // chunk):
        rs = slice(rc * chunk, (rc + 1) * chunk)
        y = jnp.dot(hid_ref[rs, :], w, preferred_element_type=F32) * gcol_ref[rs, :] * gt
        for rr in range(chunk // sub):
            for c in range(nlt):
                y_ref[rc * chunk // sub + rr, c] = y[rr * sub:(rr + 1) * sub, c * LANES:(c + 1) * LANES]

    @pl.when(e == 0)
    def _():
        for c in range(nlt):
            tile_copy(c, True).wait()

    base = (g * n_exp + e) * cap
    acc_flat = acc_ref.reshape(group_len * nlt, LANES)
    y_flat = y_ref.reshape(cap * nlt, LANES)

    def add_rows(bi, carry):
        i0 = bi * SCATTER_BATCH
        starts = [idx_ref[base + i0 + r] for r in range(SCATTER_BATCH)]
        acc = [acc_flat[pl.ds(s, nlt, stride=sub), :] for s in starts]
        for r in range(SCATTER_BATCH):
            yr = y_flat[pl.ds(i0 * nlt + r, nlt, stride=sub), :]
            acc_flat[pl.ds(starts[r], nlt, stride=sub), :] = acc[r] + yr
        return carry

    lax.fori_loop(0, cap // SCATTER_BATCH, add_rows, 0)

    @pl.when(e == n_exp - 1)
    def _():
        for c in range(nlt):
            tile_copy(c, False).start()
        for c in range(nlt):
            tile_copy(c, False).wait()


def _moe_down(idx_flat, hid, gate_col, w_down, layer, x, mod3, k_gate, n_groups, group_len, cap,
              hid_row0, x_row0, mod_row_of_group, out_rows):
    _, e, ff, d = w_down.shape
    assert SCATTER_BATCH == 8 and cap % 8 == 0 and x_row0 % group_len == 0 and out_rows % 8 == 0
    dc = 1024
    nlt = dc // LANES
    kern = functools.partial(_moe_down_kernel, cap=cap, n_exp=e, group_len=group_len, row0=x_row0, dc=dc)
    hblk0 = hid_row0 // cap
    gblk = k_gate * (d // dc)
    out = pl.pallas_call(
        kern,
        grid_spec=pltpu.PrefetchScalarGridSpec(
            num_scalar_prefetch=1,
            grid=(n_groups, d // dc, e),
            in_specs=[pl.BlockSpec((None, cap, ff), lambda g, j, i, idx: (i, hblk0 + g, 0)),
                      pl.BlockSpec((None, cap, 1), lambda g, j, i, idx: (i, hblk0 + g, 0)),
                      pl.BlockSpec((None, None, ff, dc), lambda g, j, i, idx: (layer, i, 0, j)),
                      pl.BlockSpec((None, 1, dc), lambda g, j, i, idx: (mod_row_of_group(g), 0, gblk + j)),
                      pl.BlockSpec(memory_space=pl.ANY)],
            out_specs=pl.BlockSpec(memory_space=pl.ANY),
            scratch_shapes=[pltpu.VMEM((group_len // 8, nlt, 8, LANES), F32),
                            pltpu.VMEM((cap // 8, nlt, 8, LANES), F32),
                            pltpu.SemaphoreType.DMA((2,))]),
        out_shape=jax.ShapeDtypeStruct((out_rows // 8, 8, d), F32),
        input_output_aliases={5: 0},
        compiler_params=_cparams("arbitrary", "arbitrary", "arbitrary"),
        name="moe_down",
    )((idx_flat >> 3) * (8 * nlt) + (idx_flat & 7), hid, gate_col, w_down, mod3, x.reshape(out_rows // 8, 8, d))
    return out.reshape(out_rows, d)


def _layer(x, mod_l, layer, last, dims, w_in, na_q_gain, na_k_gain, na_rel_bias, ml_conv, ml_gate_bias,
           ml_out_gain, gm_v_gain, gm_ws, gm_bias, w_branch, w_out, g_norm1, g_norm2, w_router, w_e_gate, w_e_up,
           w_e_down, rope_tabs):
    b, l, lc, n_lat = dims["B"], dims["L"], dims["Lc"], dims["n_lat"]
    m, d = x.shape
    rows = n_lat if last else m
    mod3 = mod_l.reshape(mod_l.shape[0], 1, mod_l.shape[1])
    n_exp = w_router.shape[1]

    h = _prenorm(x, g_norm1, mod3, 0, 1, dims, m, False)
    p = _in_proj(h, w_in, layer)
    b_g = jnp.pad(ml_gate_bias.reshape(1, N_ML_GATES), ((0, 0), (0, LANES - N_ML_GATES)))
    gates = _ml_gates(h, w_in, layer, b_g)
    gates_t = gates[:, :N_ML_GATES].T

    kr = min(NA_ROWS, l // GRID_W)
    ya = _na_latent(p, na_q_gain, na_k_gain, _na_bias_table(na_rel_bias, kr), dims)
    if not last:
        ya = jnp.concatenate([ya, _na_context(p, na_q_gain, na_k_gain, dims)], axis=0)

    qk = _ml_prep(p, ml_conv, rope_tabs, dims)
    hs = _mlstm_scan(qk, p, gates, gates_t, dims)
    ym = _ml_out(hs, p, ml_out_gain, rows)
    yg = _gmlp(p, gm_v_gain, gm_ws, gm_bias, rows)

    z = _branch(ya, ym, yg, w_branch, layer, p, d, rows)
    x1 = _outproj(z, w_out, layer, x, mod3, 2, dims, rows)

    h2 = _prenorm(x1, g_norm2, mod3, 3, 4, dims, rows, True)
    aff = _affinity(h2, w_router, rows)
    cap_l = max(1, min(l, (CAPACITY_FACTOR * l) // n_exp))
    idx_l, gate_l = _select(aff, n_exp, b, l, 0, cap_l)
    row_l = idx_l + (jnp.arange(b, dtype=I32) * l)[:, None, None]
    src = [jnp.swapaxes(row_l, 0, 1).reshape(n_exp, b * cap_l)]
    gsel = [jnp.swapaxes(gate_l, 0, 1).reshape(n_exp, b * cap_l)]
    if not last:
        cap_c = max(1, min(lc, (CAPACITY_FACTOR * lc) // n_exp))
        idx_c, gate_c = _select(aff, n_exp, b, lc, n_lat // lc, cap_c)
        row_c = idx_c + (n_lat + jnp.arange(b, dtype=I32) * lc)[:, None, None]
        src.append(jnp.swapaxes(row_c, 0, 1).reshape(n_exp, b * cap_c))
        gsel.append(jnp.swapaxes(gate_c, 0, 1).reshape(n_exp, b * cap_c))
    src = jnp.concatenate(src, axis=1)
    n_rows = src.shape[1]
    gate_col = jnp.concatenate(gsel, axis=1).reshape(n_exp, n_rows, 1)
    hid = _moe_up(src.reshape(-1), h2, w_e_gate, w_e_up, layer, n_rows)
    x2 = _moe_down(idx_l.reshape(-1), hid, gate_col, w_e_down, layer, x1, mod3, 5, b, l, cap_l,
                   0, 0, lambda g: g, rows)
    if not last:
        x2 = _moe_down(idx_c.reshape(-1), hid, gate_col, w_e_down, layer, x2, mod3, 5, b, lc, cap_c,
                       b * cap_l, n_lat, lambda g: b, rows)
    return x2


def kernel(x, c, ctx, c_ctx, w_ada, b_ada, g_norm1, g_norm2, w_in, na_q_gain, na_k_gain, na_rel_bias, ml_conv,
           ml_gate_bias, ml_out_gain, gm_v_gain, gm_ws, gm_bias, w_branch, w_out, w_router, w_e_gate, w_e_up,
           w_e_down):
    b, l, d = x.shape
    lc = ctx.shape[1]
    depth = w_ada.shape[0]
    dims = {"B": b, "L": l, "Lc": lc, "n_lat": b * l}
    assert b + 1 <= 8 and l % ROW_TILE == 0 and lc % ROW_TILE == 0 and (b * l) % MM_TM == 0 and (b * lc) % MM_TM == 0

    c8 = jnp.concatenate([c, c_ctx[None, :], jnp.zeros((8 - b - 1, d), F32)], axis=0)
    mod = _ada(c8, w_ada, b_ada)
    xs = jnp.concatenate([x.reshape(b * l, d), ctx.reshape(b * lc, d)], axis=0)
    rope_tabs = _rope_tables(l, ROW_TILE)
    w_in_bf16 = w_in.astype(BF16)
    for i in range(depth):
        xs = _layer(xs, mod[i], i, i == depth - 1, dims, w_in_bf16, na_q_gain[i], na_k_gain[i], na_rel_bias[i],
                    ml_conv[i], ml_gate_bias[i], ml_out_gain[i], gm_v_gain[i], gm_ws[i], gm_bias[i], w_branch,
                    w_out, g_norm1[i], g_norm2[i], w_router[i], w_e_gate, w_e_up, w_e_down, rope_tabs)
    return xs[:b * l].reshape(b, l, d)
```

```python
import functools

import numpy as np
import jax
import jax.numpy as jnp
from jax import lax
from jax.experimental import pallas as pl
from jax.experimental.pallas import tpu as pltpu

F32 = jnp.float32
BF16 = jnp.bfloat16
I32 = jnp.int32

GRID_W = 64
NA_HEADS, NA_DH, NA_ROWS, NA_COLS = 8, 128, 8, 16
NA_W = NA_HEADS * NA_DH
ML_HEADS, ML_DK, ML_DV, ML_CHUNK = 8, 128, 256, 128
ML_QK, ML_V = ML_HEADS * ML_DK, ML_HEADS * ML_DV
GM_GROUPS, GM_CPG, GM_CHUNK = 8, 128, 128
GM_W = GM_GROUPS * GM_CPG
N_BRANCH = 3
CAPACITY_FACTOR = 2
ROPE_BASE = 10000.0
EPS = 1e-6
NEG_INF = -1e30

VMEM_LIMIT_BYTES = 56 * 1024 * 1024
LANES = 128
ROW_TILE = 256
MM_TM = 512
SCATTER_BATCH = 8

OFF_NA_Q, OFF_NA_K, OFF_NA_V = 0, NA_W, 2 * NA_W
OFF_ML_QK = 3 * NA_W
OFF_ML_V = OFF_ML_QK + 2 * ML_QK
OFF_ML_O = OFF_ML_V + ML_V
OFF_ML_GATES = OFF_ML_O + ML_V
N_ML_GATES = 4 * ML_HEADS
OFF_GM_U = OFF_ML_GATES
OFF_GM_V = OFF_GM_U + GM_W
OFF_GATE = OFF_GM_V + GM_W
HALF_W = 1024


def _cparams(*sem):
    return pltpu.CompilerParams(dimension_semantics=sem, vmem_limit_bytes=VMEM_LIMIT_BYTES)


def _nt_dot(a, b):
    return lax.dot_general(a, b, (((1,), (1,)), ((), ())), preferred_element_type=F32)


def _ada_kernel(c_ref, w_ref, b_ref, o_ref):
    a = jax.nn.silu(c_ref[...]).astype(BF16)
    o_ref[...] = jnp.dot(a, w_ref[...].astype(BF16), preferred_element_type=F32) + b_ref[...]


def _ada(c8, w_ada, b_ada):
    depth, d, n = w_ada.shape
    tn = 512
    return pl.pallas_call(
        _ada_kernel,
        grid=(depth, n // tn),
        in_specs=[pl.BlockSpec((8, d), lambda l, j: (0, 0)),
                  pl.BlockSpec((None, d, tn), lambda l, j: (l, 0, j)),
                  pl.BlockSpec((None, 1, tn), lambda l, j: (l, 0, j))],
        out_specs=pl.BlockSpec((None, 8, tn), lambda l, j: (l, 0, j)),
        out_shape=jax.ShapeDtypeStruct((depth, 8, n), F32),
        compiler_params=_cparams("parallel", "parallel"),
        name="ada_mod",
    )(c8, w_ada, b_ada.reshape(depth, 1, n))


def _mod_row(i, tm, n_lat, seq, n_batch):
    r = i * tm
    return jnp.where(r >= n_lat, n_batch, r // seq)


HI16 = np.uint32(0xFFFF0000)


def _pack_bf16_pairs(y):
    half = y.shape[1] // 2
    bits = lax.bitcast_convert_type(y.astype(BF16).astype(F32), jnp.uint32)
    return (bits[:, half:] & HI16) | (bits[:, :half] >> 16)


def _unpack_bf16_pairs(packed):
    lo = lax.bitcast_convert_type(packed << 16, F32).astype(BF16)
    hi = lax.bitcast_convert_type(packed & HI16, F32).astype(BF16)
    return jnp.concatenate([lo, hi], axis=1)


def _prenorm_kernel(x_ref, g_ref, sh_ref, sc_ref, o_ref, *, packed):
    x = x_ref[...]
    y = x * lax.rsqrt(jnp.mean(x * x, axis=-1, keepdims=True) + EPS)
    y = y * g_ref[...]
    y = y * (1 + sc_ref[...]) + sh_ref[...]
    o_ref[...] = _pack_bf16_pairs(y) if packed else y.astype(o_ref.dtype)


def _prenorm(x, g, mod3, k_shift, k_scale, dims, rows, packed):
    d = x.shape[1]
    tm = ROW_TILE
    d_out = d // 2 if packed else d
    mrow = functools.partial(_mod_row, tm=tm, n_lat=dims["n_lat"], seq=dims["L"], n_batch=dims["B"])
    return pl.pallas_call(
        functools.partial(_prenorm_kernel, packed=packed),
        grid=(rows // tm,),
        in_specs=[pl.BlockSpec((tm, d), lambda i: (i, 0)),
                  pl.BlockSpec((1, d), lambda i: (0, 0)),
                  pl.BlockSpec((None, 1, d), lambda i: (mrow(i), 0, k_shift)),
                  pl.BlockSpec((None, 1, d), lambda i: (mrow(i), 0, k_scale))],
        out_specs=pl.BlockSpec((tm, d_out), lambda i: (i, 0)),
        out_shape=jax.ShapeDtypeStruct((rows, d_out), jnp.uint32 if packed else BF16),
        compiler_params=_cparams("parallel"),
        name="prenorm",
    )(x, g.reshape(1, d), mod3, mod3)


IN_TN = 1024
K_CHUNK = 512


def _inproj_kernel(a_ref, wm_ref, wt_ref, o_ref, wp_ref, *, n_aligned, shift):
    j, i = pl.program_id(0), pl.program_id(1)
    k, tn = wm_ref.shape

    @pl.when(jnp.logical_and(i == 0, j < n_aligned))
    def _():
        for c in range(k // K_CHUNK):
            rows = slice(c * K_CHUNK, (c + 1) * K_CHUNK)
            wp_ref[rows, :] = wm_ref[rows, :].astype(BF16)

    @pl.when(jnp.logical_and(i == 0, j >= n_aligned))
    def _():
        for c in range(k // K_CHUNK):
            rows = slice(c * K_CHUNK, (c + 1) * K_CHUNK)
            w = jnp.concatenate([wm_ref[rows, :], wt_ref[rows, :]], axis=1).astype(F32)
            wp_ref[rows, :] = pltpu.roll(w, tn + LANES - shift, 1)[:, :tn].astype(BF16)

    o_ref[...] = jnp.dot(a_ref[...], wp_ref[...], preferred_element_type=F32)


def _in_proj(h, w_in, layer):
    m, k = h.shape
    n_w = w_in.shape[2]
    n_out = n_w - N_ML_GATES
    tm, tn = MM_TM, IN_TN
    assert OFF_ML_GATES % tn == 0 and n_out % tn == 0 and N_ML_GATES < LANES
    kern = functools.partial(_inproj_kernel, n_aligned=OFF_ML_GATES // tn, shift=N_ML_GATES)
    return pl.pallas_call(
        kern,
        grid=(n_out // tn, m // tm),
        in_specs=[pl.BlockSpec((tm, k), lambda j, i: (i, 0)),
                  pl.BlockSpec((None, k, tn), lambda j, i: (layer, 0, j)),
                  pl.BlockSpec((None, k, LANES), lambda j, i: (layer, 0, (tn // LANES) * (j + 1)))],
        out_specs=pl.BlockSpec((tm, tn), lambda j, i: (i, j)),
        out_shape=jax.ShapeDtypeStruct((m, n_out), F32),
        scratch_shapes=[pltpu.VMEM((k, tn), BF16)],
        compiler_params=_cparams("parallel", "arbitrary"),
        name="in_proj",
    )(h, w_in, w_in)


def _gates_kernel(a_ref, w_ref, b_ref, o_ref):
    g = jnp.dot(a_ref[...], w_ref[...].astype(BF16), preferred_element_type=F32) + b_ref[...]
    col = lax.broadcasted_iota(I32, g.shape, 1)
    is_forget = ((col // ML_HEADS) % 2) == 1
    o_ref[...] = jnp.where(is_forget, jax.nn.log_sigmoid(g), g)


def _ml_gates(h, w_in, layer, b_g):
    m, k = h.shape
    tm = MM_TM
    return pl.pallas_call(
        _gates_kernel,
        grid=(m // tm,),
        in_specs=[pl.BlockSpec((tm, k), lambda i: (i, 0)),
                  pl.BlockSpec((None, k, LANES), lambda i: (layer, 0, OFF_ML_GATES // LANES)),
                  pl.BlockSpec((1, LANES), lambda i: (0, 0))],
        out_specs=pl.BlockSpec((tm, LANES), lambda i: (i, 0)),
        out_shape=jax.ShapeDtypeStruct((m, LANES), F32),
        compiler_params=_cparams("parallel"),
        name="ml_gates",
    )(h, w_in, b_g)


def _head_rms(x, g):
    return x * lax.rsqrt(jnp.mean(x * x, axis=-1, keepdims=True) + EPS) * g


def _na_kernel(q_ref, k_ref, v_ref, kc_ref, vc_ref, qg_ref, kg_ref, bias_ref, o_ref,
               qn_ref, kn_ref, vn_ref, *, rows, kr):
    w = GRID_W
    nk = kr * w
    scale = NA_DH ** -0.5
    qn_ref[...] = _head_rms(q_ref[...], qg_ref[...]).astype(BF16)
    kn_ref[...] = _head_rms(k_ref[...], kg_ref[...]).astype(BF16)
    vn_ref[...] = v_ref[...].astype(BF16)
    kc = _head_rms(kc_ref[...], kg_ref[...]).astype(BF16)
    vc = vc_ref[...].astype(BF16)
    qcol = lax.broadcasted_iota(I32, (w, nk), 0)
    kcol = lax.broadcasted_iota(I32, (w, nk), 1) % w
    c0 = jnp.clip(qcol - NA_COLS // 2, 0, w - NA_COLS)
    in_win = (kcol >= c0) & (kcol < c0 + NA_COLS)

    def body(r, carry):
        r0 = jnp.clip(r - kr // 2, 0, rows - kr)
        off = r0 - r + (kr - 1)
        qr = qn_ref[pl.ds(pl.multiple_of(r * w, w), w), :]
        kw = kn_ref[pl.ds(pl.multiple_of(r0 * w, w), nk), :]
        vw = vn_ref[pl.ds(pl.multiple_of(r0 * w, w), nk), :]
        s_lat = _nt_dot(qr, kw) * scale + bias_ref[off]
        s_lat = jnp.where(in_win, s_lat, NEG_INF)
        s_ctx = _nt_dot(qr, kc) * scale
        m = jnp.maximum(jnp.max(s_lat, axis=-1, keepdims=True), jnp.max(s_ctx, axis=-1, keepdims=True))
        p_lat = jnp.exp(s_lat - m)
        p_ctx = jnp.exp(s_ctx - m)
        den = jnp.sum(p_lat, axis=-1, keepdims=True) + jnp.sum(p_ctx, axis=-1, keepdims=True)
        out = (jnp.dot(p_lat.astype(BF16), vw, preferred_element_type=F32)
               + jnp.dot(p_ctx.astype(BF16), vc, preferred_element_type=F32))
        o_ref[pl.ds(pl.multiple_of(r * w, w), w), :] = (out / den).astype(o_ref.dtype)
        return carry

    lax.fori_loop(0, rows, body, 0, unroll=4)


def _na_bias_table(rel_bias, kr):
    w = GRID_W
    n_dc = 2 * NA_COLS - 1
    dc = np.clip(np.arange(w)[None, :] - np.arange(w)[:, None] + NA_COLS - 1, 0, n_dc - 1)
    onehot = jnp.asarray((dc[:, :, None] == np.arange(n_dc)).astype(np.float32))
    toep = jnp.einsum("hrd,ckd->hrck", rel_bias.astype(F32), onehot, precision=lax.Precision.HIGHEST)
    r_lo = NA_ROWS - kr
    tab = jnp.stack([toep[:, r_lo + off:r_lo + off + kr] for off in range(kr)], axis=1)
    return jnp.transpose(tab, (0, 1, 3, 2, 4)).reshape(rel_bias.shape[0], kr, w, kr * w)


def _na_latent(p, q_gain, k_gain, bias_tab, dims):
    b, l, lc = dims["B"], dims["L"], dims["Lc"]
    rows = l // GRID_W
    kr = min(NA_ROWS, rows)
    nk = kr * GRID_W
    hd = NA_DH
    kern = functools.partial(_na_kernel, rows=rows, kr=kr)
    ctx_blk = (b * l) // lc
    return pl.pallas_call(
        kern,
        grid=(b, NA_HEADS),
        in_specs=[pl.BlockSpec((l, hd), lambda i, h: (i, OFF_NA_Q // hd + h)),
                  pl.BlockSpec((l, hd), lambda i, h: (i, OFF_NA_K // hd + h)),
                  pl.BlockSpec((l, hd), lambda i, h: (i, OFF_NA_V // hd + h)),
                  pl.BlockSpec((lc, hd), lambda i, h: (ctx_blk + i, OFF_NA_K // hd + h)),
                  pl.BlockSpec((lc, hd), lambda i, h: (ctx_blk + i, OFF_NA_V // hd + h)),
                  pl.BlockSpec((1, hd), lambda i, h: (0, 0)),
                  pl.BlockSpec((1, hd), lambda i, h: (0, 0)),
                  pl.BlockSpec((None, kr, GRID_W, nk), lambda i, h: (h, 0, 0, 0))],
        out_specs=pl.BlockSpec((l, hd), lambda i, h: (i, h)),
        out_shape=jax.ShapeDtypeStruct((b * l, NA_W), BF16),
        scratch_shapes=[pltpu.VMEM((l, hd), BF16), pltpu.VMEM((l, hd), BF16), pltpu.VMEM((l, hd), BF16)],
        compiler_params=_cparams("parallel", "parallel"),
        name="na_latent",
    )(p, p, p, p, p, q_gain.reshape(1, hd), k_gain.reshape(1, hd), bias_tab)


def _ctx_attn_kernel(q_ref, k_ref, v_ref, qg_ref, kg_ref, o_ref):
    q = _head_rms(q_ref[...], qg_ref[...]).astype(BF16)
    k = _head_rms(k_ref[...], kg_ref[...]).astype(BF16)
    s = _nt_dot(q, k) * (NA_DH ** -0.5)
    m = jnp.max(s, axis=-1, keepdims=True)
    pr = jnp.exp(s - m)
    den = jnp.sum(pr, axis=-1, keepdims=True)
    out = jnp.dot(pr.astype(BF16), v_ref[...].astype(BF16), preferred_element_type=F32)
    o_ref[...] = (out / den).astype(o_ref.dtype)


def _na_context(p, q_gain, k_gain, dims):
    b, l, lc = dims["B"], dims["L"], dims["Lc"]
    hd = NA_DH
    ctx_blk = (b * l) // lc
    return pl.pallas_call(
        _ctx_attn_kernel,
        grid=(b, NA_HEADS),
        in_specs=[pl.BlockSpec((lc, hd), lambda i, h: (ctx_blk + i, OFF_NA_Q // hd + h)),
                  pl.BlockSpec((lc, hd), lambda i, h: (ctx_blk + i, OFF_NA_K // hd + h)),
                  pl.BlockSpec((lc, hd), lambda i, h: (ctx_blk + i, OFF_NA_V // hd + h)),
                  pl.BlockSpec((1, hd), lambda i, h: (0, 0)),
                  pl.BlockSpec((1, hd), lambda i, h: (0, 0))],
        out_specs=pl.BlockSpec((lc, hd), lambda i, h: (i, h)),
        out_shape=jax.ShapeDtypeStruct((b * lc, NA_W), BF16),
        compiler_params=_cparams("parallel", "parallel"),
        name="na_context",
    )(p, p, p, q_gain.reshape(1, hd), k_gain.reshape(1, hd))


def _rope_tables(seq, pad_rows):
    nf = ML_DK // 4
    t = np.arange(seq)
    pos = np.stack([t // GRID_W, t % GRID_W], axis=-1).astype(np.float32)
    inv_freq = (ROPE_BASE ** (-np.arange(nf, dtype=np.float32) / nf)).astype(np.float32)
    lane = np.arange(ML_DK)
    axis, pair, f = lane // (2 * nf), (lane // nf) % 2, lane % nf
    ang = jnp.asarray(pos[:, axis]) * jnp.asarray(inv_freq[f])[None, :]
    cos, sin = jnp.cos(ang), jnp.sin(ang)
    s_lo = jnp.where(jnp.asarray(pair == 0)[None, :], -sin, 0.0)
    s_hi = jnp.where(jnp.asarray(pair == 1)[None, :], sin, 0.0)
    pad = lambda a, v: jnp.concatenate([a, jnp.full((pad_rows, ML_DK), v, F32)], axis=0)
    return pad(cos, 1.0), pad(s_lo, 0.0), pad(s_hi, 0.0)


def _mlprep_kernel(x_ref, xp_ref, xn_ref, v0_ref, v1_ref, w_ref, cos_ref, slo_ref, shi_ref,
                   q_ref, kt_ref, v_ref, *, n_lat, seq, seq_ctx):
    i, j = pl.program_id(0), pl.program_id(1)
    t = x_ref.shape[0]
    nf = ML_DK // 4
    r0 = i * t
    is_lat = r0 < n_lat
    starts = jnp.where(is_lat, r0 % seq == 0, (r0 - n_lat) % seq_ctx == 0)
    ends = jnp.where(is_lat, (r0 + t) % seq == 0, (r0 + t - n_lat) % seq_ctx == 0)
    row = lax.broadcasted_iota(I32, (t, ML_DK), 0)

    def rope(h):
        cs = slice(h * ML_DK, (h + 1) * ML_DK)
        x = x_ref[:, cs]
        prev_row = jnp.where(starts, 0.0, xp_ref[7:8, cs])
        next_row = jnp.where(ends, 0.0, xn_ref[0:1, cs])
        x_prev = jnp.where(row == 0, prev_row, pltpu.roll(x, 1, 0))
        x_next = jnp.where(row == t - 1, next_row, pltpu.roll(x, t - 1, 0))
        yh = jax.nn.silu(x_prev * w_ref[0:1, cs] + x * w_ref[1:2, cs] + x_next * w_ref[2:3, cs])
        return (yh * cos_ref[...] + pltpu.roll(yh, ML_DK - nf, 1) * slo_ref[...]
                + pltpu.roll(yh, nf, 1) * shi_ref[...])

    @pl.when(j == 0)
    def _():
        half = ML_HEADS // 2
        for h in range(ML_HEADS):
            q_ref[h] = rope(h).astype(q_ref.dtype)
            hv = h % half
            v_src = v0_ref if h < half else v1_ref
            v_ref[h] = v_src[:, hv * ML_DV:(hv + 1) * ML_DV].astype(v_ref.dtype)

    @pl.when(j == 1)
    def _():
        for h in range(ML_HEADS):
            kt_ref[h] = (rope(h) * (ML_DK ** -0.5)).T.astype(kt_ref.dtype)


def _ml_prep(p, conv_w, tables, dims):
    m = p.shape[0]
    t = ROW_TILE
    n_lat, l, lc = dims["n_lat"], dims["L"], dims["Lc"]
    hd = ML_DK
    wb = ML_QK
    col0 = OFF_ML_QK // wb
    nblk8 = m // 8
    lat_blocks = l // t

    def tab_idx(i, j):
        return (jnp.where(i * t < n_lat, i % lat_blocks, lat_blocks), 0)

    kern = functools.partial(_mlprep_kernel, n_lat=n_lat, seq=l, seq_ctx=lc)
    return pl.pallas_call(
        kern,
        grid=(m // t, 2),
        in_specs=[pl.BlockSpec((t, wb), lambda i, j: (i, col0 + j)),
                  pl.BlockSpec((8, wb), lambda i, j: (jnp.maximum(i * (t // 8) - 1, 0), col0 + j)),
                  pl.BlockSpec((8, wb), lambda i, j: (jnp.minimum((i + 1) * (t // 8), nblk8 - 1), col0 + j)),
                  pl.BlockSpec((t, HALF_W), lambda i, j: (i, OFF_ML_V // HALF_W)),
                  pl.BlockSpec((t, HALF_W), lambda i, j: (i, OFF_ML_V // HALF_W + 1)),
                  pl.BlockSpec((conv_w.shape[0], wb), lambda i, j: (0, j)),
                  pl.BlockSpec((t, hd), tab_idx),
                  pl.BlockSpec((t, hd), tab_idx),
                  pl.BlockSpec((t, hd), tab_idx)],
        out_specs=[pl.BlockSpec((ML_HEADS, t, ML_DK), lambda i, j: (0, i, 0)),
                   pl.BlockSpec((ML_HEADS, ML_DK, t), lambda i, j: (0, 0, i)),
                   pl.BlockSpec((ML_HEADS, t, ML_DV), lambda i, j: (0, i, 0))],
        out_shape=[jax.ShapeDtypeStruct((ML_HEADS, m, ML_DK), BF16),
                   jax.ShapeDtypeStruct((ML_HEADS, ML_DK, m), BF16),
                   jax.ShapeDtypeStruct((ML_HEADS, m, ML_DV), BF16)],
        compiler_params=_cparams("arbitrary", "arbitrary"),
        name="ml_prep",
    )(p, p, p, p, p, conv_w, *tables)


ML_AUG = ML_DV + LANES


def _mlstm_kernel(q_ref, kt_ref, v_ref, gr_ref, o_ref, c_ref, m_ref, cum_ref):
    d, s = pl.program_id(1), pl.program_id(2)
    t = ML_CHUNK
    nh = ML_HEADS

    @pl.when(s == 0)
    def _():
        c_ref[...] = jnp.zeros_like(c_ref)
        m_ref[...] = jnp.zeros_like(m_ref)

    fwd = d == 0
    ri = lax.broadcasted_iota(I32, (t, t), 0)
    ci = lax.broadcasted_iota(I32, (t, t), 1)
    tri = (ci - ri) * jnp.where(fwd, 1, -1) <= 0
    trif = tri.astype(F32)
    cum_ref[...] = lax.dot_general(gr_ref[...], trif, (((1,), (1,)), ((), ())), preferred_element_type=F32,
                                   precision=lax.Precision.HIGHEST)
    row_i = jnp.where(fwd, 0, 2 * nh)
    row_f = row_i + nh
    ones = jnp.ones((t, LANES), BF16)

    def head(h, carry):
        lir = gr_ref[pl.ds(row_i + h, 1), :]
        lfr = gr_ref[pl.ds(row_f + h, 1), :]
        br = cum_ref[pl.ds(row_f + h, 1), :]
        bc = jnp.sum(trif * lfr, axis=-1, keepdims=True)
        blh = jnp.sum(lfr, axis=-1, keepdims=True)
        m_h = m_ref[pl.ds(h, 1), 0:1]
        a = bc + m_h
        dlog = jnp.where(tri, bc - br + lir, NEG_INF)
        mj = jnp.maximum(a, jnp.max(dlog, axis=-1, keepdims=True))
        w_inter = jnp.exp(a - mj)
        qh = q_ref[h]
        kth = kt_ref[h]
        vh = v_ref[h]
        sm = jnp.dot(qh, kth, preferred_element_type=F32) * jnp.exp(dlog - mj)
        c_h = c_ref[h]
        qc = jnp.dot(qh, c_h.astype(BF16), preferred_element_type=F32)
        num = w_inter * qc[:, :ML_DV] + jnp.dot(sm.astype(BF16), vh, preferred_element_type=F32)
        den = w_inter * qc[:, ML_DV:ML_DV + 1] + jnp.sum(sm, axis=-1, keepdims=True)
        o_ref[h] = num / jnp.maximum(jnp.abs(den), jnp.exp(-mj))
        gl_r = blh - br + lir
        m_new = jnp.maximum(blh + m_h, jnp.max(gl_r, axis=-1, keepdims=True))
        sc = jnp.exp(blh + m_h - m_new)
        ktw = (kth.astype(F32) * jnp.exp(gl_r - m_new)).astype(BF16)
        v_aug = jnp.concatenate([vh, ones], axis=1)
        c_ref[h] = sc * c_h + jnp.dot(ktw, v_aug, preferred_element_type=F32)
        m_ref[pl.ds(h, 1), :] = jnp.broadcast_to(m_new, (1, LANES))
        return carry

    lax.fori_loop(0, nh, head, 0, unroll=8)


def _mlstm_scan(q3, kt3, v3, gates_t, dims):
    m = q3.shape[1]
    t = ML_CHUNK
    b, l, lc, n_lat = dims["B"], dims["L"], dims["Lc"], dims["n_lat"]
    ncl, ncc = l // t, lc // t
    ng = gates_t.shape[0]

    def chunk(i, d, s):
        in_ctx = s < ncc
        c_ctx = jnp.where(d == 0, s, ncc - 1 - s)
        s_lat = s - ncc
        c_lat = jnp.where(d == 0, s_lat, ncl - 1 - s_lat)
        return jnp.where(in_ctx, n_lat // t + i * ncc + c_ctx, i * ncl + c_lat)

    return pl.pallas_call(
        _mlstm_kernel,
        grid=(b, 2, ncc + ncl),
        in_specs=[pl.BlockSpec((ML_HEADS, t, ML_DK), lambda i, d, s: (0, chunk(i, d, s), 0)),
                  pl.BlockSpec((ML_HEADS, ML_DK, t), lambda i, d, s: (0, 0, chunk(i, d, s))),
                  pl.BlockSpec((ML_HEADS, t, ML_DV), lambda i, d, s: (0, chunk(i, d, s), 0)),
                  pl.BlockSpec((ng, t), lambda i, d, s: (0, chunk(i, d, s)))],
        out_specs=pl.BlockSpec((None, ML_HEADS, t, ML_DV), lambda i, d, s: (d, 0, chunk(i, d, s), 0)),
        out_shape=jax.ShapeDtypeStruct((2, ML_HEADS, m, ML_DV), F32),
        scratch_shapes=[pltpu.VMEM((ML_HEADS, ML_DK, ML_AUG), F32),
                        pltpu.VMEM((ML_HEADS, LANES), F32),
                        pltpu.VMEM((ng, t), F32)],
        compiler_params=_cparams("parallel", "parallel", "arbitrary"),
        name="mlstm_scan",
    )(q3, kt3, v3, gates_t)


def _mlout_kernel(hf_ref, hb_ref, og_ref, g_ref, y_ref):
    for h in range(HALF_W // ML_DV):
        sl = slice(h * ML_DV, (h + 1) * ML_DV)
        x = hf_ref[h] + hb_ref[h]
        y = x * lax.rsqrt(jnp.mean(x * x, axis=-1, keepdims=True) + EPS) * g_ref[:, sl]
        y_ref[:, sl] = (y * jax.nn.sigmoid(og_ref[:, sl])).astype(y_ref.dtype)


def _ml_out(hs, p, gain, rows):
    tm = ROW_TILE
    wb = HALF_W
    hb = wb // ML_DV
    return pl.pallas_call(
        _mlout_kernel,
        grid=(rows // tm, ML_V // wb),
        in_specs=[pl.BlockSpec((None, hb, tm, ML_DV), lambda i, j: (0, j, i, 0)),
                  pl.BlockSpec((None, hb, tm, ML_DV), lambda i, j: (1, j, i, 0)),
                  pl.BlockSpec((tm, wb), lambda i, j: (i, OFF_ML_O // wb + j)),
                  pl.BlockSpec((1, wb), lambda i, j: (0, j))],
        out_specs=pl.BlockSpec((tm, wb), lambda i, j: (i, j)),
        out_shape=jax.ShapeDtypeStruct((rows, ML_V), BF16),
        compiler_params=_cparams("parallel", "parallel"),
        name="ml_out",
    )(hs, hs, p, gain.reshape(1, ML_V))


def _gelu_exact(x):
    return 0.5 * x * (1.0 + lax.erf(x * np.float32(np.sqrt(0.5))))


def _gmlp_kernel(u_ref, v_ref, g_ref, ws_ref, bt_ref, o_ref):
    u = _gelu_exact(u_ref[...])
    v = _gelu_exact(v_ref[...])
    vn = (v * lax.rsqrt(jnp.mean(v * v, axis=-1, keepdims=True) + EPS) * g_ref[...]).astype(BF16)
    bt = bt_ref[...]
    for c in range(u.shape[0] // GM_CHUNK):
        rs = slice(c * GM_CHUNK, (c + 1) * GM_CHUNK)
        for g in range(GM_GROUPS):
            cs = slice(g * GM_CPG, (g + 1) * GM_CPG)
            mixed = jnp.dot(ws_ref[g].astype(BF16), vn[rs, cs], preferred_element_type=F32) + bt[:, g:g + 1]
            o_ref[rs, cs] = (u[rs, cs] * mixed).astype(o_ref.dtype)


def _gmlp(p, v_gain, ws, bias, rows):
    tm = ROW_TILE
    return pl.pallas_call(
        _gmlp_kernel,
        grid=(rows // tm,),
        in_specs=[pl.BlockSpec((tm, GM_W), lambda i: (i, OFF_GM_U // GM_W)),
                  pl.BlockSpec((tm, GM_W), lambda i: (i, OFF_GM_V // GM_W)),
                  pl.BlockSpec((1, GM_W), lambda i: (0, 0)),
                  pl.BlockSpec((GM_GROUPS, GM_CHUNK, GM_CHUNK), lambda i: (0, 0, 0)),
                  pl.BlockSpec((GM_CHUNK, GM_GROUPS), lambda i: (0, 0))],
        out_specs=pl.BlockSpec((tm, GM_W), lambda i: (i, 0)),
        out_shape=jax.ShapeDtypeStruct((rows, GM_W), BF16),
        compiler_params=_cparams("parallel"),
        name="gmlp",
    )(p, p, v_gain.reshape(1, GM_W), ws, bias.T)


def _branch_kernel(ya_ref, ym_ref, yg_ref, w_ref, g0_ref, g1_ref, g2_ref, o_ref, wp_ref):
    @pl.when(pl.program_id(1) == 0)
    def _():
        wp_ref[...] = w_ref[...].astype(BF16)

    dot = functools.partial(jnp.dot, preferred_element_type=F32)
    za = dot(ya_ref[...], wp_ref[0:NA_W, :])
    zm = dot(ym_ref[...], wp_ref[NA_W:NA_W + ML_V, :])
    zg = dot(yg_ref[...], wp_ref[NA_W + ML_V:, :])
    z = (jax.nn.sigmoid(g0_ref[...]) * za + jax.nn.sigmoid(g1_ref[...]) * zm
         + jax.nn.sigmoid(g2_ref[...]) * zg)
    o_ref[...] = z.astype(o_ref.dtype)


def _branch(ya, ym, yg, w_branch, layer, p, d_model, rows):
    tm, tn = MM_TM, 512
    k = w_branch.shape[1]
    gate_blk = lambda b: (OFF_GATE + b * d_model) // tn
    return pl.pallas_call(
        _branch_kernel,
        grid=(d_model // tn, rows // tm),
        in_specs=[pl.BlockSpec((tm, NA_W), lambda j, i: (i, 0)),
                  pl.BlockSpec((tm, ML_V), lambda j, i: (i, 0)),
                  pl.BlockSpec((tm, GM_W), lambda j, i: (i, 0)),
                  pl.BlockSpec((None, k, tn), lambda j, i: (layer, 0, j)),
                  pl.BlockSpec((tm, tn), lambda j, i: (i, gate_blk(0) + j)),
                  pl.BlockSpec((tm, tn), lambda j, i: (i, gate_blk(1) + j)),
                  pl.BlockSpec((tm, tn), lambda j, i: (i, gate_blk(2) + j))],
        out_specs=pl.BlockSpec((tm, tn), lambda j, i: (i, j)),
        out_shape=jax.ShapeDtypeStruct((rows, d_model), BF16),
        scratch_shapes=[pltpu.VMEM((k, tn), BF16)],
        compiler_params=_cparams("parallel", "arbitrary"),
        name="branch_merge",
    )(ya, ym, yg, w_branch, p, p, p)


def _outproj_kernel(z_ref, w_ref, x_ref, gt_ref, o_ref, wp_ref):
    @pl.when(pl.program_id(1) == 0)
    def _():
        wp_ref[...] = w_ref[...].astype(BF16)

    y = jnp.dot(z_ref[...], wp_ref[...], preferred_element_type=F32)
    o_ref[...] = x_ref[...] + gt_ref[...] * y


def _outproj(z, w_out, layer, x, mod3, k_gate, dims, rows):
    k = z.shape[1]
    d = w_out.shape[2]
    tm, tn = MM_TM, 512
    mrow = functools.partial(_mod_row, tm=tm, n_lat=dims["n_lat"], seq=dims["L"], n_batch=dims["B"])
    gblk = k_gate * (d // tn)
    return pl.pallas_call(
        _outproj_kernel,
        grid=(d // tn, rows // tm),
        in_specs=[pl.BlockSpec((tm, k), lambda j, i: (i, 0)),
                  pl.BlockSpec((None, k, tn), lambda j, i: (layer, 0, j)),
                  pl.BlockSpec((tm, tn), lambda j, i: (i, j)),
                  pl.BlockSpec((None, 1, tn), lambda j, i: (mrow(i), 0, gblk + j))],
        out_specs=pl.BlockSpec((tm, tn), lambda j, i: (i, j)),
        out_shape=jax.ShapeDtypeStruct((rows, d), F32),
        scratch_shapes=[pltpu.VMEM((k, tn), BF16)],
        compiler_params=_cparams("parallel", "arbitrary"),
        name="out_proj",
    )(z, w_out, x, mod3)


def _affinity_kernel(h_ref, w_ref, o_ref, *, n_exp):
    logits = jnp.dot(_unpack_bf16_pairs(h_ref[...]), w_ref[...], preferred_element_type=F32)
    col = lax.broadcasted_iota(I32, logits.shape, 1)
    logits = jnp.where(col < n_exp, logits, NEG_INF)
    o_ref[...] = jax.nn.softmax(logits, axis=-1)


def _affinity(h2, w_router, rows):
    d, e = w_router.shape
    tm = MM_TM
    w_pad = jnp.pad(w_router, ((0, 0), (0, LANES - e))).astype(BF16)
    return pl.pallas_call(
        functools.partial(_affinity_kernel, n_exp=e),
        grid=(rows // tm,),
        in_specs=[pl.BlockSpec((tm, d // 2), lambda i: (i, 0)),
                  pl.BlockSpec((d, LANES), lambda i: (0, 0))],
        out_specs=pl.BlockSpec((tm, LANES), lambda i: (i, 0)),
        out_shape=jax.ShapeDtypeStruct((rows, LANES), F32),
        compiler_params=_cparams("parallel"),
        name="router_affinity",
    )(h2, w_pad)


def _cumsum_rows(x01, blk):
    n = x01.shape[0]
    ri = lax.broadcasted_iota(I32, (blk, blk), 0)
    ci = lax.broadcasted_iota(I32, (blk, blk), 1)
    tril = (ci <= ri).astype(BF16)
    parts = []
    carry = jnp.zeros((1, x01.shape[1]), F32)
    for j in range(n // blk):
        cs = jnp.dot(tril, x01[j * blk:(j + 1) * blk, :].astype(BF16), preferred_element_type=F32) + carry
        parts.append(cs)
        carry = cs[blk - 1:blk, :]
    return jnp.concatenate(parts, axis=0) if len(parts) > 1 else parts[0]


def _select_kernel(aff_ref, idx_ref, gate_ref, rank_ref, *, cap, tb, n_exp):
    aff = aff_ref[...]
    n = aff.shape[0]
    key = lax.bitcast_convert_type(aff, I32)
    capf = jnp.float32(cap)

    def search(i, thr):
        cand = thr | jnp.left_shift(jnp.int32(1), 30 - i)
        cnt = jnp.sum((key >= cand).astype(F32), axis=0, keepdims=True)
        return jnp.where(cnt >= capf, cand, thr)

    thr = lax.fori_loop(0, 31, search, jnp.zeros((1, LANES), I32))
    above = key > thr
    tied = key == thr
    need = capf - jnp.sum(above.astype(F32), axis=0, keepdims=True)
    tied_f = tied.astype(F32)
    tie_rank = _cumsum_rows(tied_f, tb) - tied_f
    sel = above | (tied & (tie_rank < need))
    incl = _cumsum_rows(sel.astype(F32), tb)
    rank_ref[0] = incl
    rank_ref[1] = jnp.where(sel, incl, -1.0)
    rank_ref[2] = aff
    slot = lax.broadcasted_iota(I32, (tb, cap), 1).astype(F32)

    for ex in range(n_exp):
        def block(j, acc):
            acc_i, acc_g = acc
            rows = pl.ds(pl.multiple_of(j * tb, tb), tb)
            inc = rank_ref[0, rows, ex:ex + 1]
            inc_sel = rank_ref[1, rows, ex:ex + 1]
            a = rank_ref[2, rows, ex:ex + 1]
            acc_i = acc_i + jnp.sum(jnp.where(inc <= slot, 1.0, 0.0), axis=0, keepdims=True)
            acc_g = acc_g + jnp.sum(jnp.where(inc_sel == slot + 1.0, a, 0.0), axis=0, keepdims=True)
            return acc_i, acc_g

        zero = jnp.zeros((1, cap), F32)
        acc_i, acc_g = lax.fori_loop(0, n // tb, block, (zero, zero))
        idx_ref[ex:ex + 1, :] = acc_i.astype(I32)
        gate_ref[ex:ex + 1, :] = acc_g


def _select(aff, e, n_groups, group_len, blk0, cap):
    tb = min(256, group_len)
    kern = functools.partial(_select_kernel, cap=cap, tb=tb, n_exp=e)
    return pl.pallas_call(
        kern,
        grid=(n_groups,),
        in_specs=[pl.BlockSpec((group_len, LANES), lambda g: (blk0 + g, 0))],
        out_specs=[pl.BlockSpec((None, e, cap), lambda g: (g, 0, 0)),
                   pl.BlockSpec((None, e, cap), lambda g: (g, 0, 0))],
        out_shape=[jax.ShapeDtypeStruct((n_groups, e, cap), I32),
                   jax.ShapeDtypeStruct((n_groups, e, cap), F32)],
        scratch_shapes=[pltpu.VMEM((3, group_len, LANES), F32)],
        compiler_params=_cparams("parallel"),
        name="router_select",
    )(aff)


def _moe_up_kernel(idx_ref, h_hbm, wg_ref, wu_ref, o_ref, xs_ref, xb_ref, sem, *, n_rows, n_exp, n_f):
    e, f = pl.program_id(0), pl.program_id(1)
    per_step = n_rows // n_f

    def row_copy(ex, i):
        slot = ex % 2
        src = idx_ref[ex * n_rows + i]
        return pltpu.make_async_copy(h_hbm.at[pl.ds(src, 1), :], xs_ref.at[slot, pl.ds(i, 1), :], sem.at[slot])

    def start_rows(ex, lo, n):
        def start(i, c):
            row_copy(ex, lo + i).start()
            return c
        lax.fori_loop(0, n, start, 0, unroll=4)

    @pl.when(jnp.logical_and(e == 0, f == 0))
    def _():
        start_rows(0, 0, n_rows)

    @pl.when(f == 0)
    def _():
        def wait(i, c):
            row_copy(e, i).wait()
            return c
        lax.fori_loop(0, n_rows, wait, 0, unroll=4)
        xb_ref[...] = _unpack_bf16_pairs(xs_ref[e % 2])

    @pl.when(e + 1 < n_exp)
    def _():
        start_rows(e + 1, f * per_step, per_step)

    xb = xb_ref[...]
    a = jnp.dot(xb, wg_ref[...].astype(BF16), preferred_element_type=F32)
    u = jnp.dot(xb, wu_ref[...].astype(BF16), preferred_element_type=F32)
    o_ref[...] = (jax.nn.silu(a) * u).astype(o_ref.dtype)


def _moe_up(idx_flat, h2_packed, w_gate, w_up, layer, n_rows):
    _, e, d, ff = w_gate.shape
    tf = 256
    n_f = ff // tf
    assert n_rows % n_f == 0
    kern = functools.partial(_moe_up_kernel, n_rows=n_rows, n_exp=e, n_f=n_f)
    return pl.pallas_call(
        kern,
        grid_spec=pltpu.PrefetchScalarGridSpec(
            num_scalar_prefetch=1,
            grid=(e, n_f),
            in_specs=[pl.BlockSpec(memory_space=pl.ANY),
                      pl.BlockSpec((None, None, d, tf), lambda i, f, idx: (layer, i, 0, f)),
                      pl.BlockSpec((None, None, d, tf), lambda i, f, idx: (layer, i, 0, f))],
            out_specs=pl.BlockSpec((None, n_rows, tf), lambda i, f, idx: (i, 0, f)),
            scratch_shapes=[pltpu.VMEM((2, n_rows, d // 2), jnp.uint32), pltpu.VMEM((n_rows, d), BF16),
                            pltpu.SemaphoreType.DMA((2,))]),
        out_shape=jax.ShapeDtypeStruct((e, n_rows, ff), BF16),
        compiler_params=_cparams("arbitrary", "arbitrary"),
        name="moe_up",
    )(idx_flat, h2_packed, w_gate, w_up)


def _moe_down_kernel(idx_ref, hid_ref, gcol_ref, wd_ref, gt_ref, x_hbm, o_hbm, acc_ref, y_ref, sem, *,
                     cap, n_exp, group_len, row0, dc):
    g, j, e = pl.program_id(0), pl.program_id(1), pl.program_id(2)
    nlt = dc // LANES
    sub = 8
    rows = pl.ds(pl.multiple_of((row0 + g * group_len) // sub, group_len // sub), group_len // sub)

    def tile_copy(c, to_vmem):
        hbm = (x_hbm if to_vmem else o_hbm).at[rows, :, pl.ds(pl.multiple_of(j * dc + c * LANES, LANES), LANES)]
        vmem = acc_ref.at[:, c]
        return pltpu.make_async_copy(hbm, vmem, sem.at[0]) if to_vmem else pltpu.make_async_copy(vmem, hbm, sem.at[1])

    @pl.when(e == 0)
    def _():
        for c in range(nlt):
            tile_copy(c, True).start()

    chunk = min(cap, 128)
    w = wd_ref[...].astype(BF16)
    gt = gt_ref[...]
    for rc in range(cap // chunk):
        rs = slice(rc * chunk, (rc + 1) * chunk)
        y = jnp.dot(hid_ref[rs, :], w, preferred_element_type=F32) * gcol_ref[rs, :] * gt
        for rr in range(chunk // sub):
            for c in range(nlt):
                y_ref[rc * chunk // sub + rr, c] = y[rr * sub:(rr + 1) * sub, c * LANES:(c + 1) * LANES]

    @pl.when(e == 0)
    def _():
        for c in range(nlt):
            tile_copy(c, True).wait()

    base = (g * n_exp + e) * cap
    acc_flat = acc_ref.reshape(group_len * nlt, LANES)
    y_flat = y_ref.reshape(cap * nlt, LANES)

    def add_rows(bi, carry):
        i0 = bi * SCATTER_BATCH
        starts = [idx_ref[base + i0 + r] for r in range(SCATTER_BATCH)]
        acc = [acc_flat[pl.ds(s, nlt, stride=sub), :] for s in starts]
        for r in range(SCATTER_BATCH):
            yr = y_flat[pl.ds(i0 * nlt + r, nlt, stride=sub), :]
            acc_flat[pl.ds(starts[r], nlt, stride=sub), :] = acc[r] + yr
        return carry

    lax.fori_loop(0, cap // SCATTER_BATCH, add_rows, 0)

    @pl.when(e == n_exp - 1)
    def _():
        for c in range(nlt):
            tile_copy(c, False).start()
        for c in range(nlt):
            tile_copy(c, False).wait()


def _moe_down(idx_flat, hid, gate_col, w_down, layer, x, mod3, k_gate, n_groups, group_len, cap,
              hid_row0, x_row0, mod_row_of_group, out_rows):
    _, e, ff, d = w_down.shape
    assert SCATTER_BATCH == 8 and cap % 8 == 0 and x_row0 % group_len == 0 and out_rows % 8 == 0
    dc = 1024
    nlt = dc // LANES
    kern = functools.partial(_moe_down_kernel, cap=cap, n_exp=e, group_len=group_len, row0=x_row0, dc=dc)
    hblk0 = hid_row0 // cap
    gblk = k_gate * (d // dc)
    out = pl.pallas_call(
        kern,
        grid_spec=pltpu.PrefetchScalarGridSpec(
            num_scalar_prefetch=1,
            grid=(n_groups, d // dc, e),
            in_specs=[pl.BlockSpec((None, cap, ff), lambda g, j, i, idx: (i, hblk0 + g, 0)),
                      pl.BlockSpec((None, cap, 1), lambda g, j, i, idx: (i, hblk0 + g, 0)),
                      pl.BlockSpec((None, None, ff, dc), lambda g, j, i, idx: (layer, i, 0, j)),
                      pl.BlockSpec((None, 1, dc), lambda g, j, i, idx: (mod_row_of_group(g), 0, gblk + j)),
                      pl.BlockSpec(memory_space=pl.ANY)],
            out_specs=pl.BlockSpec(memory_space=pl.ANY),
            scratch_shapes=[pltpu.VMEM((group_len // 8, nlt, 8, LANES), F32),
                            pltpu.VMEM((cap // 8, nlt, 8, LANES), F32),
                            pltpu.SemaphoreType.DMA((2,))]),
        out_shape=jax.ShapeDtypeStruct((out_rows // 8, 8, d), F32),
        input_output_aliases={5: 0},
        compiler_params=_cparams("arbitrary", "arbitrary", "arbitrary"),
        name="moe_down",
    )((idx_flat >> 3) * (8 * nlt) + (idx_flat & 7), hid, gate_col, w_down, mod3, x.reshape(out_rows // 8, 8, d))
    return out.reshape(out_rows, d)


def _layer(x, mod_l, layer, last, dims, w_in, na_q_gain, na_k_gain, na_rel_bias, ml_conv, ml_gate_bias,
           ml_out_gain, gm_v_gain, gm_ws, gm_bias, w_branch, w_out, g_norm1, g_norm2, w_router, w_e_gate, w_e_up,
           w_e_down, rope_tabs):
    b, l, lc, n_lat = dims["B"], dims["L"], dims["Lc"], dims["n_lat"]
    m, d = x.shape
    rows = n_lat if last else m
    mod3 = mod_l.reshape(mod_l.shape[0], 1, mod_l.shape[1])
    n_exp = w_router.shape[1]

    h = _prenorm(x, g_norm1, mod3, 0, 1, dims, m, False)
    p = _in_proj(h, w_in, layer)
    b_g = jnp.pad(ml_gate_bias.reshape(1, N_ML_GATES), ((0, 0), (0, LANES - N_ML_GATES)))
    gates = _ml_gates(h, w_in, layer, b_g)
    gates_t = gates[:, :N_ML_GATES].T

    kr = min(NA_ROWS, l // GRID_W)
    ya = _na_latent(p, na_q_gain, na_k_gain, _na_bias_table(na_rel_bias, kr), dims)
    if not last:
        ya = jnp.concatenate([ya, _na_context(p, na_q_gain, na_k_gain, dims)], axis=0)

    q3, kt3, v3 = _ml_prep(p, ml_conv, rope_tabs, dims)
    hs = _mlstm_scan(q3, kt3, v3, gates_t, dims)
    ym = _ml_out(hs, p, ml_out_gain, rows)
    yg = _gmlp(p, gm_v_gain, gm_ws, gm_bias, rows)

    z = _branch(ya, ym, yg, w_branch, layer, p, d, rows)
    x1 = _outproj(z, w_out, layer, x, mod3, 2, dims, rows)

    h2 = _prenorm(x1, g_norm2, mod3, 3, 4, dims, rows, True)
    aff = _affinity(h2, w_router, rows)
    cap_l = max(1, min(l, (CAPACITY_FACTOR * l) // n_exp))
    idx_l, gate_l = _select(aff, n_exp, b, l, 0, cap_l)
    row_l = idx_l + (jnp.arange(b, dtype=I32) * l)[:, None, None]
    src = [jnp.swapaxes(row_l, 0, 1).reshape(n_exp, b * cap_l)]
    gsel = [jnp.swapaxes(gate_l, 0, 1).reshape(n_exp, b * cap_l)]
    if not last:
        cap_c = max(1, min(lc, (CAPACITY_FACTOR * lc) // n_exp))
        idx_c, gate_c = _select(aff, n_exp, b, lc, n_lat // lc, cap_c)
        row_c = idx_c + (n_lat + jnp.arange(b, dtype=I32) * lc)[:, None, None]
        src.append(jnp.swapaxes(row_c, 0, 1).reshape(n_exp, b * cap_c))
        gsel.append(jnp.swapaxes(gate_c, 0, 1).reshape(n_exp, b * cap_c))
    src = jnp.concatenate(src, axis=1)
    n_rows = src.shape[1]
    gate_col = jnp.concatenate(gsel, axis=1).reshape(n_exp, n_rows, 1)
    hid = _moe_up(src.reshape(-1), h2, w_e_gate, w_e_up, layer, n_rows)
    x2 = _moe_down(idx_l.reshape(-1), hid, gate_col, w_e_down, layer, x1, mod3, 5, b, l, cap_l,
                   0, 0, lambda g: g, rows)
    if not last:
        idx_cc = jnp.swapaxes(idx_c + (jnp.arange(b, dtype=I32) * lc)[:, None, None], 0, 1)
        x2 = _moe_down(idx_cc.reshape(-1), hid, gate_col, w_e_down, layer, x2, mod3, 5, 1, b * lc, b * cap_c,
                       b * cap_l, n_lat, lambda g: b, rows)
    return x2


def kernel(x, c, ctx, c_ctx, w_ada, b_ada, g_norm1, g_norm2, w_in, na_q_gain, na_k_gain, na_rel_bias, ml_conv,
           ml_gate_bias, ml_out_gain, gm_v_gain, gm_ws, gm_bias, w_branch, w_out, w_router, w_e_gate, w_e_up,
           w_e_down):
    b, l, d = x.shape
    lc = ctx.shape[1]
    depth = w_ada.shape[0]
    dims = {"B": b, "L": l, "Lc": lc, "n_lat": b * l}
    assert b + 1 <= 8 and l % ROW_TILE == 0 and lc % ROW_TILE == 0 and (b * l) % MM_TM == 0 and (b * lc) % MM_TM == 0

    c8 = jnp.concatenate([c, c_ctx[None, :], jnp.zeros((8 - b - 1, d), F32)], axis=0)
    mod = _ada(c8, w_ada, b_ada)
    xs = jnp.concatenate([x.reshape(b * l, d), ctx.reshape(b * lc, d)], axis=0)
    rope_tabs = _rope_tables(l, ROW_TILE)
    w_in_bf16 = w_in.astype(BF16)
    for i in range(depth):
        xs = _layer(xs, mod[i], i, i == depth - 1, dims, w_in_bf16, na_q_gain[i], na_k_gain[i], na_rel_bias[i],
                    ml_conv[i], ml_gate_bias[i], ml_out_gain[i], gm_v_gain[i], gm_ws[i], gm_bias[i], w_branch,
                    w_out, g_norm1[i], g_norm2[i], w_router[i], w_e_gate, w_e_up, w_e_down, rope_tabs)
    return xs[:b * l].reshape(b, l, d)
```

```python
import functools

import numpy as np
import jax
import jax.numpy as jnp
from jax import lax
from jax.experimental import pallas as pl
from jax.experimental.pallas import tpu as pltpu

F32 = jnp.float32
BF16 = jnp.bfloat16
I32 = jnp.int32

GRID_W = 64
NA_HEADS, NA_DH, NA_ROWS, NA_COLS = 8, 128, 8, 16
NA_W = NA_HEADS * NA_DH
ML_HEADS, ML_DK, ML_DV, ML_CHUNK = 8, 128, 256, 128
ML_QK, ML_V = ML_HEADS * ML_DK, ML_HEADS * ML_DV
GM_GROUPS, GM_CPG, GM_CHUNK = 8, 128, 128
GM_W = GM_GROUPS * GM_CPG
N_BRANCH = 3
CAPACITY_FACTOR = 2
ROPE_BASE = 10000.0
EPS = 1e-6
NEG_INF = -1e30

VMEM_LIMIT_BYTES = 56 * 1024 * 1024
LANES = 128
ROW_TILE = 256
MM_TM = 512
SCATTER_BATCH = 8

OFF_NA_Q, OFF_NA_K, OFF_NA_V = 0, NA_W, 2 * NA_W
OFF_ML_QK = 3 * NA_W
OFF_ML_V = OFF_ML_QK + 2 * ML_QK
OFF_ML_O = OFF_ML_V + ML_V
OFF_ML_GATES = OFF_ML_O + ML_V
N_ML_GATES = 4 * ML_HEADS
OFF_GM_U = OFF_ML_GATES
OFF_GM_V = OFF_GM_U + GM_W
OFF_GATE = OFF_GM_V + GM_W
HALF_W = 1024


def _cparams(*sem):
    return pltpu.CompilerParams(dimension_semantics=sem, vmem_limit_bytes=VMEM_LIMIT_BYTES)


def _nt_dot(a, b):
    return lax.dot_general(a, b, (((1,), (1,)), ((), ())), preferred_element_type=F32)


def _ada_kernel(c_ref, w_ref, b_ref, o_ref):
    a = jax.nn.silu(c_ref[...]).astype(BF16)
    o_ref[...] = jnp.dot(a, w_ref[...].astype(BF16), preferred_element_type=F32) + b_ref[...]


def _ada(c8, w_ada, b_ada):
    depth, d, n = w_ada.shape
    tn = 512
    return pl.pallas_call(
        _ada_kernel,
        grid=(depth, n // tn),
        in_specs=[pl.BlockSpec((8, d), lambda l, j: (0, 0)),
                  pl.BlockSpec((None, d, tn), lambda l, j: (l, 0, j)),
                  pl.BlockSpec((None, 1, tn), lambda l, j: (l, 0, j))],
        out_specs=pl.BlockSpec((None, 8, tn), lambda l, j: (l, 0, j)),
        out_shape=jax.ShapeDtypeStruct((depth, 8, n), F32),
        compiler_params=_cparams("parallel", "parallel"),
        name="ada_mod",
    )(c8, w_ada, b_ada.reshape(depth, 1, n))


def _mod_row(i, tm, n_lat, seq, n_batch):
    r = i * tm
    return jnp.where(r >= n_lat, n_batch, r // seq)


HI16 = np.uint32(0xFFFF0000)


def _pack_bf16_pairs(y):
    half = y.shape[1] // 2
    bits = lax.bitcast_convert_type(y.astype(BF16).astype(F32), jnp.uint32)
    return (bits[:, half:] & HI16) | (bits[:, :half] >> 16)


def _unpack_bf16_pairs(packed):
    lo = lax.bitcast_convert_type(packed << 16, F32).astype(BF16)
    hi = lax.bitcast_convert_type(packed & HI16, F32).astype(BF16)
    return jnp.concatenate([lo, hi], axis=1)


def _prenorm_kernel(x_ref, g_ref, sh_ref, sc_ref, o_ref, *, packed):
    x = x_ref[...]
    y = x * lax.rsqrt(jnp.mean(x * x, axis=-1, keepdims=True) + EPS)
    y = y * g_ref[...]
    y = y * (1 + sc_ref[...]) + sh_ref[...]
    o_ref[...] = _pack_bf16_pairs(y) if packed else y.astype(o_ref.dtype)


def _prenorm(x, g, mod3, k_shift, k_scale, dims, rows, packed):
    d = x.shape[1]
    tm = ROW_TILE
    d_out = d // 2 if packed else d
    mrow = functools.partial(_mod_row, tm=tm, n_lat=dims["n_lat"], seq=dims["L"], n_batch=dims["B"])
    return pl.pallas_call(
        functools.partial(_prenorm_kernel, packed=packed),
        grid=(rows // tm,),
        in_specs=[pl.BlockSpec((tm, d), lambda i: (i, 0)),
                  pl.BlockSpec((1, d), lambda i: (0, 0)),
                  pl.BlockSpec((None, 1, d), lambda i: (mrow(i), 0, k_shift)),
                  pl.BlockSpec((None, 1, d), lambda i: (mrow(i), 0, k_scale))],
        out_specs=pl.BlockSpec((tm, d_out), lambda i: (i, 0)),
        out_shape=jax.ShapeDtypeStruct((rows, d_out), jnp.uint32 if packed else BF16),
        compiler_params=_cparams("parallel"),
        name="prenorm",
    )(x, g.reshape(1, d), mod3, mod3)


IN_TN = 1024
IN_MAX_CHUNKS = 16


def _inproj_kernel(a_ref, wt_hbm, o_ref, wp_ref, stage_ref, sem, *, layer, n_panels, n_aligned, shift, n_chunks):
    p, i = pl.program_id(0), pl.program_id(1)
    tn = o_ref.shape[1]
    rows_c = tn // n_chunks
    building = jnp.logical_and(p < n_panels, i < n_chunks)

    def chunk_copy(c):
        r0 = p * tn + jnp.where(p >= n_aligned, shift, 0) + c * rows_c
        src = wt_hbm.at[layer, pl.ds(pl.multiple_of(r0, 8), rows_c), :]
        return pltpu.make_async_copy(src, stage_ref.at[c % 2], sem.at[c % 2])

    @pl.when(jnp.logical_and(building, i == 0))
    def _():
        chunk_copy(0).start()

    @pl.when(building)
    def _():
        chunk_copy(i).wait()

        @pl.when(i + 1 < n_chunks)
        def _():
            chunk_copy(i + 1).start()

        wp_ref[p % 2, pl.ds(pl.multiple_of(i * rows_c, rows_c), rows_c), :] = stage_ref[i % 2].astype(BF16)

    @pl.when(p > 0)
    def _():
        o_ref[...] = _nt_dot(a_ref[...], wp_ref[(p + 1) % 2])


def _in_proj(h, w_in_t, layer):
    m, k = h.shape
    n_out = w_in_t.shape[1] - N_ML_GATES
    tm, tn = MM_TM, IN_TN
    n_panels = n_out // tn
    n_chunks = IN_MAX_CHUNKS
    while n_chunks > m // tm:
        n_chunks //= 2
    assert OFF_ML_GATES % tn == 0 and n_out % tn == 0 and N_ML_GATES % 8 == 0 and n_chunks >= 1
    kern = functools.partial(_inproj_kernel, layer=layer, n_panels=n_panels, n_aligned=OFF_ML_GATES // tn,
                             shift=N_ML_GATES, n_chunks=n_chunks)
    return pl.pallas_call(
        kern,
        grid=(n_panels + 1, m // tm),
        in_specs=[pl.BlockSpec((tm, k), lambda p, i: (i, 0)),
                  pl.BlockSpec(memory_space=pl.ANY)],
        out_specs=pl.BlockSpec((tm, tn), lambda p, i: (jnp.where(p == 0, 0, i), jnp.maximum(p - 1, 0))),
        out_shape=jax.ShapeDtypeStruct((m, n_out), F32),
        scratch_shapes=[pltpu.VMEM((2, tn, k), BF16), pltpu.VMEM((2, tn // n_chunks, k), F32),
                        pltpu.SemaphoreType.DMA((2,))],
        compiler_params=_cparams("arbitrary", "arbitrary"),
        name="in_proj",
    )(h, w_in_t)


def _gates_kernel(a_ref, w_ref, b_ref, o_ref):
    g = _nt_dot(a_ref[...], w_ref[...].astype(BF16)) + b_ref[...]
    col = lax.broadcasted_iota(I32, g.shape, 1)
    is_forget = ((col // ML_HEADS) % 2) == 1
    o_ref[...] = jnp.where(is_forget, jax.nn.log_sigmoid(g), g)


def _ml_gates(h, w_in_t, layer, b_g):
    m, k = h.shape
    tm = MM_TM
    return pl.pallas_call(
        _gates_kernel,
        grid=(m // tm,),
        in_specs=[pl.BlockSpec((tm, k), lambda i: (i, 0)),
                  pl.BlockSpec((None, LANES, k), lambda i: (layer, OFF_ML_GATES // LANES, 0)),
                  pl.BlockSpec((1, LANES), lambda i: (0, 0))],
        out_specs=pl.BlockSpec((tm, LANES), lambda i: (i, 0)),
        out_shape=jax.ShapeDtypeStruct((m, LANES), F32),
        compiler_params=_cparams("parallel"),
        name="ml_gates",
    )(h, w_in_t, b_g)


def _head_rms(x, g):
    return x * lax.rsqrt(jnp.mean(x * x, axis=-1, keepdims=True) + EPS) * g


def _na_kernel(q_ref, k_ref, v_ref, kc_ref, vc_ref, qg_ref, kg_ref, bias_ref, o_ref,
               qn_ref, kn_ref, vn_ref, *, rows, kr):
    w = GRID_W
    nk = kr * w
    scale = NA_DH ** -0.5
    qn_ref[...] = _head_rms(q_ref[...], qg_ref[...]).astype(BF16)
    kn_ref[...] = _head_rms(k_ref[...], kg_ref[...]).astype(BF16)
    vn_ref[...] = v_ref[...].astype(BF16)
    kc = _head_rms(kc_ref[...], kg_ref[...]).astype(BF16)
    vc = vc_ref[...].astype(BF16)
    qcol = lax.broadcasted_iota(I32, (w, nk), 0)
    kcol = lax.broadcasted_iota(I32, (w, nk), 1) % w
    c0 = jnp.clip(qcol - NA_COLS // 2, 0, w - NA_COLS)
    in_win = (kcol >= c0) & (kcol < c0 + NA_COLS)

    def body(r, carry):
        r0 = jnp.clip(r - kr // 2, 0, rows - kr)
        off = r0 - r + (kr - 1)
        qr = qn_ref[pl.ds(pl.multiple_of(r * w, w), w), :]
        kw = kn_ref[pl.ds(pl.multiple_of(r0 * w, w), nk), :]
        vw = vn_ref[pl.ds(pl.multiple_of(r0 * w, w), nk), :]
        s_lat = _nt_dot(qr, kw) * scale + bias_ref[off]
        s_lat = jnp.where(in_win, s_lat, NEG_INF)
        s_ctx = _nt_dot(qr, kc) * scale
        m = jnp.maximum(jnp.max(s_lat, axis=-1, keepdims=True), jnp.max(s_ctx, axis=-1, keepdims=True))
        p_lat = jnp.exp(s_lat - m)
        p_ctx = jnp.exp(s_ctx - m)
        den = jnp.sum(p_lat, axis=-1, keepdims=True) + jnp.sum(p_ctx, axis=-1, keepdims=True)
        out = (jnp.dot(p_lat.astype(BF16), vw, preferred_element_type=F32)
               + jnp.dot(p_ctx.astype(BF16), vc, preferred_element_type=F32))
        o_ref[pl.ds(pl.multiple_of(r * w, w), w), :] = (out / den).astype(o_ref.dtype)
        return carry

    lax.fori_loop(0, rows, body, 0, unroll=4)


def _na_bias_table(rel_bias, kr):
    w = GRID_W
    n_dc = 2 * NA_COLS - 1
    dc = np.clip(np.arange(w)[None, :] - np.arange(w)[:, None] + NA_COLS - 1, 0, n_dc - 1)
    onehot = jnp.asarray((dc[:, :, None] == np.arange(n_dc)).astype(np.float32))
    toep = jnp.einsum("hrd,ckd->hrck", rel_bias.astype(F32), onehot, precision=lax.Precision.HIGHEST)
    r_lo = NA_ROWS - kr
    tab = jnp.stack([toep[:, r_lo + off:r_lo + off + kr] for off in range(kr)], axis=1)
    return jnp.transpose(tab, (0, 1, 3, 2, 4)).reshape(rel_bias.shape[0], kr, w, kr * w)


def _na_latent(p, q_gain, k_gain, bias_tab, dims):
    b, l, lc = dims["B"], dims["L"], dims["Lc"]
    rows = l // GRID_W
    kr = min(NA_ROWS, rows)
    nk = kr * GRID_W
    hd = NA_DH
    kern = functools.partial(_na_kernel, rows=rows, kr=kr)
    ctx_blk = (b * l) // lc
    return pl.pallas_call(
        kern,
        grid=(b, NA_HEADS),
        in_specs=[pl.BlockSpec((l, hd), lambda i, h: (i, OFF_NA_Q // hd + h)),
                  pl.BlockSpec((l, hd), lambda i, h: (i, OFF_NA_K // hd + h)),
                  pl.BlockSpec((l, hd), lambda i, h: (i, OFF_NA_V // hd + h)),
                  pl.BlockSpec((lc, hd), lambda i, h: (ctx_blk + i, OFF_NA_K // hd + h)),
                  pl.BlockSpec((lc, hd), lambda i, h: (ctx_blk + i, OFF_NA_V // hd + h)),
                  pl.BlockSpec((1, hd), lambda i, h: (0, 0)),
                  pl.BlockSpec((1, hd), lambda i, h: (0, 0)),
                  pl.BlockSpec((None, kr, GRID_W, nk), lambda i, h: (h, 0, 0, 0))],
        out_specs=pl.BlockSpec((l, hd), lambda i, h: (i, h)),
        out_shape=jax.ShapeDtypeStruct((b * l, NA_W), BF16),
        scratch_shapes=[pltpu.VMEM((l, hd), BF16), pltpu.VMEM((l, hd), BF16), pltpu.VMEM((l, hd), BF16)],
        compiler_params=_cparams("parallel", "parallel"),
        name="na_latent",
    )(p, p, p, p, p, q_gain.reshape(1, hd), k_gain.reshape(1, hd), bias_tab)


def _ctx_attn_kernel(q_ref, k_ref, v_ref, qg_ref, kg_ref, o_ref):
    q = _head_rms(q_ref[...], qg_ref[...]).astype(BF16)
    k = _head_rms(k_ref[...], kg_ref[...]).astype(BF16)
    s = _nt_dot(q, k) * (NA_DH ** -0.5)
    m = jnp.max(s, axis=-1, keepdims=True)
    pr = jnp.exp(s - m)
    den = jnp.sum(pr, axis=-1, keepdims=True)
    out = jnp.dot(pr.astype(BF16), v_ref[...].astype(BF16), preferred_element_type=F32)
    o_ref[...] = (out / den).astype(o_ref.dtype)


def _na_context(p, q_gain, k_gain, dims):
    b, l, lc = dims["B"], dims["L"], dims["Lc"]
    hd = NA_DH
    ctx_blk = (b * l) // lc
    return pl.pallas_call(
        _ctx_attn_kernel,
        grid=(b, NA_HEADS),
        in_specs=[pl.BlockSpec((lc, hd), lambda i, h: (ctx_blk + i, OFF_NA_Q // hd + h)),
                  pl.BlockSpec((lc, hd), lambda i, h: (ctx_blk + i, OFF_NA_K // hd + h)),
                  pl.BlockSpec((lc, hd), lambda i, h: (ctx_blk + i, OFF_NA_V // hd + h)),
                  pl.BlockSpec((1, hd), lambda i, h: (0, 0)),
                  pl.BlockSpec((1, hd), lambda i, h: (0, 0))],
        out_specs=pl.BlockSpec((lc, hd), lambda i, h: (i, h)),
        out_shape=jax.ShapeDtypeStruct((b * lc, NA_W), BF16),
        compiler_params=_cparams("parallel", "parallel"),
        name="na_context",
    )(p, p, p, q_gain.reshape(1, hd), k_gain.reshape(1, hd))


def _rope_tables(seq, pad_rows):
    nf = ML_DK // 4
    t = np.arange(seq)
    pos = np.stack([t // GRID_W, t % GRID_W], axis=-1).astype(np.float32)
    inv_freq = (ROPE_BASE ** (-np.arange(nf, dtype=np.float32) / nf)).astype(np.float32)
    lane = np.arange(ML_DK)
    axis, pair, f = lane // (2 * nf), (lane // nf) % 2, lane % nf
    ang = jnp.asarray(pos[:, axis]) * jnp.asarray(inv_freq[f])[None, :]
    cos, sin = jnp.cos(ang), jnp.sin(ang)
    s_lo = jnp.where(jnp.asarray(pair == 0)[None, :], -sin, 0.0)
    s_hi = jnp.where(jnp.asarray(pair == 1)[None, :], sin, 0.0)
    pad = lambda a, v: jnp.concatenate([a, jnp.full((pad_rows, ML_DK), v, F32)], axis=0)
    return pad(cos, 1.0), pad(s_lo, 0.0), pad(s_hi, 0.0)


def _mlprep_kernel(x_ref, xp_ref, xn_ref, v0_ref, v1_ref, w_ref, cos_ref, slo_ref, shi_ref,
                   q_ref, kt_ref, v_ref, *, n_lat, seq, seq_ctx):
    i, j = pl.program_id(0), pl.program_id(1)
    t = x_ref.shape[0]
    nf = ML_DK // 4
    r0 = i * t
    is_lat = r0 < n_lat
    starts = jnp.where(is_lat, r0 % seq == 0, (r0 - n_lat) % seq_ctx == 0)
    ends = jnp.where(is_lat, (r0 + t) % seq == 0, (r0 + t - n_lat) % seq_ctx == 0)
    row = lax.broadcasted_iota(I32, (t, ML_DK), 0)

    def rope(h):
        cs = slice(h * ML_DK, (h + 1) * ML_DK)
        x = x_ref[:, cs]
        prev_row = jnp.where(starts, 0.0, xp_ref[7:8, cs])
        next_row = jnp.where(ends, 0.0, xn_ref[0:1, cs])
        x_prev = jnp.where(row == 0, prev_row, pltpu.roll(x, 1, 0))
        x_next = jnp.where(row == t - 1, next_row, pltpu.roll(x, t - 1, 0))
        yh = jax.nn.silu(x_prev * w_ref[0:1, cs] + x * w_ref[1:2, cs] + x_next * w_ref[2:3, cs])
        return (yh * cos_ref[...] + pltpu.roll(yh, ML_DK - nf, 1) * slo_ref[...]
                + pltpu.roll(yh, nf, 1) * shi_ref[...])

    @pl.when(j == 0)
    def _():
        half = ML_HEADS // 2
        for h in range(ML_HEADS):
            q_ref[h] = rope(h).astype(q_ref.dtype)
            hv = h % half
            v_src = v0_ref if h < half else v1_ref
            v_ref[h] = v_src[:, hv * ML_DV:(hv + 1) * ML_DV].astype(v_ref.dtype)

    @pl.when(j == 1)
    def _():
        for h in range(ML_HEADS):
            kt_ref[h] = (rope(h) * (ML_DK ** -0.5)).T.astype(kt_ref.dtype)


def _ml_prep(p, conv_w, tables, dims):
    m = p.shape[0]
    t = ROW_TILE
    n_lat, l, lc = dims["n_lat"], dims["L"], dims["Lc"]
    hd = ML_DK
    wb = ML_QK
    col0 = OFF_ML_QK // wb
    nblk8 = m // 8
    lat_blocks = l // t

    def tab_idx(i, j):
        return (jnp.where(i * t < n_lat, i % lat_blocks, lat_blocks), 0)

    kern = functools.partial(_mlprep_kernel, n_lat=n_lat, seq=l, seq_ctx=lc)
    return pl.pallas_call(
        kern,
        grid=(m // t, 2),
        in_specs=[pl.BlockSpec((t, wb), lambda i, j: (i, col0 + j)),
                  pl.BlockSpec((8, wb), lambda i, j: (jnp.maximum(i * (t // 8) - 1, 0), col0 + j)),
                  pl.BlockSpec((8, wb), lambda i, j: (jnp.minimum((i + 1) * (t // 8), nblk8 - 1), col0 + j)),
                  pl.BlockSpec((t, HALF_W), lambda i, j: (i, OFF_ML_V // HALF_W)),
                  pl.BlockSpec((t, HALF_W), lambda i, j: (i, OFF_ML_V // HALF_W + 1)),
                  pl.BlockSpec((conv_w.shape[0], wb), lambda i, j: (0, j)),
                  pl.BlockSpec((t, hd), tab_idx),
                  pl.BlockSpec((t, hd), tab_idx),
                  pl.BlockSpec((t, hd), tab_idx)],
        out_specs=[pl.BlockSpec((ML_HEADS, t, ML_DK), lambda i, j: (0, i, 0)),
                   pl.BlockSpec((ML_HEADS, ML_DK, t), lambda i, j: (0, 0, i)),
                   pl.BlockSpec((ML_HEADS, t, ML_DV), lambda i, j: (0, i, 0))],
        out_shape=[jax.ShapeDtypeStruct((ML_HEADS, m, ML_DK), BF16),
                   jax.ShapeDtypeStruct((ML_HEADS, ML_DK, m), BF16),
                   jax.ShapeDtypeStruct((ML_HEADS, m, ML_DV), BF16)],
        compiler_params=_cparams("arbitrary", "arbitrary"),
        name="ml_prep",
    )(p, p, p, p, p, conv_w, *tables)


ML_AUG = ML_DV + LANES


def _mlstm_kernel(q_ref, kt_ref, v_ref, gr_ref, o_ref, c_ref, m_ref, cum_ref):
    d, s = pl.program_id(1), pl.program_id(2)
    t = ML_CHUNK
    nh = ML_HEADS

    @pl.when(s == 0)
    def _():
        c_ref[...] = jnp.zeros_like(c_ref)
        m_ref[...] = jnp.zeros_like(m_ref)

    fwd = d == 0
    ri = lax.broadcasted_iota(I32, (t, t), 0)
    ci = lax.broadcasted_iota(I32, (t, t), 1)
    tri = (ci - ri) * jnp.where(fwd, 1, -1) <= 0
    trif = tri.astype(F32)
    cum_ref[...] = lax.dot_general(gr_ref[...], trif, (((1,), (1,)), ((), ())), preferred_element_type=F32,
                                   precision=lax.Precision.HIGHEST)
    row_i = jnp.where(fwd, 0, 2 * nh)
    row_f = row_i + nh
    ones = jnp.ones((t, LANES), BF16)

    def head(h, carry):
        lir = gr_ref[pl.ds(row_i + h, 1), :]
        lfr = gr_ref[pl.ds(row_f + h, 1), :]
        br = cum_ref[pl.ds(row_f + h, 1), :]
        bc = jnp.sum(trif * lfr, axis=-1, keepdims=True)
        blh = jnp.sum(lfr, axis=-1, keepdims=True)
        m_h = m_ref[pl.ds(h, 1), 0:1]
        a = bc + m_h
        dlog = jnp.where(tri, bc - br + lir, NEG_INF)
        mj = jnp.maximum(a, jnp.max(dlog, axis=-1, keepdims=True))
        w_inter = jnp.exp(a - mj)
        qh = q_ref[h]
        kth = kt_ref[h]
        vh = v_ref[h]
        sm = jnp.dot(qh, kth, preferred_element_type=F32) * jnp.exp(dlog - mj)
        c_h = c_ref[h]
        qc = jnp.dot(qh, c_h.astype(BF16), preferred_element_type=F32)
        num = w_inter * qc[:, :ML_DV] + jnp.dot(sm.astype(BF16), vh, preferred_element_type=F32)
        den = w_inter * qc[:, ML_DV:ML_DV + 1] + jnp.sum(sm, axis=-1, keepdims=True)
        o_ref[h] = num / jnp.maximum(jnp.abs(den), jnp.exp(-mj))
        gl_r = blh - br + lir
        m_new = jnp.maximum(blh + m_h, jnp.max(gl_r, axis=-1, keepdims=True))
        sc = jnp.exp(blh + m_h - m_new)
        ktw = (kth.astype(F32) * jnp.exp(gl_r - m_new)).astype(BF16)
        v_aug = jnp.concatenate([vh, ones], axis=1)
        c_ref[h] = sc * c_h + jnp.dot(ktw, v_aug, preferred_element_type=F32)
        m_ref[pl.ds(h, 1), :] = jnp.broadcast_to(m_new, (1, LANES))
        return carry

    lax.fori_loop(0, nh, head, 0, unroll=8)


def _mlstm_scan(q3, kt3, v3, gates_t, dims):
    m = q3.shape[1]
    t = ML_CHUNK
    b, l, lc, n_lat = dims["B"], dims["L"], dims["Lc"], dims["n_lat"]
    ncl, ncc = l // t, lc // t
    ng = gates_t.shape[0]

    def chunk(i, d, s):
        in_ctx = s < ncc
        c_ctx = jnp.where(d == 0, s, ncc - 1 - s)
        s_lat = s - ncc
        c_lat = jnp.where(d == 0, s_lat, ncl - 1 - s_lat)
        return jnp.where(in_ctx, n_lat // t + i * ncc + c_ctx, i * ncl + c_lat)

    return pl.pallas_call(
        _mlstm_kernel,
        grid=(b, 2, ncc + ncl),
        in_specs=[pl.BlockSpec((ML_HEADS, t, ML_DK), lambda i, d, s: (0, chunk(i, d, s), 0)),
                  pl.BlockSpec((ML_HEADS, ML_DK, t), lambda i, d, s: (0, 0, chunk(i, d, s))),
                  pl.BlockSpec((ML_HEADS, t, ML_DV), lambda i, d, s: (0, chunk(i, d, s), 0)),
                  pl.BlockSpec((ng, t), lambda i, d, s: (0, chunk(i, d, s)))],
        out_specs=pl.BlockSpec((None, ML_HEADS, t, ML_DV), lambda i, d, s: (d, 0, chunk(i, d, s), 0)),
        out_shape=jax.ShapeDtypeStruct((2, ML_HEADS, m, ML_DV), F32),
        scratch_shapes=[pltpu.VMEM((ML_HEADS, ML_DK, ML_AUG), F32),
                        pltpu.VMEM((ML_HEADS, LANES), F32),
                        pltpu.VMEM((ng, t), F32)],
        compiler_params=_cparams("parallel", "parallel", "arbitrary"),
        name="mlstm_scan",
    )(q3, kt3, v3, gates_t)


def _mlout_kernel(hf_ref, hb_ref, og_ref, g_ref, y_ref):
    for h in range(HALF_W // ML_DV):
        sl = slice(h * ML_DV, (h + 1) * ML_DV)
        x = hf_ref[h] + hb_ref[h]
        y = x * lax.rsqrt(jnp.mean(x * x, axis=-1, keepdims=True) + EPS) * g_ref[:, sl]
        y_ref[:, sl] = (y * jax.nn.sigmoid(og_ref[:, sl])).astype(y_ref.dtype)


def _ml_out(hs, p, gain, rows):
    tm = ROW_TILE
    wb = HALF_W
    hb = wb // ML_DV
    return pl.pallas_call(
        _mlout_kernel,
        grid=(rows // tm, ML_V // wb),
        in_specs=[pl.BlockSpec((None, hb, tm, ML_DV), lambda i, j: (0, j, i, 0)),
                  pl.BlockSpec((None, hb, tm, ML_DV), lambda i, j: (1, j, i, 0)),
                  pl.BlockSpec((tm, wb), lambda i, j: (i, OFF_ML_O // wb + j)),
                  pl.BlockSpec((1, wb), lambda i, j: (0, j))],
        out_specs=pl.BlockSpec((tm, wb), lambda i, j: (i, j)),
        out_shape=jax.ShapeDtypeStruct((rows, ML_V), BF16),
        compiler_params=_cparams("parallel", "parallel"),
        name="ml_out",
    )(hs, hs, p, gain.reshape(1, ML_V))


def _gelu_exact(x):
    return 0.5 * x * (1.0 + lax.erf(x * np.float32(np.sqrt(0.5))))


def _gmlp_kernel(u_ref, v_ref, g_ref, ws_ref, bt_ref, o_ref):
    u = _gelu_exact(u_ref[...])
    v = _gelu_exact(v_ref[...])
    vn = (v * lax.rsqrt(jnp.mean(v * v, axis=-1, keepdims=True) + EPS) * g_ref[...]).astype(BF16)
    bt = bt_ref[...]
    for c in range(u.shape[0] // GM_CHUNK):
        rs = slice(c * GM_CHUNK, (c + 1) * GM_CHUNK)
        for g in range(GM_GROUPS):
            cs = slice(g * GM_CPG, (g + 1) * GM_CPG)
            mixed = jnp.dot(ws_ref[g].astype(BF16), vn[rs, cs], preferred_element_type=F32) + bt[:, g:g + 1]
            o_ref[rs, cs] = (u[rs, cs] * mixed).astype(o_ref.dtype)


def _gmlp(p, v_gain, ws, bias, rows):
    tm = ROW_TILE
    return pl.pallas_call(
        _gmlp_kernel,
        grid=(rows // tm,),
        in_specs=[pl.BlockSpec((tm, GM_W), lambda i: (i, OFF_GM_U // GM_W)),
                  pl.BlockSpec((tm, GM_W), lambda i: (i, OFF_GM_V // GM_W)),
                  pl.BlockSpec((1, GM_W), lambda i: (0, 0)),
                  pl.BlockSpec((GM_GROUPS, GM_CHUNK, GM_CHUNK), lambda i: (0, 0, 0)),
                  pl.BlockSpec((GM_CHUNK, GM_GROUPS), lambda i: (0, 0))],
        out_specs=pl.BlockSpec((tm, GM_W), lambda i: (i, 0)),
        out_shape=jax.ShapeDtypeStruct((rows, GM_W), BF16),
        compiler_params=_cparams("parallel"),
        name="gmlp",
    )(p, p, v_gain.reshape(1, GM_W), ws, bias.T)


def _branch_kernel(ya_ref, ym_ref, yg_ref, w_ref, g0_ref, g1_ref, g2_ref, o_ref, wp_ref):
    @pl.when(pl.program_id(1) == 0)
    def _():
        wp_ref[...] = w_ref[...].astype(BF16)

    dot = functools.partial(jnp.dot, preferred_element_type=F32)
    za = dot(ya_ref[...], wp_ref[0:NA_W, :])
    zm = dot(ym_ref[...], wp_ref[NA_W:NA_W + ML_V, :])
    zg = dot(yg_ref[...], wp_ref[NA_W + ML_V:, :])
    z = (jax.nn.sigmoid(g0_ref[...]) * za + jax.nn.sigmoid(g1_ref[...]) * zm
         + jax.nn.sigmoid(g2_ref[...]) * zg)
    o_ref[...] = z.astype(o_ref.dtype)


def _branch(ya, ym, yg, w_branch, layer, p, d_model, rows):
    tm, tn = MM_TM, 512
    k = w_branch.shape[1]
    gate_blk = lambda b: (OFF_GATE + b * d_model) // tn
    return pl.pallas_call(
        _branch_kernel,
        grid=(d_model // tn, rows // tm),
        in_specs=[pl.BlockSpec((tm, NA_W), lambda j, i: (i, 0)),
                  pl.BlockSpec((tm, ML_V), lambda j, i: (i, 0)),
                  pl.BlockSpec((tm, GM_W), lambda j, i: (i, 0)),
                  pl.BlockSpec((None, k, tn), lambda j, i: (layer, 0, j)),
                  pl.BlockSpec((tm, tn), lambda j, i: (i, gate_blk(0) + j)),
                  pl.BlockSpec((tm, tn), lambda j, i: (i, gate_blk(1) + j)),
                  pl.BlockSpec((tm, tn), lambda j, i: (i, gate_blk(2) + j))],
        out_specs=pl.BlockSpec((tm, tn), lambda j, i: (i, j)),
        out_shape=jax.ShapeDtypeStruct((rows, d_model), BF16),
        scratch_shapes=[pltpu.VMEM((k, tn), BF16)],
        compiler_params=_cparams("parallel", "arbitrary"),
        name="branch_merge",
    )(ya, ym, yg, w_branch, p, p, p)


def _outproj_kernel(z_ref, w_ref, x_ref, gt_ref, o_ref, wp_ref):
    @pl.when(pl.program_id(1) == 0)
    def _():
        wp_ref[...] = w_ref[...].astype(BF16)

    y = jnp.dot(z_ref[...], wp_ref[...], preferred_element_type=F32)
    o_ref[...] = x_ref[...] + gt_ref[...] * y


def _outproj(z, w_out, layer, x, mod3, k_gate, dims, rows):
    k = z.shape[1]
    d = w_out.shape[2]
    tm, tn = MM_TM, 512
    mrow = functools.partial(_mod_row, tm=tm, n_lat=dims["n_lat"], seq=dims["L"], n_batch=dims["B"])
    gblk = k_gate * (d // tn)
    return pl.pallas_call(
        _outproj_kernel,
        grid=(d // tn, rows // tm),
        in_specs=[pl.BlockSpec((tm, k), lambda j, i: (i, 0)),
                  pl.BlockSpec((None, k, tn), lambda j, i: (layer, 0, j)),
                  pl.BlockSpec((tm, tn), lambda j, i: (i, j)),
                  pl.BlockSpec((None, 1, tn), lambda j, i: (mrow(i), 0, gblk + j))],
        out_specs=pl.BlockSpec((tm, tn), lambda j, i: (i, j)),
        out_shape=jax.ShapeDtypeStruct((rows, d), F32),
        scratch_shapes=[pltpu.VMEM((k, tn), BF16)],
        compiler_params=_cparams("parallel", "arbitrary"),
        name="out_proj",
    )(z, w_out, x, mod3)


def _affinity_kernel(h_ref, w_ref, o_ref, *, n_exp):
    logits = jnp.dot(_unpack_bf16_pairs(h_ref[...]), w_ref[...], preferred_element_type=F32)
    col = lax.broadcasted_iota(I32, logits.shape, 1)
    logits = jnp.where(col < n_exp, logits, NEG_INF)
    o_ref[...] = jax.nn.softmax(logits, axis=-1)


def _affinity(h2, w_router, rows):
    d, e = w_router.shape
    tm = MM_TM
    w_pad = jnp.pad(w_router, ((0, 0), (0, LANES - e))).astype(BF16)
    return pl.pallas_call(
        functools.partial(_affinity_kernel, n_exp=e),
        grid=(rows // tm,),
        in_specs=[pl.BlockSpec((tm, d // 2), lambda i: (i, 0)),
                  pl.BlockSpec((d, LANES), lambda i: (0, 0))],
        out_specs=pl.BlockSpec((tm, LANES), lambda i: (i, 0)),
        out_shape=jax.ShapeDtypeStruct((rows, LANES), F32),
        compiler_params=_cparams("parallel"),
        name="router_affinity",
    )(h2, w_pad)


def _cumsum_rows(x01, blk):
    n = x01.shape[0]
    ri = lax.broadcasted_iota(I32, (blk, blk), 0)
    ci = lax.broadcasted_iota(I32, (blk, blk), 1)
    tril = (ci <= ri).astype(BF16)
    parts = []
    carry = jnp.zeros((1, x01.shape[1]), F32)
    for j in range(n // blk):
        cs = jnp.dot(tril, x01[j * blk:(j + 1) * blk, :].astype(BF16), preferred_element_type=F32) + carry
        parts.append(cs)
        carry = cs[blk - 1:blk, :]
    return jnp.concatenate(parts, axis=0) if len(parts) > 1 else parts[0]


def _select_kernel(aff_ref, idx_ref, gate_ref, rank_ref, *, cap, tb, n_exp):
    aff = aff_ref[...]
    n = aff.shape[0]
    key = lax.bitcast_convert_type(aff, I32)
    capf = jnp.float32(cap)

    def search(i, thr):
        cand = thr | jnp.left_shift(jnp.int32(1), 30 - i)
        cnt = jnp.sum((key >= cand).astype(F32), axis=0, keepdims=True)
        return jnp.where(cnt >= capf, cand, thr)

    thr = lax.fori_loop(0, 31, search, jnp.zeros((1, LANES), I32))
    above = key > thr
    tied = key == thr
    need = capf - jnp.sum(above.astype(F32), axis=0, keepdims=True)
    tied_f = tied.astype(F32)
    tie_rank = _cumsum_rows(tied_f, tb) - tied_f
    sel = above | (tied & (tie_rank < need))
    incl = _cumsum_rows(sel.astype(F32), tb)
    rank_ref[0] = incl
    rank_ref[1] = jnp.where(sel, incl, -1.0)
    rank_ref[2] = aff
    slot = lax.broadcasted_iota(I32, (tb, cap), 1).astype(F32)

    for ex in range(n_exp):
        def block(j, acc):
            acc_i, acc_g = acc
            rows = pl.ds(pl.multiple_of(j * tb, tb), tb)
            inc = rank_ref[0, rows, ex:ex + 1]
            inc_sel = rank_ref[1, rows, ex:ex + 1]
            a = rank_ref[2, rows, ex:ex + 1]
            acc_i = acc_i + jnp.sum(jnp.where(inc <= slot, 1.0, 0.0), axis=0, keepdims=True)
            acc_g = acc_g + jnp.sum(jnp.where(inc_sel == slot + 1.0, a, 0.0), axis=0, keepdims=True)
            return acc_i, acc_g

        zero = jnp.zeros((1, cap), F32)
        acc_i, acc_g = lax.fori_loop(0, n // tb, block, (zero, zero))
        idx_ref[ex:ex + 1, :] = acc_i.astype(I32)
        gate_ref[ex:ex + 1, :] = acc_g


def _select(aff, e, n_groups, group_len, blk0, cap):
    tb = min(256, group_len)
    kern = functools.partial(_select_kernel, cap=cap, tb=tb, n_exp=e)
    return pl.pallas_call(
        kern,
        grid=(n_groups,),
        in_specs=[pl.BlockSpec((group_len, LANES), lambda g: (blk0 + g, 0))],
        out_specs=[pl.BlockSpec((None, e, cap), lambda g: (g, 0, 0)),
                   pl.BlockSpec((None, e, cap), lambda g: (g, 0, 0))],
        out_shape=[jax.ShapeDtypeStruct((n_groups, e, cap), I32),
                   jax.ShapeDtypeStruct((n_groups, e, cap), F32)],
        scratch_shapes=[pltpu.VMEM((3, group_len, LANES), F32)],
        compiler_params=_cparams("parallel"),
        name="router_select",
    )(aff)


def _moe_up_kernel(idx_ref, h_hbm, wg_ref, wu_ref, o_ref, xs_ref, xb_ref, sem, *, n_rows, n_exp, n_f):
    e, f = pl.program_id(0), pl.program_id(1)
    per_step = n_rows // n_f

    def row_copy(ex, i):
        slot = ex % 2
        src = idx_ref[ex * n_rows + i]
        return pltpu.make_async_copy(h_hbm.at[pl.ds(src, 1), :], xs_ref.at[slot, pl.ds(i, 1), :], sem.at[slot])

    def start_rows(ex, lo, n):
        def start(i, c):
            row_copy(ex, lo + i).start()
            return c
        lax.fori_loop(0, n, start, 0, unroll=4)

    @pl.when(jnp.logical_and(e == 0, f == 0))
    def _():
        start_rows(0, 0, n_rows)

    @pl.when(f == 0)
    def _():
        def wait(i, c):
            row_copy(e, i).wait()
            return c
        lax.fori_loop(0, n_rows, wait, 0, unroll=4)
        xb_ref[...] = _unpack_bf16_pairs(xs_ref[e % 2])

    @pl.when(e + 1 < n_exp)
    def _():
        start_rows(e + 1, f * per_step, per_step)

    xb = xb_ref[...]
    a = jnp.dot(xb, wg_ref[...].astype(BF16), preferred_element_type=F32)
    u = jnp.dot(xb, wu_ref[...].astype(BF16), preferred_element_type=F32)
    o_ref[...] = (jax.nn.silu(a) * u).astype(o_ref.dtype)


def _moe_up(idx_flat, h2_packed, w_gate, w_up, layer, n_rows):
    _, e, d, ff = w_gate.shape
    tf = 256
    n_f = ff // tf
    assert n_rows % n_f == 0
    kern = functools.partial(_moe_up_kernel, n_rows=n_rows, n_exp=e, n_f=n_f)
    return pl.pallas_call(
        kern,
        grid_spec=pltpu.PrefetchScalarGridSpec(
            num_scalar_prefetch=1,
            grid=(e, n_f),
            in_specs=[pl.BlockSpec(memory_space=pl.ANY),
                      pl.BlockSpec((None, None, d, tf), lambda i, f, idx: (layer, i, 0, f)),
                      pl.BlockSpec((None, None, d, tf), lambda i, f, idx: (layer, i, 0, f))],
            out_specs=pl.BlockSpec((None, n_rows, tf), lambda i, f, idx: (i, 0, f)),
            scratch_shapes=[pltpu.VMEM((2, n_rows, d // 2), jnp.uint32), pltpu.VMEM((n_rows, d), BF16),
                            pltpu.SemaphoreType.DMA((2,))]),
        out_shape=jax.ShapeDtypeStruct((e, n_rows, ff), BF16),
        compiler_params=_cparams("arbitrary", "arbitrary"),
        name="moe_up",
    )(idx_flat, h2_packed, w_gate, w_up)


def _moe_down_kernel(idx_ref, hid_ref, gcol_ref, wd_ref, gt_ref, x_hbm, o_hbm, acc_ref, y_ref, sem, *,
                     cap, n_exp, group_len, row0, dc):
    g, j, e = pl.program_id(0), pl.program_id(1), pl.program_id(2)
    nlt = dc // LANES
    sub = 8
    rows = pl.ds(pl.multiple_of((row0 + g * group_len) // sub, group_len // sub), group_len // sub)

    def tile_copy(c, to_vmem):
        hbm = (x_hbm if to_vmem else o_hbm).at[rows, :, pl.ds(pl.multiple_of(j * dc + c * LANES, LANES), LANES)]
        vmem = acc_ref.at[:, c]
        return pltpu.make_async_copy(hbm, vmem, sem.at[0]) if to_vmem else pltpu.make_async_copy(vmem, hbm, sem.at[1])

    @pl.when(e == 0)
    def _():
        for c in range(nlt):
            tile_copy(c, True).start()

    chunk = min(cap, 128)
    w = wd_ref[...].astype(BF16)
    gt = gt_ref[...]
    for rc in range(cap // chunk):
        rs = slice(rc * chunk, (rc + 1) * chunk)
        y = jnp.dot(hid_ref[rs, :], w, preferred_element_type=F32) * gcol_ref[rs, :] * gt
        for rr in range(chunk // sub):
            for c in range(nlt):
                y_ref[rc * chunk // sub + rr, c] = y[rr * sub:(rr + 1) * sub, c * LANES:(c + 1) * LANES]

    @pl.when(e == 0)
    def _():
        for c in range(nlt):
            tile_copy(c, True).wait()

    base = (g * n_exp + e) * cap
    acc_flat = acc_ref.reshape(group_len * nlt, LANES)
    y_flat = y_ref.reshape(cap * nlt, LANES)

    def add_rows(bi, carry):
        i0 = bi * SCATTER_BATCH
        starts = [idx_ref[base + i0 + r] for r in range(SCATTER_BATCH)]
        acc = [acc_flat[pl.ds(s, nlt, stride=sub), :] for s in starts]
        for r in range(SCATTER_BATCH):
            yr = y_flat[pl.ds(i0 * nlt + r, nlt, stride=sub), :]
            acc_flat[pl.ds(starts[r], nlt, stride=sub), :] = acc[r] + yr
        return carry

    lax.fori_loop(0, cap // SCATTER_BATCH, add_rows, 0)

    @pl.when(e == n_exp - 1)
    def _():
        for c in range(nlt):
            tile_copy(c, False).start()
        for c in range(nlt):
            tile_copy(c, False).wait()


def _moe_down(idx_flat, hid, gate_col, w_down, layer, x, mod3, k_gate, n_groups, group_len, cap,
              hid_row0, x_row0, mod_row_of_group, out_rows):
    _, e, ff, d = w_down.shape
    assert SCATTER_BATCH == 8 and cap % 8 == 0 and x_row0 % group_len == 0 and out_rows % 8 == 0
    dc = 1024
    nlt = dc // LANES
    kern = functools.partial(_moe_down_kernel, cap=cap, n_exp=e, group_len=group_len, row0=x_row0, dc=dc)
    hblk0 = hid_row0 // cap
    gblk = k_gate * (d // dc)
    out = pl.pallas_call(
        kern,
        grid_spec=pltpu.PrefetchScalarGridSpec(
            num_scalar_prefetch=1,
            grid=(n_groups, d // dc, e),
            in_specs=[pl.BlockSpec((None, cap, ff), lambda g, j, i, idx: (i, hblk0 + g, 0)),
                      pl.BlockSpec((None, cap, 1), lambda g, j, i, idx: (i, hblk0 + g, 0)),
                      pl.BlockSpec((None, None, ff, dc), lambda g, j, i, idx: (layer, i, 0, j)),
                      pl.BlockSpec((None, 1, dc), lambda g, j, i, idx: (mod_row_of_group(g), 0, gblk + j)),
                      pl.BlockSpec(memory_space=pl.ANY)],
            out_specs=pl.BlockSpec(memory_space=pl.ANY),
            scratch_shapes=[pltpu.VMEM((group_len // 8, nlt, 8, LANES), F32),
                            pltpu.VMEM((cap // 8, nlt, 8, LANES), F32),
                            pltpu.SemaphoreType.DMA((2,))]),
        out_shape=jax.ShapeDtypeStruct((out_rows // 8, 8, d), F32),
        input_output_aliases={5: 0},
        compiler_params=_cparams("arbitrary", "arbitrary", "arbitrary"),
        name="moe_down",
    )((idx_flat >> 3) * (8 * nlt) + (idx_flat & 7), hid, gate_col, w_down, mod3, x.reshape(out_rows // 8, 8, d))
    return out.reshape(out_rows, d)


def _layer(x, mod_l, layer, last, dims, w_in, na_q_gain, na_k_gain, na_rel_bias, ml_conv, ml_gate_bias,
           ml_out_gain, gm_v_gain, gm_ws, gm_bias, w_branch, w_out, g_norm1, g_norm2, w_router, w_e_gate, w_e_up,
           w_e_down, rope_tabs):
    b, l, lc, n_lat = dims["B"], dims["L"], dims["Lc"], dims["n_lat"]
    m, d = x.shape
    rows = n_lat if last else m
    mod3 = mod_l.reshape(mod_l.shape[0], 1, mod_l.shape[1])
    n_exp = w_router.shape[1]

    h = _prenorm(x, g_norm1, mod3, 0, 1, dims, m, False)
    p = _in_proj(h, w_in, layer)
    b_g = jnp.pad(ml_gate_bias.reshape(1, N_ML_GATES), ((0, 0), (0, LANES - N_ML_GATES)))
    gates = _ml_gates(h, w_in, layer, b_g)
    gates_t = gates[:, :N_ML_GATES].T

    kr = min(NA_ROWS, l // GRID_W)
    ya = _na_latent(p, na_q_gain, na_k_gain, _na_bias_table(na_rel_bias, kr), dims)
    if not last:
        ya = jnp.concatenate([ya, _na_context(p, na_q_gain, na_k_gain, dims)], axis=0)

    q3, kt3, v3 = _ml_prep(p, ml_conv, rope_tabs, dims)
    hs = _mlstm_scan(q3, kt3, v3, gates_t, dims)
    ym = _ml_out(hs, p, ml_out_gain, rows)
    yg = _gmlp(p, gm_v_gain, gm_ws, gm_bias, rows)

    z = _branch(ya, ym, yg, w_branch, layer, p, d, rows)
    x1 = _outproj(z, w_out, layer, x, mod3, 2, dims, rows)

    h2 = _prenorm(x1, g_norm2, mod3, 3, 4, dims, rows, True)
    aff = _affinity(h2, w_router, rows)
    cap_l = max(1, min(l, (CAPACITY_FACTOR * l) // n_exp))
    idx_l, gate_l = _select(aff, n_exp, b, l, 0, cap_l)
    row_l = idx_l + (jnp.arange(b, dtype=I32) * l)[:, None, None]
    src = [jnp.swapaxes(row_l, 0, 1).reshape(n_exp, b * cap_l)]
    gsel = [jnp.swapaxes(gate_l, 0, 1).reshape(n_exp, b * cap_l)]
    if not last:
        cap_c = max(1, min(lc, (CAPACITY_FACTOR * lc) // n_exp))
        idx_c, gate_c = _select(aff, n_exp, b, lc, n_lat // lc, cap_c)
        row_c = idx_c + (n_lat + jnp.arange(b, dtype=I32) * lc)[:, None, None]
        src.append(jnp.swapaxes(row_c, 0, 1).reshape(n_exp, b * cap_c))
        gsel.append(jnp.swapaxes(gate_c, 0, 1).reshape(n_exp, b * cap_c))
    src = jnp.concatenate(src, axis=1)
    n_rows = src.shape[1]
    gate_col = jnp.concatenate(gsel, axis=1).reshape(n_exp, n_rows, 1)
    hid = _moe_up(src.reshape(-1), h2, w_e_gate, w_e_up, layer, n_rows)
    x2 = _moe_down(idx_l.reshape(-1), hid, gate_col, w_e_down, layer, x1, mod3, 5, b, l, cap_l,
                   0, 0, lambda g: g, rows)
    if not last:
        idx_cc = jnp.swapaxes(idx_c + (jnp.arange(b, dtype=I32) * lc)[:, None, None], 0, 1)
        x2 = _moe_down(idx_cc.reshape(-1), hid, gate_col, w_e_down, layer, x2, mod3, 5, 1, b * lc, b * cap_c,
                       b * cap_l, n_lat, lambda g: b, rows)
    return x2


def kernel(x, c, ctx, c_ctx, w_ada, b_ada, g_norm1, g_norm2, w_in, na_q_gain, na_k_gain, na_rel_bias, ml_conv,
           ml_gate_bias, ml_out_gain, gm_v_gain, gm_ws, gm_bias, w_branch, w_out, w_router, w_e_gate, w_e_up,
           w_e_down):
    b, l, d = x.shape
    lc = ctx.shape[1]
    depth = w_ada.shape[0]
    dims = {"B": b, "L": l, "Lc": lc, "n_lat": b * l}
    assert b + 1 <= 8 and l % ROW_TILE == 0 and lc % ROW_TILE == 0 and (b * l) % MM_TM == 0 and (b * lc) % MM_TM == 0

    c8 = jnp.concatenate([c, c_ctx[None, :], jnp.zeros((8 - b - 1, d), F32)], axis=0)
    mod = _ada(c8, w_ada, b_ada)
    xs = jnp.concatenate([x.reshape(b * l, d), ctx.reshape(b * lc, d)], axis=0)
    rope_tabs = _rope_tables(l, ROW_TILE)
    w_in_t = jnp.swapaxes(w_in, 1, 2)
    for i in range(depth):
        xs = _layer(xs, mod[i], i, i == depth - 1, dims, w_in_t, na_q_gain[i], na_k_gain[i], na_rel_bias[i],
                    ml_conv[i], ml_gate_bias[i], ml_out_gain[i], gm_v_gain[i], gm_ws[i], gm_bias[i], w_branch,
                    w_out, g_norm1[i], g_norm2[i], w_router[i], w_e_gate, w_e_up, w_e_down, rope_tabs)
    return xs[:b * l].reshape(b, l, d)
```

```python
import functools

import numpy as np
import jax
import jax.numpy as jnp
from jax import lax
from jax.experimental import pallas as pl
from jax.experimental.pallas import tpu as pltpu

F32 = jnp.float32
BF16 = jnp.bfloat16
I32 = jnp.int32

GRID_W = 64
NA_HEADS, NA_DH, NA_ROWS, NA_COLS = 8, 128, 8, 16
NA_W = NA_HEADS * NA_DH
ML_HEADS, ML_DK, ML_DV, ML_CHUNK = 8, 128, 256, 128
ML_QK, ML_V = ML_HEADS * ML_DK, ML_HEADS * ML_DV
GM_GROUPS, GM_CPG, GM_CHUNK = 8, 128, 128
GM_W = GM_GROUPS * GM_CPG
N_BRANCH = 3
CAPACITY_FACTOR = 2
ROPE_BASE = 10000.0
EPS = 1e-6
NEG_INF = -1e30

VMEM_LIMIT_BYTES = 56 * 1024 * 1024
LANES = 128
ROW_TILE = 256
MM_TM = 512
SCATTER_BATCH = 8

OFF_NA_Q, OFF_NA_K, OFF_NA_V = 0, NA_W, 2 * NA_W
OFF_ML_QK = 3 * NA_W
OFF_ML_V = OFF_ML_QK + 2 * ML_QK
OFF_ML_O = OFF_ML_V + ML_V
OFF_ML_GATES = OFF_ML_O + ML_V
N_ML_GATES = 4 * ML_HEADS
OFF_GM_U = OFF_ML_GATES
OFF_GM_V = OFF_GM_U + GM_W
OFF_GATE = OFF_GM_V + GM_W
HALF_W = 1024


def _cparams(*sem):
    return pltpu.CompilerParams(dimension_semantics=sem, vmem_limit_bytes=VMEM_LIMIT_BYTES)


def _nt_dot(a, b):
    return lax.dot_general(a, b, (((1,), (1,)), ((), ())), preferred_element_type=F32)


def _ada_kernel(c_ref, w_ref, b_ref, o_ref):
    a = jax.nn.silu(c_ref[...]).astype(BF16)
    o_ref[...] = jnp.dot(a, w_ref[...].astype(BF16), preferred_element_type=F32) + b_ref[...]


def _ada(c8, w_ada, b_ada):
    depth, d, n = w_ada.shape
    tn = 512
    return pl.pallas_call(
        _ada_kernel,
        grid=(depth, n // tn),
        in_specs=[pl.BlockSpec((8, d), lambda l, j: (0, 0)),
                  pl.BlockSpec((None, d, tn), lambda l, j: (l, 0, j)),
                  pl.BlockSpec((None, 1, tn), lambda l, j: (l, 0, j))],
        out_specs=pl.BlockSpec((None, 8, tn), lambda l, j: (l, 0, j)),
        out_shape=jax.ShapeDtypeStruct((depth, 8, n), F32),
        compiler_params=_cparams("parallel", "parallel"),
        name="ada_mod",
    )(c8, w_ada, b_ada.reshape(depth, 1, n))


def _mod_row(i, tm, n_lat, seq, n_batch):
    r = i * tm
    return jnp.where(r >= n_lat, n_batch, r // seq)


HI16 = np.uint32(0xFFFF0000)


def _pack_bf16_pairs(y):
    half = y.shape[1] // 2
    bits = lax.bitcast_convert_type(y.astype(BF16).astype(F32), jnp.uint32)
    return (bits[:, half:] & HI16) | (bits[:, :half] >> 16)


def _unpack_bf16_pairs(packed):
    lo = lax.bitcast_convert_type(packed << 16, F32).astype(BF16)
    hi = lax.bitcast_convert_type(packed & HI16, F32).astype(BF16)
    return jnp.concatenate([lo, hi], axis=1)


def _prenorm_kernel(x_ref, g_ref, sh_ref, sc_ref, o_ref, *, packed):
    x = x_ref[...]
    y = x * lax.rsqrt(jnp.mean(x * x, axis=-1, keepdims=True) + EPS)
    y = y * g_ref[...]
    y = y * (1 + sc_ref[...]) + sh_ref[...]
    o_ref[...] = _pack_bf16_pairs(y) if packed else y.astype(o_ref.dtype)


def _prenorm(x, g, mod3, k_shift, k_scale, dims, rows, packed):
    d = x.shape[1]
    tm = ROW_TILE
    d_out = d // 2 if packed else d
    mrow = functools.partial(_mod_row, tm=tm, n_lat=dims["n_lat"], seq=dims["L"], n_batch=dims["B"])
    return pl.pallas_call(
        functools.partial(_prenorm_kernel, packed=packed),
        grid=(rows // tm,),
        in_specs=[pl.BlockSpec((tm, d), lambda i: (i, 0)),
                  pl.BlockSpec((1, d), lambda i: (0, 0)),
                  pl.BlockSpec((None, 1, d), lambda i: (mrow(i), 0, k_shift)),
                  pl.BlockSpec((None, 1, d), lambda i: (mrow(i), 0, k_scale))],
        out_specs=pl.BlockSpec((tm, d_out), lambda i: (i, 0)),
        out_shape=jax.ShapeDtypeStruct((rows, d_out), jnp.uint32 if packed else BF16),
        compiler_params=_cparams("parallel"),
        name="prenorm",
    )(x, g.reshape(1, d), mod3, mod3)


IN_TN = 1024
IN_MAX_CHUNKS = 16


def _inproj_kernel(a_ref, wt_hbm, o_ref, wp_ref, stage_ref, sem, *, layer, n_panels, n_aligned, shift, n_chunks):
    p, i = pl.program_id(0), pl.program_id(1)
    tn = o_ref.shape[1]
    rows_c = tn // n_chunks
    building = jnp.logical_and(p < n_panels, i < n_chunks)

    def chunk_copy(c):
        r0 = p * tn + jnp.where(p >= n_aligned, shift, 0) + c * rows_c
        src = wt_hbm.at[layer, pl.ds(pl.multiple_of(r0, 8), rows_c), :]
        return pltpu.make_async_copy(src, stage_ref.at[c % 2], sem.at[c % 2])

    @pl.when(jnp.logical_and(building, i == 0))
    def _():
        chunk_copy(0).start()

    @pl.when(building)
    def _():
        chunk_copy(i).wait()

        @pl.when(i + 1 < n_chunks)
        def _():
            chunk_copy(i + 1).start()

        wp_ref[p % 2, pl.ds(pl.multiple_of(i * rows_c, rows_c), rows_c), :] = stage_ref[i % 2].astype(BF16)

    @pl.when(p > 0)
    def _():
        o_ref[...] = _nt_dot(a_ref[...], wp_ref[(p + 1) % 2])


def _in_proj(h, w_in_t, layer):
    m, k = h.shape
    n_out = w_in_t.shape[1] - N_ML_GATES
    tm, tn = MM_TM, IN_TN
    n_panels = n_out // tn
    n_chunks = IN_MAX_CHUNKS
    while n_chunks > m // tm:
        n_chunks //= 2
    assert OFF_ML_GATES % tn == 0 and n_out % tn == 0 and N_ML_GATES % 8 == 0 and n_chunks >= 1
    kern = functools.partial(_inproj_kernel, layer=layer, n_panels=n_panels, n_aligned=OFF_ML_GATES // tn,
                             shift=N_ML_GATES, n_chunks=n_chunks)
    return pl.pallas_call(
        kern,
        grid=(n_panels + 1, m // tm),
        in_specs=[pl.BlockSpec((tm, k), lambda p, i: (i, 0)),
                  pl.BlockSpec(memory_space=pl.ANY)],
        out_specs=pl.BlockSpec((tm, tn), lambda p, i: (jnp.where(p == 0, 0, i), jnp.maximum(p - 1, 0))),
        out_shape=jax.ShapeDtypeStruct((m, n_out), F32),
        scratch_shapes=[pltpu.VMEM((2, tn, k), BF16), pltpu.VMEM((2, tn // n_chunks, k), F32),
                        pltpu.SemaphoreType.DMA((2,))],
        compiler_params=_cparams("arbitrary", "arbitrary"),
        name="in_proj",
    )(h, w_in_t)


def _gates_kernel(a_ref, w_ref, b_ref, o_ref):
    g = _nt_dot(a_ref[...], w_ref[...].astype(BF16)) + b_ref[...]
    col = lax.broadcasted_iota(I32, g.shape, 1)
    is_forget = ((col // ML_HEADS) % 2) == 1
    o_ref[...] = jnp.where(is_forget, jax.nn.log_sigmoid(g), g)


def _ml_gates(h, w_in_t, layer, b_g):
    m, k = h.shape
    tm = MM_TM
    return pl.pallas_call(
        _gates_kernel,
        grid=(m // tm,),
        in_specs=[pl.BlockSpec((tm, k), lambda i: (i, 0)),
                  pl.BlockSpec((None, LANES, k), lambda i: (layer, OFF_ML_GATES // LANES, 0)),
                  pl.BlockSpec((1, LANES), lambda i: (0, 0))],
        out_specs=pl.BlockSpec((tm, LANES), lambda i: (i, 0)),
        out_shape=jax.ShapeDtypeStruct((m, LANES), F32),
        compiler_params=_cparams("parallel"),
        name="ml_gates",
    )(h, w_in_t, b_g)


def _head_rms(x, g):
    return x * lax.rsqrt(jnp.mean(x * x, axis=-1, keepdims=True) + EPS) * g


def _na_kernel(q_ref, k_ref, v_ref, kc_ref, vc_ref, qg_ref, kg_ref, bias_ref, o_ref,
               qn_ref, kn_ref, vn_ref, *, rows, kr):
    w = GRID_W
    ku = min(kr + 1, rows)
    nku = ku * w
    scale = NA_DH ** -0.5
    qn_ref[...] = _head_rms(q_ref[...], qg_ref[...]).astype(BF16)
    kn_ref[...] = _head_rms(k_ref[...], kg_ref[...]).astype(BF16)
    vn_ref[...] = v_ref[...].astype(BF16)
    kc = _head_rms(kc_ref[...], kg_ref[...]).astype(BF16)
    vc = vc_ref[...].astype(BF16)

    def body(it, carry):
        ra = 2 * it
        r0a = jnp.clip(ra - kr // 2, 0, rows - kr)
        r0b = jnp.clip(ra + 1 - kr // 2, 0, rows - kr)
        u0 = jnp.minimum(r0a, rows - ku)
        va = 2 * (r0a - ra + (kr - 1)) + (r0a - u0)
        vb = 2 * (r0b - (ra + 1) + (kr - 1)) + (r0b - u0)
        q2 = qn_ref[pl.ds(pl.multiple_of(ra * w, 2 * w), 2 * w), :]
        kw = kn_ref[pl.ds(pl.multiple_of(u0 * w, w), nku), :]
        vw = vn_ref[pl.ds(pl.multiple_of(u0 * w, w), nku), :]
        bias = jnp.concatenate([bias_ref[va], bias_ref[vb]], axis=0)
        s_lat = _nt_dot(q2, kw) * scale + bias
        s_ctx = _nt_dot(q2, kc) * scale
        m = jnp.maximum(jnp.max(s_lat, axis=-1, keepdims=True), jnp.max(s_ctx, axis=-1, keepdims=True))
        p_lat = jnp.exp(s_lat - m)
        p_ctx = jnp.exp(s_ctx - m)
        den = jnp.sum(p_lat, axis=-1, keepdims=True) + jnp.sum(p_ctx, axis=-1, keepdims=True)
        out = (jnp.dot(p_lat.astype(BF16), vw, preferred_element_type=F32)
               + jnp.dot(p_ctx.astype(BF16), vc, preferred_element_type=F32))
        o_ref[pl.ds(pl.multiple_of(ra * w, 2 * w), 2 * w), :] = (out / den).astype(o_ref.dtype)
        return carry

    lax.fori_loop(0, rows // 2, body, 0, unroll=4)


def _na_bias_table(rel_bias, kr, rows):
    w = GRID_W
    nk = kr * w
    ku = min(kr + 1, rows)
    n_dc = 2 * NA_COLS - 1
    dc = np.clip(np.arange(w)[None, :] - np.arange(w)[:, None] + NA_COLS - 1, 0, n_dc - 1)
    onehot = jnp.asarray((dc[:, :, None] == np.arange(n_dc)).astype(np.float32))
    toep = jnp.einsum("hrd,ckd->hrck", rel_bias.astype(F32), onehot, precision=lax.Precision.HIGHEST)
    r_lo = NA_ROWS - kr
    tab = jnp.stack([toep[:, r_lo + off:r_lo + off + kr] for off in range(kr)], axis=1)
    tab = jnp.transpose(tab, (0, 1, 3, 2, 4)).reshape(rel_bias.shape[0], kr, w, nk)
    c0 = np.clip(np.arange(w) - NA_COLS // 2, 0, w - NA_COLS)[:, None]
    kcol = (np.arange(nk) % w)[None, :]
    in_win = jnp.asarray((kcol >= c0) & (kcol < c0 + NA_COLS))
    tab = jnp.where(in_win, tab, NEG_INF)
    pad = ku * w - nk
    shifted = [jnp.pad(tab, ((0, 0), (0, 0), (0, 0), (d * w, pad - d * w)), constant_values=NEG_INF)
               for d in range(pad // w + 1)]
    shifted = (shifted + shifted)[:2]
    return jnp.stack(shifted, axis=2).reshape(rel_bias.shape[0], 2 * kr, w, ku * w)


def _na_latent(p, q_gain, k_gain, bias_tab, dims):
    b, l, lc = dims["B"], dims["L"], dims["Lc"]
    rows = l // GRID_W
    kr = min(NA_ROWS, rows)
    nk = min(kr + 1, rows) * GRID_W
    hd = NA_DH
    assert rows % 2 == 0
    kern = functools.partial(_na_kernel, rows=rows, kr=kr)
    ctx_blk = (b * l) // lc
    return pl.pallas_call(
        kern,
        grid=(b, NA_HEADS),
        in_specs=[pl.BlockSpec((l, hd), lambda i, h: (i, OFF_NA_Q // hd + h)),
                  pl.BlockSpec((l, hd), lambda i, h: (i, OFF_NA_K // hd + h)),
                  pl.BlockSpec((l, hd), lambda i, h: (i, OFF_NA_V // hd + h)),
                  pl.BlockSpec((lc, hd), lambda i, h: (ctx_blk + i, OFF_NA_K // hd + h)),
                  pl.BlockSpec((lc, hd), lambda i, h: (ctx_blk + i, OFF_NA_V // hd + h)),
                  pl.BlockSpec((1, hd), lambda i, h: (0, 0)),
                  pl.BlockSpec((1, hd), lambda i, h: (0, 0)),
                  pl.BlockSpec((None, 2 * kr, GRID_W, nk), lambda i, h: (h, 0, 0, 0))],
        out_specs=pl.BlockSpec((l, hd), lambda i, h: (i, h)),
        out_shape=jax.ShapeDtypeStruct((b * l, NA_W), BF16),
        scratch_shapes=[pltpu.VMEM((l, hd), BF16), pltpu.VMEM((l, hd), BF16), pltpu.VMEM((l, hd), BF16)],
        compiler_params=_cparams("parallel", "parallel"),
        name="na_latent",
    )(p, p, p, p, p, q_gain.reshape(1, hd), k_gain.reshape(1, hd), bias_tab)


def _ctx_attn_kernel(q_ref, k_ref, v_ref, qg_ref, kg_ref, o_ref):
    q = _head_rms(q_ref[...], qg_ref[...]).astype(BF16)
    k = _head_rms(k_ref[...], kg_ref[...]).astype(BF16)
    s = _nt_dot(q, k) * (NA_DH ** -0.5)
    m = jnp.max(s, axis=-1, keepdims=True)
    pr = jnp.exp(s - m)
    den = jnp.sum(pr, axis=-1, keepdims=True)
    out = jnp.dot(pr.astype(BF16), v_ref[...].astype(BF16), preferred_element_type=F32)
    o_ref[...] = (out / den).astype(o_ref.dtype)


def _na_context(p, q_gain, k_gain, dims):
    b, l, lc = dims["B"], dims["L"], dims["Lc"]
    hd = NA_DH
    ctx_blk = (b * l) // lc
    return pl.pallas_call(
        _ctx_attn_kernel,
        grid=(b, NA_HEADS),
        in_specs=[pl.BlockSpec((lc, hd), lambda i, h: (ctx_blk + i, OFF_NA_Q // hd + h)),
                  pl.BlockSpec((lc, hd), lambda i, h: (ctx_blk + i, OFF_NA_K // hd + h)),
                  pl.BlockSpec((lc, hd), lambda i, h: (ctx_blk + i, OFF_NA_V // hd + h)),
                  pl.BlockSpec((1, hd), lambda i, h: (0, 0)),
                  pl.BlockSpec((1, hd), lambda i, h: (0, 0))],
        out_specs=pl.BlockSpec((lc, hd), lambda i, h: (i, h)),
        out_shape=jax.ShapeDtypeStruct((b * lc, NA_W), BF16),
        compiler_params=_cparams("parallel", "parallel"),
        name="na_context",
    )(p, p, p, q_gain.reshape(1, hd), k_gain.reshape(1, hd))


def _rope_tables(seq, pad_rows):
    nf = ML_DK // 4
    t = np.arange(seq)
    pos = np.stack([t // GRID_W, t % GRID_W], axis=-1).astype(np.float32)
    inv_freq = (ROPE_BASE ** (-np.arange(nf, dtype=np.float32) / nf)).astype(np.float32)
    lane = np.arange(ML_DK)
    axis, pair, f = lane // (2 * nf), (lane // nf) % 2, lane % nf
    ang = jnp.asarray(pos[:, axis]) * jnp.asarray(inv_freq[f])[None, :]
    cos, sin = jnp.cos(ang), jnp.sin(ang)
    s_lo = jnp.where(jnp.asarray(pair == 0)[None, :], -sin, 0.0)
    s_hi = jnp.where(jnp.asarray(pair == 1)[None, :], sin, 0.0)
    pad = lambda a, v: jnp.concatenate([a, jnp.full((pad_rows, ML_DK), v, F32)], axis=0)
    return pad(cos, 1.0), pad(s_lo, 0.0), pad(s_hi, 0.0)


def _mlprep_kernel(x_ref, xp_ref, xn_ref, v0_ref, v1_ref, w_ref, cos_ref, slo_ref, shi_ref,
                   q_ref, kt_ref, v_ref, *, n_lat, seq, seq_ctx):
    i, j = pl.program_id(0), pl.program_id(1)
    t = x_ref.shape[0]
    nf = ML_DK // 4
    r0 = i * t
    is_lat = r0 < n_lat
    starts = jnp.where(is_lat, r0 % seq == 0, (r0 - n_lat) % seq_ctx == 0)
    ends = jnp.where(is_lat, (r0 + t) % seq == 0, (r0 + t - n_lat) % seq_ctx == 0)
    row = lax.broadcasted_iota(I32, (t, ML_DK), 0)

    def rope(h):
        cs = slice(h * ML_DK, (h + 1) * ML_DK)
        x = x_ref[:, cs]
        prev_row = jnp.where(starts, 0.0, xp_ref[7:8, cs])
        next_row = jnp.where(ends, 0.0, xn_ref[0:1, cs])
        x_prev = jnp.where(row == 0, prev_row, pltpu.roll(x, 1, 0))
        x_next = jnp.where(row == t - 1, next_row, pltpu.roll(x, t - 1, 0))
        yh = jax.nn.silu(x_prev * w_ref[0:1, cs] + x * w_ref[1:2, cs] + x_next * w_ref[2:3, cs])
        return (yh * cos_ref[...] + pltpu.roll(yh, ML_DK - nf, 1) * slo_ref[...]
                + pltpu.roll(yh, nf, 1) * shi_ref[...])

    @pl.when(j == 0)
    def _():
        half = ML_HEADS // 2
        for h in range(ML_HEADS):
            q_ref[h] = rope(h).astype(q_ref.dtype)
            hv = h % half
            v_src = v0_ref if h < half else v1_ref
            v_ref[h] = v_src[:, hv * ML_DV:(hv + 1) * ML_DV].astype(v_ref.dtype)

    @pl.when(j == 1)
    def _():
        for h in range(ML_HEADS):
            kt_ref[h] = (rope(h) * (ML_DK ** -0.5)).T.astype(kt_ref.dtype)


def _ml_prep(p, conv_w, tables, dims):
    m = p.shape[0]
    t = ROW_TILE
    n_lat, l, lc = dims["n_lat"], dims["L"], dims["Lc"]
    hd = ML_DK
    wb = ML_QK
    col0 = OFF_ML_QK // wb
    nblk8 = m // 8
    lat_blocks = l // t

    def tab_idx(i, j):
        return (jnp.where(i * t < n_lat, i % lat_blocks, lat_blocks), 0)

    kern = functools.partial(_mlprep_kernel, n_lat=n_lat, seq=l, seq_ctx=lc)
    return pl.pallas_call(
        kern,
        grid=(m // t, 2),
        in_specs=[pl.BlockSpec((t, wb), lambda i, j: (i, col0 + j)),
                  pl.BlockSpec((8, wb), lambda i, j: (jnp.maximum(i * (t // 8) - 1, 0), col0 + j)),
                  pl.BlockSpec((8, wb), lambda i, j: (jnp.minimum((i + 1) * (t // 8), nblk8 - 1), col0 + j)),
                  pl.BlockSpec((t, HALF_W), lambda i, j: (i, OFF_ML_V // HALF_W)),
                  pl.BlockSpec((t, HALF_W), lambda i, j: (i, OFF_ML_V // HALF_W + 1)),
                  pl.BlockSpec((conv_w.shape[0], wb), lambda i, j: (0, j)),
                  pl.BlockSpec((t, hd), tab_idx),
                  pl.BlockSpec((t, hd), tab_idx),
                  pl.BlockSpec((t, hd), tab_idx)],
        out_specs=[pl.BlockSpec((ML_HEADS, t, ML_DK), lambda i, j: (0, i, 0)),
                   pl.BlockSpec((ML_HEADS, ML_DK, t), lambda i, j: (0, 0, i)),
                   pl.BlockSpec((ML_HEADS, t, ML_DV), lambda i, j: (0, i, 0))],
        out_shape=[jax.ShapeDtypeStruct((ML_HEADS, m, ML_DK), BF16),
                   jax.ShapeDtypeStruct((ML_HEADS, ML_DK, m), BF16),
                   jax.ShapeDtypeStruct((ML_HEADS, m, ML_DV), BF16)],
        compiler_params=_cparams("arbitrary", "arbitrary"),
        name="ml_prep",
    )(p, p, p, p, p, conv_w, *tables)


ML_AUG = ML_DV + LANES


def _mlstm_kernel(q_ref, kt_ref, v_ref, gr_ref, o_ref, c_ref, m_ref, cum_ref):
    d, s = pl.program_id(1), pl.program_id(2)
    t = ML_CHUNK
    nh = ML_HEADS

    @pl.when(s == 0)
    def _():
        c_ref[...] = jnp.zeros_like(c_ref)
        m_ref[...] = jnp.zeros_like(m_ref)

    fwd = d == 0
    ri = lax.broadcasted_iota(I32, (t, t), 0)
    ci = lax.broadcasted_iota(I32, (t, t), 1)
    tri = (ci - ri) * jnp.where(fwd, 1, -1) <= 0
    trif = tri.astype(F32)
    cum_ref[...] = lax.dot_general(gr_ref[...], trif, (((1,), (1,)), ((), ())), preferred_element_type=F32,
                                   precision=lax.Precision.HIGHEST)
    row_i = jnp.where(fwd, 0, 2 * nh)
    row_f = row_i + nh
    ones = jnp.ones((t, LANES), BF16)

    def head(h, carry):
        lir = gr_ref[pl.ds(row_i + h, 1), :]
        lfr = gr_ref[pl.ds(row_f + h, 1), :]
        br = cum_ref[pl.ds(row_f + h, 1), :]
        bc = jnp.sum(trif * lfr, axis=-1, keepdims=True)
        blh = jnp.sum(lfr, axis=-1, keepdims=True)
        m_h = m_ref[pl.ds(h, 1), 0:1]
        a = bc + m_h
        dlog = jnp.where(tri, bc - br + lir, NEG_INF)
        mj = jnp.maximum(a, jnp.max(dlog, axis=-1, keepdims=True))
        w_inter = jnp.exp(a - mj)
        qh = q_ref[h]
        kth = kt_ref[h]
        vh = v_ref[h]
        sm = jnp.dot(qh, kth, preferred_element_type=F32) * jnp.exp(dlog - mj)
        c_h = c_ref[h]
        qc = jnp.dot(qh, c_h.astype(BF16), preferred_element_type=F32)
        num = w_inter * qc[:, :ML_DV] + jnp.dot(sm.astype(BF16), vh, preferred_element_type=F32)
        den = w_inter * qc[:, ML_DV:ML_DV + 1] + jnp.sum(sm, axis=-1, keepdims=True)
        o_ref[h] = num / jnp.maximum(jnp.abs(den), jnp.exp(-mj))
        gl_r = blh - br + lir
        m_new = jnp.maximum(blh + m_h, jnp.max(gl_r, axis=-1, keepdims=True))
        sc = jnp.exp(blh + m_h - m_new)
        ktw = (kth.astype(F32) * jnp.exp(gl_r - m_new)).astype(BF16)
        v_aug = jnp.concatenate([vh, ones], axis=1)
        c_ref[h] = sc * c_h + jnp.dot(ktw, v_aug, preferred_element_type=F32)
        m_ref[pl.ds(h, 1), :] = jnp.broadcast_to(m_new, (1, LANES))
        return carry

    lax.fori_loop(0, nh, head, 0, unroll=8)


def _mlstm_scan(q3, kt3, v3, gates_t, dims):
    m = q3.shape[1]
    t = ML_CHUNK
    b, l, lc, n_lat = dims["B"], dims["L"], dims["Lc"], dims["n_lat"]
    ncl, ncc = l // t, lc // t
    ng = gates_t.shape[0]

    def chunk(i, d, s):
        in_ctx = s < ncc
        c_ctx = jnp.where(d == 0, s, ncc - 1 - s)
        s_lat = s - ncc
        c_lat = jnp.where(d == 0, s_lat, ncl - 1 - s_lat)
        return jnp.where(in_ctx, n_lat // t + i * ncc + c_ctx, i * ncl + c_lat)

    return pl.pallas_call(
        _mlstm_kernel,
        grid=(b, 2, ncc + ncl),
        in_specs=[pl.BlockSpec((ML_HEADS, t, ML_DK), lambda i, d, s: (0, chunk(i, d, s), 0)),
                  pl.BlockSpec((ML_HEADS, ML_DK, t), lambda i, d, s: (0, 0, chunk(i, d, s))),
                  pl.BlockSpec((ML_HEADS, t, ML_DV), lambda i, d, s: (0, chunk(i, d, s), 0)),
                  pl.BlockSpec((ng, t), lambda i, d, s: (0, chunk(i, d, s)))],
        out_specs=pl.BlockSpec((None, ML_HEADS, t, ML_DV), lambda i, d, s: (d, 0, chunk(i, d, s), 0)),
        out_shape=jax.ShapeDtypeStruct((2, ML_HEADS, m, ML_DV), F32),
        scratch_shapes=[pltpu.VMEM((ML_HEADS, ML_DK, ML_AUG), F32),
                        pltpu.VMEM((ML_HEADS, LANES), F32),
                        pltpu.VMEM((ng, t), F32)],
        compiler_params=_cparams("parallel", "parallel", "arbitrary"),
        name="mlstm_scan",
    )(q3, kt3, v3, gates_t)


def _mlout_kernel(hf_ref, hb_ref, og_ref, g_ref, y_ref):
    for h in range(HALF_W // ML_DV):
        sl = slice(h * ML_DV, (h + 1) * ML_DV)
        x = hf_ref[h] + hb_ref[h]
        y = x * lax.rsqrt(jnp.mean(x * x, axis=-1, keepdims=True) + EPS) * g_ref[:, sl]
        y_ref[:, sl] = (y * jax.nn.sigmoid(og_ref[:, sl])).astype(y_ref.dtype)


def _ml_out(hs, p, gain, rows):
    tm = ROW_TILE
    wb = HALF_W
    hb = wb // ML_DV
    return pl.pallas_call(
        _mlout_kernel,
        grid=(rows // tm, ML_V // wb),
        in_specs=[pl.BlockSpec((None, hb, tm, ML_DV), lambda i, j: (0, j, i, 0)),
                  pl.BlockSpec((None, hb, tm, ML_DV), lambda i, j: (1, j, i, 0)),
                  pl.BlockSpec((tm, wb), lambda i, j: (i, OFF_ML_O // wb + j)),
                  pl.BlockSpec((1, wb), lambda i, j: (0, j))],
        out_specs=pl.BlockSpec((tm, wb), lambda i, j: (i, j)),
        out_shape=jax.ShapeDtypeStruct((rows, ML_V), BF16),
        compiler_params=_cparams("parallel", "parallel"),
        name="ml_out",
    )(hs, hs, p, gain.reshape(1, ML_V))


def _gelu_exact(x):
    return 0.5 * x * (1.0 + lax.erf(x * np.float32(np.sqrt(0.5))))


def _gmlp_kernel(u_ref, v_ref, g_ref, ws_ref, bt_ref, o_ref):
    u = _gelu_exact(u_ref[...])
    v = _gelu_exact(v_ref[...])
    vn = (v * lax.rsqrt(jnp.mean(v * v, axis=-1, keepdims=True) + EPS) * g_ref[...]).astype(BF16)
    bt = bt_ref[...]
    for c in range(u.shape[0] // GM_CHUNK):
        rs = slice(c * GM_CHUNK, (c + 1) * GM_CHUNK)
        for g in range(GM_GROUPS):
            cs = slice(g * GM_CPG, (g + 1) * GM_CPG)
            mixed = jnp.dot(ws_ref[g].astype(BF16), vn[rs, cs], preferred_element_type=F32) + bt[:, g:g + 1]
            o_ref[rs, cs] = (u[rs, cs] * mixed).astype(o_ref.dtype)


def _gmlp(p, v_gain, ws, bias, rows):
    tm = ROW_TILE
    return pl.pallas_call(
        _gmlp_kernel,
        grid=(rows // tm,),
        in_specs=[pl.BlockSpec((tm, GM_W), lambda i: (i, OFF_GM_U // GM_W)),
                  pl.BlockSpec((tm, GM_W), lambda i: (i, OFF_GM_V // GM_W)),
                  pl.BlockSpec((1, GM_W), lambda i: (0, 0)),
                  pl.BlockSpec((GM_GROUPS, GM_CHUNK, GM_CHUNK), lambda i: (0, 0, 0)),
                  pl.BlockSpec((GM_CHUNK, GM_GROUPS), lambda i: (0, 0))],
        out_specs=pl.BlockSpec((tm, GM_W), lambda i: (i, 0)),
        out_shape=jax.ShapeDtypeStruct((rows, GM_W), BF16),
        compiler_params=_cparams("parallel"),
        name="gmlp",
    )(p, p, v_gain.reshape(1, GM_W), ws, bias.T)


def _branch_kernel(ya_ref, ym_ref, yg_ref, w_ref, g0_ref, g1_ref, g2_ref, o_ref, wp_ref):
    @pl.when(pl.program_id(1) == 0)
    def _():
        wp_ref[...] = w_ref[...].astype(BF16)

    dot = functools.partial(jnp.dot, preferred_element_type=F32)
    za = dot(ya_ref[...], wp_ref[0:NA_W, :])
    zm = dot(ym_ref[...], wp_ref[NA_W:NA_W + ML_V, :])
    zg = dot(yg_ref[...], wp_ref[NA_W + ML_V:, :])
    z = (jax.nn.sigmoid(g0_ref[...]) * za + jax.nn.sigmoid(g1_ref[...]) * zm
         + jax.nn.sigmoid(g2_ref[...]) * zg)
    o_ref[...] = z.astype(o_ref.dtype)


def _branch(ya, ym, yg, w_branch, layer, p, d_model, rows):
    tm, tn = MM_TM, 512
    k = w_branch.shape[1]
    gate_blk = lambda b: (OFF_GATE + b * d_model) // tn
    return pl.pallas_call(
        _branch_kernel,
        grid=(d_model // tn, rows // tm),
        in_specs=[pl.BlockSpec((tm, NA_W), lambda j, i: (i, 0)),
                  pl.BlockSpec((tm, ML_V), lambda j, i: (i, 0)),
                  pl.BlockSpec((tm, GM_W), lambda j, i: (i, 0)),
                  pl.BlockSpec((None, k, tn), lambda j, i: (layer, 0, j)),
                  pl.BlockSpec((tm, tn), lambda j, i: (i, gate_blk(0) + j)),
                  pl.BlockSpec((tm, tn), lambda j, i: (i, gate_blk(1) + j)),
                  pl.BlockSpec((tm, tn), lambda j, i: (i, gate_blk(2) + j))],
        out_specs=pl.BlockSpec((tm, tn), lambda j, i: (i, j)),
        out_shape=jax.ShapeDtypeStruct((rows, d_model), BF16),
        scratch_shapes=[pltpu.VMEM((k, tn), BF16)],
        compiler_params=_cparams("parallel", "arbitrary"),
        name="branch_merge",
    )(ya, ym, yg, w_branch, p, p, p)


def _outproj_kernel(z_ref, w_ref, x_ref, gt_ref, o_ref, wp_ref):
    @pl.when(pl.program_id(1) == 0)
    def _():
        wp_ref[...] = w_ref[...].astype(BF16)

    y = jnp.dot(z_ref[...], wp_ref[...], preferred_element_type=F32)
    o_ref[...] = x_ref[...] + gt_ref[...] * y


def _outproj(z, w_out, layer, x, mod3, k_gate, dims, rows):
    k = z.shape[1]
    d = w_out.shape[2]
    tm, tn = MM_TM, 512
    mrow = functools.partial(_mod_row, tm=tm, n_lat=dims["n_lat"], seq=dims["L"], n_batch=dims["B"])
    gblk = k_gate * (d // tn)
    return pl.pallas_call(
        _outproj_kernel,
        grid=(d // tn, rows // tm),
        in_specs=[pl.BlockSpec((tm, k), lambda j, i: (i, 0)),
                  pl.BlockSpec((None, k, tn), lambda j, i: (layer, 0, j)),
                  pl.BlockSpec((tm, tn), lambda j, i: (i, j)),
                  pl.BlockSpec((None, 1, tn), lambda j, i: (mrow(i), 0, gblk + j))],
        out_specs=pl.BlockSpec((tm, tn), lambda j, i: (i, j)),
        out_shape=jax.ShapeDtypeStruct((rows, d), F32),
        scratch_shapes=[pltpu.VMEM((k, tn), BF16)],
        compiler_params=_cparams("parallel", "arbitrary"),
        name="out_proj",
    )(z, w_out, x, mod3)


def _affinity_kernel(h_ref, w_ref, o_ref, *, n_exp):
    logits = jnp.dot(_unpack_bf16_pairs(h_ref[...]), w_ref[...], preferred_element_type=F32)
    col = lax.broadcasted_iota(I32, logits.shape, 1)
    logits = jnp.where(col < n_exp, logits, NEG_INF)
    o_ref[...] = jax.nn.softmax(logits, axis=-1)


def _affinity(h2, w_router, rows):
    d, e = w_router.shape
    tm = MM_TM
    w_pad = jnp.pad(w_router, ((0, 0), (0, LANES - e))).astype(BF16)
    return pl.pallas_call(
        functools.partial(_affinity_kernel, n_exp=e),
        grid=(rows // tm,),
        in_specs=[pl.BlockSpec((tm, d // 2), lambda i: (i, 0)),
                  pl.BlockSpec((d, LANES), lambda i: (0, 0))],
        out_specs=pl.BlockSpec((tm, LANES), lambda i: (i, 0)),
        out_shape=jax.ShapeDtypeStruct((rows, LANES), F32),
        compiler_params=_cparams("parallel"),
        name="router_affinity",
    )(h2, w_pad)


def _cumsum_rows(x01, blk):
    n = x01.shape[0]
    ri = lax.broadcasted_iota(I32, (blk, blk), 0)
    ci = lax.broadcasted_iota(I32, (blk, blk), 1)
    tril = (ci <= ri).astype(BF16)
    parts = []
    carry = jnp.zeros((1, x01.shape[1]), F32)
    for j in range(n // blk):
        cs = jnp.dot(tril, x01[j * blk:(j + 1) * blk, :].astype(BF16), preferred_element_type=F32) + carry
        parts.append(cs)
        carry = cs[blk - 1:blk, :]
    return jnp.concatenate(parts, axis=0) if len(parts) > 1 else parts[0]


def _select_kernel(aff_ref, idx_ref, gate_ref, rank_ref, *, cap, tb, n_exp):
    aff = aff_ref[...]
    n = aff.shape[0]
    key = lax.bitcast_convert_type(aff, I32)
    capf = jnp.float32(cap)

    def search(i, thr):
        cand = thr | jnp.left_shift(jnp.int32(1), 30 - i)
        cnt = jnp.sum((key >= cand).astype(F32), axis=0, keepdims=True)
        return jnp.where(cnt >= capf, cand, thr)

    thr = lax.fori_loop(0, 31, search, jnp.zeros((1, LANES), I32))
    above = key > thr
    tied = key == thr
    need = capf - jnp.sum(above.astype(F32), axis=0, keepdims=True)
    tied_f = tied.astype(F32)
    tie_rank = _cumsum_rows(tied_f, tb) - tied_f
    sel = above | (tied & (tie_rank < need))
    incl = _cumsum_rows(sel.astype(F32), tb)
    rank_ref[0] = incl
    rank_ref[1] = jnp.where(sel, incl, -1.0)
    rank_ref[2] = aff
    slot = lax.broadcasted_iota(I32, (tb, cap), 1).astype(F32)

    for ex in range(n_exp):
        def block(j, acc):
            acc_i, acc_g = acc
            rows = pl.ds(pl.multiple_of(j * tb, tb), tb)
            inc = rank_ref[0, rows, ex:ex + 1]
            inc_sel = rank_ref[1, rows, ex:ex + 1]
            a = rank_ref[2, rows, ex:ex + 1]
            acc_i = acc_i + jnp.sum(jnp.where(inc <= slot, 1.0, 0.0), axis=0, keepdims=True)
            acc_g = acc_g + jnp.sum(jnp.where(inc_sel == slot + 1.0, a, 0.0), axis=0, keepdims=True)
            return acc_i, acc_g

        zero = jnp.zeros((1, cap), F32)
        acc_i, acc_g = lax.fori_loop(0, n // tb, block, (zero, zero))
        idx_ref[ex:ex + 1, :] = acc_i.astype(I32)
        gate_ref[ex:ex + 1, :] = acc_g


def _select(aff, e, n_groups, group_len, blk0, cap):
    tb = min(256, group_len)
    kern = functools.partial(_select_kernel, cap=cap, tb=tb, n_exp=e)
    return pl.pallas_call(
        kern,
        grid=(n_groups,),
        in_specs=[pl.BlockSpec((group_len, LANES), lambda g: (blk0 + g, 0))],
        out_specs=[pl.BlockSpec((None, e, cap), lambda g: (g, 0, 0)),
                   pl.BlockSpec((None, e, cap), lambda g: (g, 0, 0))],
        out_shape=[jax.ShapeDtypeStruct((n_groups, e, cap), I32),
                   jax.ShapeDtypeStruct((n_groups, e, cap), F32)],
        scratch_shapes=[pltpu.VMEM((3, group_len, LANES), F32)],
        compiler_params=_cparams("parallel"),
        name="router_select",
    )(aff)


def _moe_up_kernel(idx_ref, h_hbm, wg_ref, wu_ref, o_ref, xs_ref, xb_ref, sem, *, n_rows, n_exp, n_f):
    e, f = pl.program_id(0), pl.program_id(1)
    per_step = n_rows // n_f

    def row_copy(ex, i):
        slot = ex % 2
        src = idx_ref[ex * n_rows + i]
        return pltpu.make_async_copy(h_hbm.at[pl.ds(src, 1), :], xs_ref.at[slot, pl.ds(i, 1), :], sem.at[slot])

    def start_rows(ex, lo, n):
        def start(i, c):
            row_copy(ex, lo + i).start()
            return c
        lax.fori_loop(0, n, start, 0, unroll=4)

    @pl.when(jnp.logical_and(e == 0, f == 0))
    def _():
        start_rows(0, 0, n_rows)

    @pl.when(f == 0)
    def _():
        def wait(i, c):
            row_copy(e, i).wait()
            return c
        lax.fori_loop(0, n_rows, wait, 0, unroll=4)
        xb_ref[...] = _unpack_bf16_pairs(xs_ref[e % 2])

    @pl.when(e + 1 < n_exp)
    def _():
        start_rows(e + 1, f * per_step, per_step)

    xb = xb_ref[...]
    a = jnp.dot(xb, wg_ref[...].astype(BF16), preferred_element_type=F32)
    u = jnp.dot(xb, wu_ref[...].astype(BF16), preferred_element_type=F32)
    o_ref[...] = (jax.nn.silu(a) * u).astype(o_ref.dtype)


def _moe_up(idx_flat, h2_packed, w_gate, w_up, layer, n_rows):
    _, e, d, ff = w_gate.shape
    tf = 256
    n_f = ff // tf
    assert n_rows % n_f == 0
    kern = functools.partial(_moe_up_kernel, n_rows=n_rows, n_exp=e, n_f=n_f)
    return pl.pallas_call(
        kern,
        grid_spec=pltpu.PrefetchScalarGridSpec(
            num_scalar_prefetch=1,
            grid=(e, n_f),
            in_specs=[pl.BlockSpec(memory_space=pl.ANY),
                      pl.BlockSpec((None, None, d, tf), lambda i, f, idx: (layer, i, 0, f)),
                      pl.BlockSpec((None, None, d, tf), lambda i, f, idx: (layer, i, 0, f))],
            out_specs=pl.BlockSpec((None, n_rows, tf), lambda i, f, idx: (i, 0, f)),
            scratch_shapes=[pltpu.VMEM((2, n_rows, d // 2), jnp.uint32), pltpu.VMEM((n_rows, d), BF16),
                            pltpu.SemaphoreType.DMA((2,))]),
        out_shape=jax.ShapeDtypeStruct((e, n_rows, ff), BF16),
        compiler_params=_cparams("arbitrary", "arbitrary"),
        name="moe_up",
    )(idx_flat, h2_packed, w_gate, w_up)


def _moe_down_kernel(idx_ref, hid_ref, gcol_ref, wd_ref, gt_ref, x_hbm, o_hbm, acc_ref, y_ref, sem, *,
                     cap, n_exp, group_len, row0, dc):
    g, j, e = pl.program_id(0), pl.program_id(1), pl.program_id(2)
    nlt = dc // LANES
    sub = 8
    rows = pl.ds(pl.multiple_of((row0 + g * group_len) // sub, group_len // sub), group_len // sub)

    def tile_copy(c, to_vmem):
        hbm = (x_hbm if to_vmem else o_hbm).at[rows, :, pl.ds(pl.multiple_of(j * dc + c * LANES, LANES), LANES)]
        vmem = acc_ref.at[:, c]
        return pltpu.make_async_copy(hbm, vmem, sem.at[0]) if to_vmem else pltpu.make_async_copy(vmem, hbm, sem.at[1])

    @pl.when(e == 0)
    def _():
        for c in range(nlt):
            tile_copy(c, True).start()

    chunk = min(cap, 128)
    w = wd_ref[...].astype(BF16)
    gt = gt_ref[...]
    for rc in range(cap // chunk):
        rs = slice(rc * chunk, (rc + 1) * chunk)
        y = jnp.dot(hid_ref[rs, :], w, preferred_element_type=F32) * gcol_ref[rs, :] * gt
        for rr in range(chunk // sub):
            for c in range(nlt):
                y_ref[rc * chunk // sub + rr, c] = y[rr * sub:(rr + 1) * sub, c * LANES:(c + 1) * LANES]

    @pl.when(e == 0)
    def _():
        for c in range(nlt):
            tile_copy(c, True).wait()

    base = (g * n_exp + e) * cap
    acc_flat = acc_ref.reshape(group_len * nlt, LANES)
    y_flat = y_ref.reshape(cap * nlt, LANES)

    def add_rows(bi, carry):
        i0 = bi * SCATTER_BATCH
        starts = [idx_ref[base + i0 + r] for r in range(SCATTER_BATCH)]
        acc = [acc_flat[pl.ds(s, nlt, stride=sub), :] for s in starts]
        for r in range(SCATTER_BATCH):
            yr = y_flat[pl.ds(i0 * nlt + r, nlt, stride=sub), :]
            acc_flat[pl.ds(starts[r], nlt, stride=sub), :] = acc[r] + yr
        return carry

    lax.fori_loop(0, cap // SCATTER_BATCH, add_rows, 0)

    @pl.when(e == n_exp - 1)
    def _():
        for c in range(nlt):
            tile_copy(c, False).start()
        for c in range(nlt):
            tile_copy(c, False).wait()


def _moe_down(idx_flat, hid, gate_col, w_down, layer, x, mod3, k_gate, n_groups, group_len, cap,
              hid_row0, x_row0, mod_row_of_group, out_rows):
    _, e, ff, d = w_down.shape
    assert SCATTER_BATCH == 8 and cap % 8 == 0 and x_row0 % group_len == 0 and out_rows % 8 == 0
    dc = 1024
    nlt = dc // LANES
    kern = functools.partial(_moe_down_kernel, cap=cap, n_exp=e, group_len=group_len, row0=x_row0, dc=dc)
    hblk0 = hid_row0 // cap
    gblk = k_gate * (d // dc)
    out = pl.pallas_call(
        kern,
        grid_spec=pltpu.PrefetchScalarGridSpec(
            num_scalar_prefetch=1,
            grid=(n_groups, d // dc, e),
            in_specs=[pl.BlockSpec((None, cap, ff), lambda g, j, i, idx: (i, hblk0 + g, 0)),
                      pl.BlockSpec((None, cap, 1), lambda g, j, i, idx: (i, hblk0 + g, 0)),
                      pl.BlockSpec((None, None, ff, dc), lambda g, j, i, idx: (layer, i, 0, j)),
                      pl.BlockSpec((None, 1, dc), lambda g, j, i, idx: (mod_row_of_group(g), 0, gblk + j)),
                      pl.BlockSpec(memory_space=pl.ANY)],
            out_specs=pl.BlockSpec(memory_space=pl.ANY),
            scratch_shapes=[pltpu.VMEM((group_len // 8, nlt, 8, LANES), F32),
                            pltpu.VMEM((cap // 8, nlt, 8, LANES), F32),
                            pltpu.SemaphoreType.DMA((2,))]),
        out_shape=jax.ShapeDtypeStruct((out_rows // 8, 8, d), F32),
        input_output_aliases={5: 0},
        compiler_params=_cparams("arbitrary", "arbitrary", "arbitrary"),
        name="moe_down",
    )((idx_flat >> 3) * (8 * nlt) + (idx_flat & 7), hid, gate_col, w_down, mod3, x.reshape(out_rows // 8, 8, d))
    return out.reshape(out_rows, d)


def _layer(x, mod_l, layer, last, dims, w_in, na_q_gain, na_k_gain, na_rel_bias, ml_conv, ml_gate_bias,
           ml_out_gain, gm_v_gain, gm_ws, gm_bias, w_branch, w_out, g_norm1, g_norm2, w_router, w_e_gate, w_e_up,
           w_e_down, rope_tabs):
    b, l, lc, n_lat = dims["B"], dims["L"], dims["Lc"], dims["n_lat"]
    m, d = x.shape
    rows = n_lat if last else m
    mod3 = mod_l.reshape(mod_l.shape[0], 1, mod_l.shape[1])
    n_exp = w_router.shape[1]

    h = _prenorm(x, g_norm1, mod3, 0, 1, dims, m, False)
    p = _in_proj(h, w_in, layer)
    b_g = jnp.pad(ml_gate_bias.reshape(1, N_ML_GATES), ((0, 0), (0, LANES - N_ML_GATES)))
    gates = _ml_gates(h, w_in, layer, b_g)
    gates_t = gates[:, :N_ML_GATES].T

    kr = min(NA_ROWS, l // GRID_W)
    ya = _na_latent(p, na_q_gain, na_k_gain, _na_bias_table(na_rel_bias, kr, l // GRID_W), dims)
    if not last:
        ya = jnp.concatenate([ya, _na_context(p, na_q_gain, na_k_gain, dims)], axis=0)

    q3, kt3, v3 = _ml_prep(p, ml_conv, rope_tabs, dims)
    hs = _mlstm_scan(q3, kt3, v3, gates_t, dims)
    ym = _ml_out(hs, p, ml_out_gain, rows)
    yg = _gmlp(p, gm_v_gain, gm_ws, gm_bias, rows)

    z = _branch(ya, ym, yg, w_branch, layer, p, d, rows)
    x1 = _outproj(z, w_out, layer, x, mod3, 2, dims, rows)

    h2 = _prenorm(x1, g_norm2, mod3, 3, 4, dims, rows, True)
    aff = _affinity(h2, w_router, rows)
    cap_l = max(1, min(l, (CAPACITY_FACTOR * l) // n_exp))
    idx_l, gate_l = _select(aff, n_exp, b, l, 0, cap_l)
    row_l = idx_l + (jnp.arange(b, dtype=I32) * l)[:, None, None]
    src = [jnp.swapaxes(row_l, 0, 1).reshape(n_exp, b * cap_l)]
    gsel = [jnp.swapaxes(gate_l, 0, 1).reshape(n_exp, b * cap_l)]
    if not last:
        cap_c = max(1, min(lc, (CAPACITY_FACTOR * lc) // n_exp))
        idx_c, gate_c = _select(aff, n_exp, b, lc, n_lat // lc, cap_c)
        row_c = idx_c + (n_lat + jnp.arange(b, dtype=I32) * lc)[:, None, None]
        src.append(jnp.swapaxes(row_c, 0, 1).reshape(n_exp, b * cap_c))
        gsel.append(jnp.swapaxes(gate_c, 0, 1).reshape(n_exp, b * cap_c))
    src = jnp.concatenate(src, axis=1)
    n_rows = src.shape[1]
    gate_col = jnp.concatenate(gsel, axis=1).reshape(n_exp, n_rows, 1)
    hid = _moe_up(src.reshape(-1), h2, w_e_gate, w_e_up, layer, n_rows)
    x2 = _moe_down(idx_l.reshape(-1), hid, gate_col, w_e_down, layer, x1, mod3, 5, b, l, cap_l,
                   0, 0, lambda g: g, rows)
    if not last:
        idx_cc = jnp.swapaxes(idx_c + (jnp.arange(b, dtype=I32) * lc)[:, None, None], 0, 1)
        x2 = _moe_down(idx_cc.reshape(-1), hid, gate_col, w_e_down, layer, x2, mod3, 5, 1, b * lc, b * cap_c,
                       b * cap_l, n_lat, lambda g: b, rows)
    return x2


def kernel(x, c, ctx, c_ctx, w_ada, b_ada, g_norm1, g_norm2, w_in, na_q_gain, na_k_gain, na_rel_bias, ml_conv,
           ml_gate_bias, ml_out_gain, gm_v_gain, gm_ws, gm_bias, w_branch, w_out, w_router, w_e_gate, w_e_up,
           w_e_down):
    b, l, d = x.shape
    lc = ctx.shape[1]
    depth = w_ada.shape[0]
    dims = {"B": b, "L": l, "Lc": lc, "n_lat": b * l}
    assert b + 1 <= 8 and l % ROW_TILE == 0 and lc % ROW_TILE == 0 and (b * l) % MM_TM == 0 and (b * lc) % MM_TM == 0

    c8 = jnp.concatenate([c, c_ctx[None, :], jnp.zeros((8 - b - 1, d), F32)], axis=0)
    mod = _ada(c8, w_ada, b_ada)
    xs = jnp.concatenate([x.reshape(b * l, d), ctx.reshape(b * lc, d)], axis=0)
    rope_tabs = _rope_tables(l, ROW_TILE)
    w_in_t = jnp.swapaxes(w_in, 1, 2)
    for i in range(depth):
        xs = _layer(xs, mod[i], i, i == depth - 1, dims, w_in_t, na_q_gain[i], na_k_gain[i], na_rel_bias[i],
                    ml_conv[i], ml_gate_bias[i], ml_out_gain[i], gm_v_gain[i], gm_ws[i], gm_bias[i], w_branch,
                    w_out, g_norm1[i], g_norm2[i], w_router[i], w_e_gate, w_e_up, w_e_down, rope_tabs)
    return xs[:b * l].reshape(b, l, d)
```

```python
import functools

import numpy as np
import jax
import jax.numpy as jnp
from jax import lax
from jax.experimental import pallas as pl
from jax.experimental.pallas import tpu as pltpu

F32 = jnp.float32
BF16 = jnp.bfloat16
I32 = jnp.int32

GRID_W = 64
NA_HEADS, NA_DH, NA_ROWS, NA_COLS = 8, 128, 8, 16
NA_W = NA_HEADS * NA_DH
ML_HEADS, ML_DK, ML_DV, ML_CHUNK = 8, 128, 256, 128
ML_QK, ML_V = ML_HEADS * ML_DK, ML_HEADS * ML_DV
GM_GROUPS, GM_CPG, GM_CHUNK = 8, 128, 128
GM_W = GM_GROUPS * GM_CPG
N_BRANCH = 3
CAPACITY_FACTOR = 2
ROPE_BASE = 10000.0
EPS = 1e-6
NEG_INF = -1e30

VMEM_LIMIT_BYTES = 56 * 1024 * 1024
LANES = 128
ROW_TILE = 256
MM_TM = 512
SCATTER_BATCH = 8

OFF_NA_Q, OFF_NA_K, OFF_NA_V = 0, NA_W, 2 * NA_W
OFF_ML_QK = 3 * NA_W
OFF_ML_V = OFF_ML_QK + 2 * ML_QK
OFF_ML_O = OFF_ML_V + ML_V
OFF_ML_GATES = OFF_ML_O + ML_V
N_ML_GATES = 4 * ML_HEADS
OFF_GM_U = OFF_ML_GATES
OFF_GM_V = OFF_GM_U + GM_W
OFF_GATE = OFF_GM_V + GM_W
HALF_W = 1024


def _wide_row_tile(rows):
    tm = rows // 8
    return tm if (rows % 8 == 0 and tm % 16 == 0 and tm >= MM_TM) else MM_TM


def _cparams(*sem):
    return pltpu.CompilerParams(dimension_semantics=sem, vmem_limit_bytes=VMEM_LIMIT_BYTES)


def _nt_dot(a, b):
    return lax.dot_general(a, b, (((1,), (1,)), ((), ())), preferred_element_type=F32)


def _ada_kernel(c_ref, w_ref, b_ref, o_ref):
    a = jax.nn.silu(c_ref[...]).astype(BF16)
    o_ref[...] = jnp.dot(a, w_ref[...].astype(BF16), preferred_element_type=F32) + b_ref[...]


def _ada(c8, w_ada, b_ada):
    depth, d, n = w_ada.shape
    tn = 512
    return pl.pallas_call(
        _ada_kernel,
        grid=(depth, n // tn),
        in_specs=[pl.BlockSpec((8, d), lambda l, j: (0, 0)),
                  pl.BlockSpec((None, d, tn), lambda l, j: (l, 0, j)),
                  pl.BlockSpec((None, 1, tn), lambda l, j: (l, 0, j))],
        out_specs=pl.BlockSpec((None, 8, tn), lambda l, j: (l, 0, j)),
        out_shape=jax.ShapeDtypeStruct((depth, 8, n), F32),
        compiler_params=_cparams("parallel", "parallel"),
        name="ada_mod",
    )(c8, w_ada, b_ada.reshape(depth, 1, n))


def _mod_row(i, tm, n_lat, seq, n_batch):
    r = i * tm
    return jnp.where(r >= n_lat, n_batch, r // seq)


HI16 = np.uint32(0xFFFF0000)


def _pack_bf16_pairs(y):
    half = y.shape[1] // 2
    bits = lax.bitcast_convert_type(y.astype(BF16).astype(F32), jnp.uint32)
    return (bits[:, half:] & HI16) | (bits[:, :half] >> 16)


def _unpack_bf16_pairs(packed):
    lo = lax.bitcast_convert_type(packed << 16, F32).astype(BF16)
    hi = lax.bitcast_convert_type(packed & HI16, F32).astype(BF16)
    return jnp.concatenate([lo, hi], axis=1)


def _prenorm_kernel(x_ref, g_ref, sh_ref, sc_ref, o_ref, *, packed):
    x = x_ref[...]
    y = x * lax.rsqrt(jnp.mean(x * x, axis=-1, keepdims=True) + EPS)
    y = y * g_ref[...]
    y = y * (1 + sc_ref[...]) + sh_ref[...]
    o_ref[...] = _pack_bf16_pairs(y) if packed else y.astype(o_ref.dtype)


def _prenorm(x, g, mod3, k_shift, k_scale, dims, rows, packed):
    d = x.shape[1]
    tm = ROW_TILE
    d_out = d // 2 if packed else d
    mrow = functools.partial(_mod_row, tm=tm, n_lat=dims["n_lat"], seq=dims["L"], n_batch=dims["B"])
    return pl.pallas_call(
        functools.partial(_prenorm_kernel, packed=packed),
        grid=(rows // tm,),
        in_specs=[pl.BlockSpec((tm, d), lambda i: (i, 0)),
                  pl.BlockSpec((1, d), lambda i: (0, 0)),
                  pl.BlockSpec((None, 1, d), lambda i: (mrow(i), 0, k_shift)),
                  pl.BlockSpec((None, 1, d), lambda i: (mrow(i), 0, k_scale))],
        out_specs=pl.BlockSpec((tm, d_out), lambda i: (i, 0)),
        out_shape=jax.ShapeDtypeStruct((rows, d_out), jnp.uint32 if packed else BF16),
        compiler_params=_cparams("parallel"),
        name="prenorm",
    )(x, g.reshape(1, d), mod3, mod3)


IN_TN = 1024
IN_MAX_CHUNKS = 16


def _inproj_kernel(a_ref, wt_hbm, o_ref, wp_ref, stage_ref, sem, *, layer, n_panels, n_aligned, shift, n_chunks):
    p, i = pl.program_id(0), pl.program_id(1)
    tn = o_ref.shape[1]
    rows_c = tn // n_chunks
    building = jnp.logical_and(p < n_panels, i < n_chunks)

    def chunk_copy(c):
        r0 = p * tn + jnp.where(p >= n_aligned, shift, 0) + c * rows_c
        src = wt_hbm.at[layer, pl.ds(pl.multiple_of(r0, 8), rows_c), :]
        return pltpu.make_async_copy(src, stage_ref.at[c % 2], sem.at[c % 2])

    @pl.when(jnp.logical_and(building, i == 0))
    def _():
        chunk_copy(0).start()

    @pl.when(building)
    def _():
        chunk_copy(i).wait()

        @pl.when(i + 1 < n_chunks)
        def _():
            chunk_copy(i + 1).start()

        wp_ref[p % 2, pl.ds(pl.multiple_of(i * rows_c, rows_c), rows_c), :] = stage_ref[i % 2].astype(BF16)

    @pl.when(p > 0)
    def _():
        o_ref[...] = _nt_dot(a_ref[...], wp_ref[(p + 1) % 2])


def _in_proj(h, w_in_t, layer):
    m, k = h.shape
    n_out = w_in_t.shape[1] - N_ML_GATES
    tn = IN_TN
    tm = _wide_row_tile(m)
    n_panels = n_out // tn
    n_chunks = IN_MAX_CHUNKS
    while n_chunks > m // tm:
        n_chunks //= 2
    assert OFF_ML_GATES % tn == 0 and n_out % tn == 0 and N_ML_GATES % 8 == 0 and n_chunks >= 1
    kern = functools.partial(_inproj_kernel, layer=layer, n_panels=n_panels, n_aligned=OFF_ML_GATES // tn,
                             shift=N_ML_GATES, n_chunks=n_chunks)
    return pl.pallas_call(
        kern,
        grid=(n_panels + 1, m // tm),
        in_specs=[pl.BlockSpec((tm, k), lambda p, i: (i, 0)),
                  pl.BlockSpec(memory_space=pl.ANY)],
        out_specs=pl.BlockSpec((tm, tn), lambda p, i: (jnp.where(p == 0, 0, i), jnp.maximum(p - 1, 0))),
        out_shape=jax.ShapeDtypeStruct((m, n_out), F32),
        scratch_shapes=[pltpu.VMEM((2, tn, k), BF16), pltpu.VMEM((2, tn // n_chunks, k), F32),
                        pltpu.SemaphoreType.DMA((2,))],
        compiler_params=_cparams("arbitrary", "arbitrary"),
        name="in_proj",
    )(h, w_in_t)


def _gates_kernel(a_ref, w_ref, b_ref, o_ref):
    g = _nt_dot(a_ref[...], w_ref[...].astype(BF16)) + b_ref[...]
    col = lax.broadcasted_iota(I32, g.shape, 1)
    is_forget = ((col // ML_HEADS) % 2) == 1
    o_ref[...] = jnp.where(is_forget, jax.nn.log_sigmoid(g), g)


def _ml_gates(h, w_in_t, layer, b_g):
    m, k = h.shape
    tm = MM_TM
    return pl.pallas_call(
        _gates_kernel,
        grid=(m // tm,),
        in_specs=[pl.BlockSpec((tm, k), lambda i: (i, 0)),
                  pl.BlockSpec((None, LANES, k), lambda i: (layer, OFF_ML_GATES // LANES, 0)),
                  pl.BlockSpec((1, LANES), lambda i: (0, 0))],
        out_specs=pl.BlockSpec((tm, LANES), lambda i: (i, 0)),
        out_shape=jax.ShapeDtypeStruct((m, LANES), F32),
        compiler_params=_cparams("parallel"),
        name="ml_gates",
    )(h, w_in_t, b_g)


def _head_rms(x, g):
    return x * lax.rsqrt(jnp.mean(x * x, axis=-1, keepdims=True) + EPS) * g


def _na_kernel(q_ref, k_ref, v_ref, kc_ref, vc_ref, qg_ref, kg_ref, bias_ref, o_ref,
               qn_ref, kn_ref, vn_ref, *, rows, kr):
    w = GRID_W
    ku = min(kr + 1, rows)
    nku = ku * w
    scale = NA_DH ** -0.5
    qn_ref[...] = _head_rms(q_ref[...], qg_ref[...]).astype(BF16)
    kn_ref[...] = _head_rms(k_ref[...], kg_ref[...]).astype(BF16)
    vn_ref[...] = v_ref[...].astype(BF16)
    kc = _head_rms(kc_ref[...], kg_ref[...]).astype(BF16)
    vc = vc_ref[...].astype(BF16)

    def body(it, carry):
        ra = 2 * it
        r0a = jnp.clip(ra - kr // 2, 0, rows - kr)
        r0b = jnp.clip(ra + 1 - kr // 2, 0, rows - kr)
        u0 = jnp.minimum(r0a, rows - ku)
        va = 2 * (r0a - ra + (kr - 1)) + (r0a - u0)
        vb = 2 * (r0b - (ra + 1) + (kr - 1)) + (r0b - u0)
        q2 = qn_ref[pl.ds(pl.multiple_of(ra * w, 2 * w), 2 * w), :]
        kw = kn_ref[pl.ds(pl.multiple_of(u0 * w, w), nku), :]
        vw = vn_ref[pl.ds(pl.multiple_of(u0 * w, w), nku), :]
        bias = jnp.concatenate([bias_ref[va], bias_ref[vb]], axis=0)
        s_lat = _nt_dot(q2, kw) * scale + bias
        s_ctx = _nt_dot(q2, kc) * scale
        m = jnp.maximum(jnp.max(s_lat, axis=-1, keepdims=True), jnp.max(s_ctx, axis=-1, keepdims=True))
        p_lat = jnp.exp(s_lat - m)
        p_ctx = jnp.exp(s_ctx - m)
        den = jnp.sum(p_lat, axis=-1, keepdims=True) + jnp.sum(p_ctx, axis=-1, keepdims=True)
        out = (jnp.dot(p_lat.astype(BF16), vw, preferred_element_type=F32)
               + jnp.dot(p_ctx.astype(BF16), vc, preferred_element_type=F32))
        o_ref[pl.ds(pl.multiple_of(ra * w, 2 * w), 2 * w), :] = (out / den).astype(o_ref.dtype)
        return carry

    lax.fori_loop(0, rows // 2, body, 0, unroll=4)


def _na_bias_table(rel_bias, kr, rows):
    w = GRID_W
    nk = kr * w
    ku = min(kr + 1, rows)
    n_dc = 2 * NA_COLS - 1
    dc = np.clip(np.arange(w)[None, :] - np.arange(w)[:, None] + NA_COLS - 1, 0, n_dc - 1)
    onehot = jnp.asarray((dc[:, :, None] == np.arange(n_dc)).astype(np.float32))
    toep = jnp.einsum("hrd,ckd->hrck", rel_bias.astype(F32), onehot, precision=lax.Precision.HIGHEST)
    r_lo = NA_ROWS - kr
    tab = jnp.stack([toep[:, r_lo + off:r_lo + off + kr] for off in range(kr)], axis=1)
    tab = jnp.transpose(tab, (0, 1, 3, 2, 4)).reshape(rel_bias.shape[0], kr, w, nk)
    c0 = np.clip(np.arange(w) - NA_COLS // 2, 0, w - NA_COLS)[:, None]
    kcol = (np.arange(nk) % w)[None, :]
    in_win = jnp.asarray((kcol >= c0) & (kcol < c0 + NA_COLS))
    tab = jnp.where(in_win, tab, NEG_INF)
    pad = ku * w - nk
    shifted = [jnp.pad(tab, ((0, 0), (0, 0), (0, 0), (d * w, pad - d * w)), constant_values=NEG_INF)
               for d in range(pad // w + 1)]
    shifted = (shifted + shifted)[:2]
    return jnp.stack(shifted, axis=2).reshape(rel_bias.shape[0], 2 * kr, w, ku * w)


def _na_latent(p, q_gain, k_gain, bias_tab, dims):
    b, l, lc = dims["B"], dims["L"], dims["Lc"]
    rows = l // GRID_W
    kr = min(NA_ROWS, rows)
    nk = min(kr + 1, rows) * GRID_W
    hd = NA_DH
    assert rows % 2 == 0
    kern = functools.partial(_na_kernel, rows=rows, kr=kr)
    ctx_blk = (b * l) // lc
    return pl.pallas_call(
        kern,
        grid=(b, NA_HEADS),
        in_specs=[pl.BlockSpec((l, hd), lambda i, h: (i, OFF_NA_Q // hd + h)),
                  pl.BlockSpec((l, hd), lambda i, h: (i, OFF_NA_K // hd + h)),
                  pl.BlockSpec((l, hd), lambda i, h: (i, OFF_NA_V // hd + h)),
                  pl.BlockSpec((lc, hd), lambda i, h: (ctx_blk + i, OFF_NA_K // hd + h)),
                  pl.BlockSpec((lc, hd), lambda i, h: (ctx_blk + i, OFF_NA_V // hd + h)),
                  pl.BlockSpec((1, hd), lambda i, h: (0, 0)),
                  pl.BlockSpec((1, hd), lambda i, h: (0, 0)),
                  pl.BlockSpec((None, 2 * kr, GRID_W, nk), lambda i, h: (h, 0, 0, 0))],
        out_specs=pl.BlockSpec((l, hd), lambda i, h: (i, h)),
        out_shape=jax.ShapeDtypeStruct((b * l, NA_W), BF16),
        scratch_shapes=[pltpu.VMEM((l, hd), BF16), pltpu.VMEM((l, hd), BF16), pltpu.VMEM((l, hd), BF16)],
        compiler_params=_cparams("parallel", "parallel"),
        name="na_latent",
    )(p, p, p, p, p, q_gain.reshape(1, hd), k_gain.reshape(1, hd), bias_tab)


def _ctx_attn_kernel(q_ref, k_ref, v_ref, qg_ref, kg_ref, o_ref):
    q = _head_rms(q_ref[...], qg_ref[...]).astype(BF16)
    k = _head_rms(k_ref[...], kg_ref[...]).astype(BF16)
    s = _nt_dot(q, k) * (NA_DH ** -0.5)
    m = jnp.max(s, axis=-1, keepdims=True)
    pr = jnp.exp(s - m)
    den = jnp.sum(pr, axis=-1, keepdims=True)
    out = jnp.dot(pr.astype(BF16), v_ref[...].astype(BF16), preferred_element_type=F32)
    o_ref[...] = (out / den).astype(o_ref.dtype)


def _na_context(p, q_gain, k_gain, dims):
    b, l, lc = dims["B"], dims["L"], dims["Lc"]
    hd = NA_DH
    ctx_blk = (b * l) // lc
    return pl.pallas_call(
        _ctx_attn_kernel,
        grid=(b, NA_HEADS),
        in_specs=[pl.BlockSpec((lc, hd), lambda i, h: (ctx_blk + i, OFF_NA_Q // hd + h)),
                  pl.BlockSpec((lc, hd), lambda i, h: (ctx_blk + i, OFF_NA_K // hd + h)),
                  pl.BlockSpec((lc, hd), lambda i, h: (ctx_blk + i, OFF_NA_V // hd + h)),
                  pl.BlockSpec((1, hd), lambda i, h: (0, 0)),
                  pl.BlockSpec((1, hd), lambda i, h: (0, 0))],
        out_specs=pl.BlockSpec((lc, hd), lambda i, h: (i, h)),
        out_shape=jax.ShapeDtypeStruct((b * lc, NA_W), BF16),
        compiler_params=_cparams("parallel", "parallel"),
        name="na_context",
    )(p, p, p, q_gain.reshape(1, hd), k_gain.reshape(1, hd))


def _rope_tables(seq, pad_rows):
    nf = ML_DK // 4
    t = np.arange(seq)
    pos = np.stack([t // GRID_W, t % GRID_W], axis=-1).astype(np.float32)
    inv_freq = (ROPE_BASE ** (-np.arange(nf, dtype=np.float32) / nf)).astype(np.float32)
    lane = np.arange(ML_DK)
    axis, pair, f = lane // (2 * nf), (lane // nf) % 2, lane % nf
    ang = jnp.asarray(pos[:, axis]) * jnp.asarray(inv_freq[f])[None, :]
    cos, sin = jnp.cos(ang), jnp.sin(ang)
    s_lo = jnp.where(jnp.asarray(pair == 0)[None, :], -sin, 0.0)
    s_hi = jnp.where(jnp.asarray(pair == 1)[None, :], sin, 0.0)
    pad = lambda a, v: jnp.concatenate([a, jnp.full((pad_rows, ML_DK), v, F32)], axis=0)
    return pad(cos, 1.0), pad(s_lo, 0.0), pad(s_hi, 0.0)


def _mlprep_kernel(x_ref, xp_ref, xn_ref, v0_ref, v1_ref, w_ref, cos_ref, slo_ref, shi_ref,
                   q_ref, kt_ref, v_ref, *, n_lat, seq, seq_ctx):
    i, j = pl.program_id(0), pl.program_id(1)
    t = x_ref.shape[0]
    nf = ML_DK // 4
    r0 = i * t
    is_lat = r0 < n_lat
    starts = jnp.where(is_lat, r0 % seq == 0, (r0 - n_lat) % seq_ctx == 0)
    ends = jnp.where(is_lat, (r0 + t) % seq == 0, (r0 + t - n_lat) % seq_ctx == 0)
    row = lax.broadcasted_iota(I32, (t, ML_DK), 0)

    def rope(h):
        cs = slice(h * ML_DK, (h + 1) * ML_DK)
        x = x_ref[:, cs]
        prev_row = jnp.where(starts, 0.0, xp_ref[7:8, cs])
        next_row = jnp.where(ends, 0.0, xn_ref[0:1, cs])
        x_prev = jnp.where(row == 0, prev_row, pltpu.roll(x, 1, 0))
        x_next = jnp.where(row == t - 1, next_row, pltpu.roll(x, t - 1, 0))
        yh = jax.nn.silu(x_prev * w_ref[0:1, cs] + x * w_ref[1:2, cs] + x_next * w_ref[2:3, cs])
        return (yh * cos_ref[...] + pltpu.roll(yh, ML_DK - nf, 1) * slo_ref[...]
                + pltpu.roll(yh, nf, 1) * shi_ref[...])

    @pl.when(j == 0)
    def _():
        half = ML_HEADS // 2
        for h in range(ML_HEADS):
            q_ref[h] = rope(h).astype(q_ref.dtype)
            hv = h % half
            v_src = v0_ref if h < half else v1_ref
            v_ref[h] = v_src[:, hv * ML_DV:(hv + 1) * ML_DV].astype(v_ref.dtype)

    @pl.when(j == 1)
    def _():
        for h in range(ML_HEADS):
            kt_ref[h] = (rope(h) * (ML_DK ** -0.5)).T.astype(kt_ref.dtype)


def _ml_prep(p, conv_w, tables, dims):
    m = p.shape[0]
    t = ROW_TILE
    n_lat, l, lc = dims["n_lat"], dims["L"], dims["Lc"]
    hd = ML_DK
    wb = ML_QK
    col0 = OFF_ML_QK // wb
    nblk8 = m // 8
    lat_blocks = l // t

    def tab_idx(i, j):
        return (jnp.where(i * t < n_lat, i % lat_blocks, lat_blocks), 0)

    kern = functools.partial(_mlprep_kernel, n_lat=n_lat, seq=l, seq_ctx=lc)
    return pl.pallas_call(
        kern,
        grid=(m // t, 2),
        in_specs=[pl.BlockSpec((t, wb), lambda i, j: (i, col0 + j)),
                  pl.BlockSpec((8, wb), lambda i, j: (jnp.maximum(i * (t // 8) - 1, 0), col0 + j)),
                  pl.BlockSpec((8, wb), lambda i, j: (jnp.minimum((i + 1) * (t // 8), nblk8 - 1), col0 + j)),
                  pl.BlockSpec((t, HALF_W), lambda i, j: (i, OFF_ML_V // HALF_W)),
                  pl.BlockSpec((t, HALF_W), lambda i, j: (i, OFF_ML_V // HALF_W + 1)),
                  pl.BlockSpec((conv_w.shape[0], wb), lambda i, j: (0, j)),
                  pl.BlockSpec((t, hd), tab_idx),
                  pl.BlockSpec((t, hd), tab_idx),
                  pl.BlockSpec((t, hd), tab_idx)],
        out_specs=[pl.BlockSpec((ML_HEADS, t, ML_DK), lambda i, j: (0, i, 0)),
                   pl.BlockSpec((ML_HEADS, ML_DK, t), lambda i, j: (0, 0, i)),
                   pl.BlockSpec((ML_HEADS, t, ML_DV), lambda i, j: (0, i, 0))],
        out_shape=[jax.ShapeDtypeStruct((ML_HEADS, m, ML_DK), BF16),
                   jax.ShapeDtypeStruct((ML_HEADS, ML_DK, m), BF16),
                   jax.ShapeDtypeStruct((ML_HEADS, m, ML_DV), BF16)],
        compiler_params=_cparams("arbitrary", "arbitrary"),
        name="ml_prep",
    )(p, p, p, p, p, conv_w, *tables)


ML_AUG = ML_DV + LANES


def _mlstm_kernel(q_ref, kt_ref, v_ref, gr_ref, o_ref, c_ref, m_ref, cum_ref):
    d, s = pl.program_id(1), pl.program_id(2)
    t = ML_CHUNK
    nh = ML_HEADS

    @pl.when(s == 0)
    def _():
        c_ref[...] = jnp.zeros_like(c_ref)
        m_ref[...] = jnp.zeros_like(m_ref)

    fwd = d == 0
    ri = lax.broadcasted_iota(I32, (t, t), 0)
    ci = lax.broadcasted_iota(I32, (t, t), 1)
    tri = (ci - ri) * jnp.where(fwd, 1, -1) <= 0
    trif = tri.astype(F32)
    cum_ref[...] = lax.dot_general(gr_ref[...], trif, (((1,), (1,)), ((), ())), preferred_element_type=F32,
                                   precision=lax.Precision.HIGHEST)
    row_i = jnp.where(fwd, 0, 2 * nh)
    row_f = row_i + nh
    ones = jnp.ones((t, LANES), BF16)

    def head(h, carry):
        lir = gr_ref[pl.ds(row_i + h, 1), :]
        lfr = gr_ref[pl.ds(row_f + h, 1), :]
        br = cum_ref[pl.ds(row_f + h, 1), :]
        bc = jnp.sum(trif * lfr, axis=-1, keepdims=True)
        blh = jnp.sum(lfr, axis=-1, keepdims=True)
        m_h = m_ref[pl.ds(h, 1), 0:1]
        a = bc + m_h
        dlog = jnp.where(tri, bc - br + lir, NEG_INF)
        mj = jnp.maximum(a, jnp.max(dlog, axis=-1, keepdims=True))
        w_inter = jnp.exp(a - mj)
        qh = q_ref[h]
        kth = kt_ref[h]
        vh = v_ref[h]
        sm = jnp.dot(qh, kth, preferred_element_type=F32) * jnp.exp(dlog - mj)
        c_h = c_ref[h]
        qc = jnp.dot(qh, c_h.astype(BF16), preferred_element_type=F32)
        num = w_inter * qc[:, :ML_DV] + jnp.dot(sm.astype(BF16), vh, preferred_element_type=F32)
        den = w_inter * qc[:, ML_DV:ML_DV + 1] + jnp.sum(sm, axis=-1, keepdims=True)
        o_ref[h] = num / jnp.maximum(jnp.abs(den), jnp.exp(-mj))
        gl_r = blh - br + lir
        m_new = jnp.maximum(blh + m_h, jnp.max(gl_r, axis=-1, keepdims=True))
        sc = jnp.exp(blh + m_h - m_new)
        ktw = (kth.astype(F32) * jnp.exp(gl_r - m_new)).astype(BF16)
        v_aug = jnp.concatenate([vh, ones], axis=1)
        c_ref[h] = sc * c_h + jnp.dot(ktw, v_aug, preferred_element_type=F32)
        m_ref[pl.ds(h, 1), :] = jnp.broadcast_to(m_new, (1, LANES))
        return carry

    lax.fori_loop(0, nh, head, 0, unroll=8)


def _mlstm_scan(q3, kt3, v3, gates_t, dims):
    m = q3.shape[1]
    t = ML_CHUNK
    b, l, lc, n_lat = dims["B"], dims["L"], dims["Lc"], dims["n_lat"]
    ncl, ncc = l // t, lc // t
    ng = gates_t.shape[0]

    def chunk(i, d, s):
        in_ctx = s < ncc
        c_ctx = jnp.where(d == 0, s, ncc - 1 - s)
        s_lat = s - ncc
        c_lat = jnp.where(d == 0, s_lat, ncl - 1 - s_lat)
        return jnp.where(in_ctx, n_lat // t + i * ncc + c_ctx, i * ncl + c_lat)

    return pl.pallas_call(
        _mlstm_kernel,
        grid=(b, 2, ncc + ncl),
        in_specs=[pl.BlockSpec((ML_HEADS, t, ML_DK), lambda i, d, s: (0, chunk(i, d, s), 0)),
                  pl.BlockSpec((ML_HEADS, ML_DK, t), lambda i, d, s: (0, 0, chunk(i, d, s))),
                  pl.BlockSpec((ML_HEADS, t, ML_DV), lambda i, d, s: (0, chunk(i, d, s), 0)),
                  pl.BlockSpec((ng, t), lambda i, d, s: (0, chunk(i, d, s)))],
        out_specs=pl.BlockSpec((None, ML_HEADS, t, ML_DV), lambda i, d, s: (d, 0, chunk(i, d, s), 0)),
        out_shape=jax.ShapeDtypeStruct((2, ML_HEADS, m, ML_DV), F32),
        scratch_shapes=[pltpu.VMEM((ML_HEADS, ML_DK, ML_AUG), F32),
                        pltpu.VMEM((ML_HEADS, LANES), F32),
                        pltpu.VMEM((ng, t), F32)],
        compiler_params=_cparams("parallel", "parallel", "arbitrary"),
        name="mlstm_scan",
    )(q3, kt3, v3, gates_t)


def _mlout_kernel(hf_ref, hb_ref, og_ref, g_ref, y_ref):
    for h in range(HALF_W // ML_DV):
        sl = slice(h * ML_DV, (h + 1) * ML_DV)
        x = hf_ref[h] + hb_ref[h]
        y = x * lax.rsqrt(jnp.mean(x * x, axis=-1, keepdims=True) + EPS) * g_ref[:, sl]
        y_ref[:, sl] = (y * jax.nn.sigmoid(og_ref[:, sl])).astype(y_ref.dtype)


def _ml_out(hs, p, gain, rows):
    tm = ROW_TILE
    wb = HALF_W
    hb = wb // ML_DV
    return pl.pallas_call(
        _mlout_kernel,
        grid=(rows // tm, ML_V // wb),
        in_specs=[pl.BlockSpec((None, hb, tm, ML_DV), lambda i, j: (0, j, i, 0)),
                  pl.BlockSpec((None, hb, tm, ML_DV), lambda i, j: (1, j, i, 0)),
                  pl.BlockSpec((tm, wb), lambda i, j: (i, OFF_ML_O // wb + j)),
                  pl.BlockSpec((1, wb), lambda i, j: (0, j))],
        out_specs=pl.BlockSpec((tm, wb), lambda i, j: (i, j)),
        out_shape=jax.ShapeDtypeStruct((rows, ML_V), BF16),
        compiler_params=_cparams("parallel", "parallel"),
        name="ml_out",
    )(hs, hs, p, gain.reshape(1, ML_V))


def _gelu_exact(x):
    return 0.5 * x * (1.0 + lax.erf(x * np.float32(np.sqrt(0.5))))


def _gmlp_kernel(u_ref, v_ref, g_ref, ws_ref, bt_ref, o_ref):
    u = _gelu_exact(u_ref[...])
    v = _gelu_exact(v_ref[...])
    vn = (v * lax.rsqrt(jnp.mean(v * v, axis=-1, keepdims=True) + EPS) * g_ref[...]).astype(BF16)
    bt = bt_ref[...]
    for c in range(u.shape[0] // GM_CHUNK):
        rs = slice(c * GM_CHUNK, (c + 1) * GM_CHUNK)
        for g in range(GM_GROUPS):
            cs = slice(g * GM_CPG, (g + 1) * GM_CPG)
            mixed = jnp.dot(ws_ref[g].astype(BF16), vn[rs, cs], preferred_element_type=F32) + bt[:, g:g + 1]
            o_ref[rs, cs] = (u[rs, cs] * mixed).astype(o_ref.dtype)


def _gmlp(p, v_gain, ws, bias, rows):
    tm = ROW_TILE
    return pl.pallas_call(
        _gmlp_kernel,
        grid=(rows // tm,),
        in_specs=[pl.BlockSpec((tm, GM_W), lambda i: (i, OFF_GM_U // GM_W)),
                  pl.BlockSpec((tm, GM_W), lambda i: (i, OFF_GM_V // GM_W)),
                  pl.BlockSpec((1, GM_W), lambda i: (0, 0)),
                  pl.BlockSpec((GM_GROUPS, GM_CHUNK, GM_CHUNK), lambda i: (0, 0, 0)),
                  pl.BlockSpec((GM_CHUNK, GM_GROUPS), lambda i: (0, 0))],
        out_specs=pl.BlockSpec((tm, GM_W), lambda i: (i, 0)),
        out_shape=jax.ShapeDtypeStruct((rows, GM_W), BF16),
        compiler_params=_cparams("parallel"),
        name="gmlp",
    )(p, p, v_gain.reshape(1, GM_W), ws, bias.T)


def _branch_kernel(ya_ref, ym_ref, yg_ref, w_ref, g0_ref, g1_ref, g2_ref, o_ref, wp_ref):
    @pl.when(pl.program_id(1) == 0)
    def _():
        wp_ref[...] = w_ref[...].astype(BF16)

    dot = functools.partial(jnp.dot, preferred_element_type=F32)
    za = dot(ya_ref[...], wp_ref[0:NA_W, :])
    zm = dot(ym_ref[...], wp_ref[NA_W:NA_W + ML_V, :])
    zg = dot(yg_ref[...], wp_ref[NA_W + ML_V:, :])
    z = (jax.nn.sigmoid(g0_ref[...]) * za + jax.nn.sigmoid(g1_ref[...]) * zm
         + jax.nn.sigmoid(g2_ref[...]) * zg)
    o_ref[...] = z.astype(o_ref.dtype)


def _branch(ya, ym, yg, w_branch, layer, p, d_model, rows):
    tn = 512
    tm = 2 * MM_TM if rows % (2 * MM_TM) == 0 else MM_TM
    k = w_branch.shape[1]
    gate_blk = lambda b: (OFF_GATE + b * d_model) // tn
    return pl.pallas_call(
        _branch_kernel,
        grid=(d_model // tn, rows // tm),
        in_specs=[pl.BlockSpec((tm, NA_W), lambda j, i: (i, 0)),
                  pl.BlockSpec((tm, ML_V), lambda j, i: (i, 0)),
                  pl.BlockSpec((tm, GM_W), lambda j, i: (i, 0)),
                  pl.BlockSpec((None, k, tn), lambda j, i: (layer, 0, j)),
                  pl.BlockSpec((tm, tn), lambda j, i: (i, gate_blk(0) + j)),
                  pl.BlockSpec((tm, tn), lambda j, i: (i, gate_blk(1) + j)),
                  pl.BlockSpec((tm, tn), lambda j, i: (i, gate_blk(2) + j))],
        out_specs=pl.BlockSpec((tm, tn), lambda j, i: (i, j)),
        out_shape=jax.ShapeDtypeStruct((rows, d_model), BF16),
        scratch_shapes=[pltpu.VMEM((k, tn), BF16)],
        compiler_params=_cparams("parallel", "arbitrary"),
        name="branch_merge",
    )(ya, ym, yg, w_branch, p, p, p)


def _outproj_kernel(z_ref, w_ref, x_ref, gt_ref, o_ref, wp_ref):
    @pl.when(pl.program_id(1) == 0)
    def _():
        wp_ref[...] = w_ref[...].astype(BF16)

    y = jnp.dot(z_ref[...], wp_ref[...], preferred_element_type=F32)
    o_ref[...] = x_ref[...] + gt_ref[...] * y


def _outproj(z, w_out, layer, x, mod3, k_gate, dims, rows):
    k = z.shape[1]
    d = w_out.shape[2]
    tn = 512
    tm = 2 * MM_TM if (rows % (2 * MM_TM) == 0 and dims["L"] % (2 * MM_TM) == 0) else MM_TM
    mrow = functools.partial(_mod_row, tm=tm, n_lat=dims["n_lat"], seq=dims["L"], n_batch=dims["B"])
    gblk = k_gate * (d // tn)
    return pl.pallas_call(
        _outproj_kernel,
        grid=(d // tn, rows // tm),
        in_specs=[pl.BlockSpec((tm, k), lambda j, i: (i, 0)),
                  pl.BlockSpec((None, k, tn), lambda j, i: (layer, 0, j)),
                  pl.BlockSpec((tm, tn), lambda j, i: (i, j)),
                  pl.BlockSpec((None, 1, tn), lambda j, i: (mrow(i), 0, gblk + j))],
        out_specs=pl.BlockSpec((tm, tn), lambda j, i: (i, j)),
        out_shape=jax.ShapeDtypeStruct((rows, d), F32),
        scratch_shapes=[pltpu.VMEM((k, tn), BF16)],
        compiler_params=_cparams("parallel", "arbitrary"),
        name="out_proj",
    )(z, w_out, x, mod3)


def _affinity_kernel(h_ref, w_ref, o_ref, *, n_exp):
    logits = jnp.dot(_unpack_bf16_pairs(h_ref[...]), w_ref[...], preferred_element_type=F32)
    col = lax.broadcasted_iota(I32, logits.shape, 1)
    logits = jnp.where(col < n_exp, logits, NEG_INF)
    o_ref[...] = jax.nn.softmax(logits, axis=-1)


def _affinity(h2, w_router, rows):
    d, e = w_router.shape
    tm = MM_TM
    w_pad = jnp.pad(w_router, ((0, 0), (0, LANES - e))).astype(BF16)
    return pl.pallas_call(
        functools.partial(_affinity_kernel, n_exp=e),
        grid=(rows // tm,),
        in_specs=[pl.BlockSpec((tm, d // 2), lambda i: (i, 0)),
                  pl.BlockSpec((d, LANES), lambda i: (0, 0))],
        out_specs=pl.BlockSpec((tm, LANES), lambda i: (i, 0)),
        out_shape=jax.ShapeDtypeStruct((rows, LANES), F32),
        compiler_params=_cparams("parallel"),
        name="router_affinity",
    )(h2, w_pad)


def _cumsum_rows(x01, blk):
    n = x01.shape[0]
    ri = lax.broadcasted_iota(I32, (blk, blk), 0)
    ci = lax.broadcasted_iota(I32, (blk, blk), 1)
    tril = (ci <= ri).astype(BF16)
    parts = []
    carry = jnp.zeros((1, x01.shape[1]), F32)
    for j in range(n // blk):
        cs = jnp.dot(tril, x01[j * blk:(j + 1) * blk, :].astype(BF16), preferred_element_type=F32) + carry
        parts.append(cs)
        carry = cs[blk - 1:blk, :]
    return jnp.concatenate(parts, axis=0) if len(parts) > 1 else parts[0]


def _select_kernel(aff_ref, idx_ref, gate_ref, rank_ref, *, cap, tb, n_exp):
    aff = aff_ref[...]
    n = aff.shape[0]
    key = lax.bitcast_convert_type(aff, I32)
    capf = jnp.float32(cap)

    def search(i, thr):
        cand = thr | jnp.left_shift(jnp.int32(1), 30 - i)
        cnt = jnp.sum((key >= cand).astype(F32), axis=0, keepdims=True)
        return jnp.where(cnt >= capf, cand, thr)

    thr = lax.fori_loop(0, 31, search, jnp.zeros((1, LANES), I32))
    above = key > thr
    tied = key == thr
    need = capf - jnp.sum(above.astype(F32), axis=0, keepdims=True)
    tied_f = tied.astype(F32)
    tie_rank = _cumsum_rows(tied_f, tb) - tied_f
    sel = above | (tied & (tie_rank < need))
    incl = _cumsum_rows(sel.astype(F32), tb)
    rank_ref[0] = incl
    rank_ref[1] = jnp.where(sel, incl, -1.0)
    rank_ref[2] = aff
    slot = lax.broadcasted_iota(I32, (tb, cap), 1).astype(F32)

    for ex in range(n_exp):
        def block(j, acc):
            acc_i, acc_g = acc
            rows = pl.ds(pl.multiple_of(j * tb, tb), tb)
            inc = rank_ref[0, rows, ex:ex + 1]
            inc_sel = rank_ref[1, rows, ex:ex + 1]
            a = rank_ref[2, rows, ex:ex + 1]
            acc_i = acc_i + jnp.sum(jnp.where(inc <= slot, 1.0, 0.0), axis=0, keepdims=True)
            acc_g = acc_g + jnp.sum(jnp.where(inc_sel == slot + 1.0, a, 0.0), axis=0, keepdims=True)
            return acc_i, acc_g

        zero = jnp.zeros((1, cap), F32)
        acc_i, acc_g = lax.fori_loop(0, n // tb, block, (zero, zero))
        idx_ref[ex:ex + 1, :] = acc_i.astype(I32)
        gate_ref[ex:ex + 1, :] = acc_g


def _select(aff, e, n_groups, group_len, blk0, cap):
    tb = min(256, group_len)
    kern = functools.partial(_select_kernel, cap=cap, tb=tb, n_exp=e)
    return pl.pallas_call(
        kern,
        grid=(n_groups,),
        in_specs=[pl.BlockSpec((group_len, LANES), lambda g: (blk0 + g, 0))],
        out_specs=[pl.BlockSpec((None, e, cap), lambda g: (g, 0, 0)),
                   pl.BlockSpec((None, e, cap), lambda g: (g, 0, 0))],
        out_shape=[jax.ShapeDtypeStruct((n_groups, e, cap), I32),
                   jax.ShapeDtypeStruct((n_groups, e, cap), F32)],
        scratch_shapes=[pltpu.VMEM((3, group_len, LANES), F32)],
        compiler_params=_cparams("parallel"),
        name="router_select",
    )(aff)


def _moe_up_kernel(idx_ref, h_hbm, wg_ref, wu_ref, o_ref, xs_ref, xb_ref, sem, *, n_rows, n_exp, n_f):
    e, f = pl.program_id(0), pl.program_id(1)
    per_step = n_rows // n_f

    def row_copy(ex, i):
        slot = ex % 2
        src = idx_ref[ex * n_rows + i]
        return pltpu.make_async_copy(h_hbm.at[pl.ds(src, 1), :], xs_ref.at[slot, pl.ds(i, 1), :], sem.at[slot])

    def start_rows(ex, lo, n):
        def start(i, c):
            row_copy(ex, lo + i).start()
            return c
        lax.fori_loop(0, n, start, 0, unroll=8)

    @pl.when(jnp.logical_and(e == 0, f == 0))
    def _():
        start_rows(0, 0, n_rows)

    @pl.when(f == 0)
    def _():
        def wait(i, c):
            row_copy(e, i).wait()
            return c
        lax.fori_loop(0, n_rows, wait, 0, unroll=4)
        xb_ref[...] = _unpack_bf16_pairs(xs_ref[e % 2])

    @pl.when(e + 1 < n_exp)
    def _():
        start_rows(e + 1, f * per_step, per_step)

    xb = xb_ref[...]
    a = jnp.dot(xb, wg_ref[...].astype(BF16), preferred_element_type=F32)
    u = jnp.dot(xb, wu_ref[...].astype(BF16), preferred_element_type=F32)
    o_ref[...] = (jax.nn.silu(a) * u).astype(o_ref.dtype)


def _moe_up(idx_flat, h2_packed, w_gate, w_up, layer, n_rows):
    _, e, d, ff = w_gate.shape
    tf = 256
    n_f = ff // tf
    assert n_rows % n_f == 0
    kern = functools.partial(_moe_up_kernel, n_rows=n_rows, n_exp=e, n_f=n_f)
    return pl.pallas_call(
        kern,
        grid_spec=pltpu.PrefetchScalarGridSpec(
            num_scalar_prefetch=1,
            grid=(e, n_f),
            in_specs=[pl.BlockSpec(memory_space=pl.ANY),
                      pl.BlockSpec((None, None, d, tf), lambda i, f, idx: (layer, i, 0, f)),
                      pl.BlockSpec((None, None, d, tf), lambda i, f, idx: (layer, i, 0, f))],
            out_specs=pl.BlockSpec((None, n_rows, tf), lambda i, f, idx: (i, 0, f)),
            scratch_shapes=[pltpu.VMEM((2, n_rows, d // 2), jnp.uint32), pltpu.VMEM((n_rows, d), BF16),
                            pltpu.SemaphoreType.DMA((2,))]),
        out_shape=jax.ShapeDtypeStruct((e, n_rows, ff), BF16),
        compiler_params=_cparams("arbitrary", "arbitrary"),
        name="moe_up",
    )(idx_flat, h2_packed, w_gate, w_up)


def _moe_down_kernel(idx_ref, hid_ref, gcol_ref, wd_ref, gt_ref, x_hbm, o_hbm, acc_ref, y_ref, sem, *,
                     cap, n_exp, group_len, row0, dc):
    g, j, e = pl.program_id(0), pl.program_id(1), pl.program_id(2)
    nlt = dc // LANES
    sub = 8
    rows = pl.ds(pl.multiple_of((row0 + g * group_len) // sub, group_len // sub), group_len // sub)

    def tile_copy(c, to_vmem):
        hbm = (x_hbm if to_vmem else o_hbm).at[rows, :, pl.ds(pl.multiple_of(j * dc + c * LANES, LANES), LANES)]
        vmem = acc_ref.at[:, c]
        return pltpu.make_async_copy(hbm, vmem, sem.at[0]) if to_vmem else pltpu.make_async_copy(vmem, hbm, sem.at[1])

    @pl.when(e == 0)
    def _():
        for c in range(nlt):
            tile_copy(c, True).start()

    chunk = min(cap, 128)
    w = wd_ref[...].astype(BF16)
    gt = gt_ref[...]
    for rc in range(cap // chunk):
        rs = slice(rc * chunk, (rc + 1) * chunk)
        y = jnp.dot(hid_ref[rs, :], w, preferred_element_type=F32) * gcol_ref[rs, :] * gt
        for rr in range(chunk // sub):
            for c in range(nlt):
                y_ref[rc * chunk // sub + rr, c] = y[rr * sub:(rr + 1) * sub, c * LANES:(c + 1) * LANES]

    @pl.when(e == 0)
    def _():
        for c in range(nlt):
            tile_copy(c, True).wait()

    base = (g * n_exp + e) * cap
    acc_flat = acc_ref.reshape(group_len * nlt, LANES)
    y_flat = y_ref.reshape(cap * nlt, LANES)

    def add_rows(bi, carry):
        i0 = bi * SCATTER_BATCH
        starts = [idx_ref[base + i0 + r] for r in range(SCATTER_BATCH)]
        acc = [acc_flat[pl.ds(s, nlt, stride=sub), :] for s in starts]
        for r in range(SCATTER_BATCH):
            yr = y_flat[pl.ds(i0 * nlt + r, nlt, stride=sub), :]
            acc_flat[pl.ds(starts[r], nlt, stride=sub), :] = acc[r] + yr
        return carry

    lax.fori_loop(0, cap // SCATTER_BATCH, add_rows, 0)

    @pl.when(e == n_exp - 1)
    def _():
        for c in range(nlt):
            tile_copy(c, False).start()
        for c in range(nlt):
            tile_copy(c, False).wait()


def _moe_down(idx_flat, hid, gate_col, w_down, layer, x, mod3, k_gate, n_groups, group_len, cap,
              hid_row0, x_row0, mod_row_of_group, out_rows):
    _, e, ff, d = w_down.shape
    assert SCATTER_BATCH == 8 and cap % 8 == 0 and x_row0 % group_len == 0 and out_rows % 8 == 0
    dc = 1024
    nlt = dc // LANES
    kern = functools.partial(_moe_down_kernel, cap=cap, n_exp=e, group_len=group_len, row0=x_row0, dc=dc)
    hblk0 = hid_row0 // cap
    gblk = k_gate * (d // dc)
    out = pl.pallas_call(
        kern,
        grid_spec=pltpu.PrefetchScalarGridSpec(
            num_scalar_prefetch=1,
            grid=(n_groups, d // dc, e),
            in_specs=[pl.BlockSpec((None, cap, ff), lambda g, j, i, idx: (i, hblk0 + g, 0)),
                      pl.BlockSpec((None, cap, 1), lambda g, j, i, idx: (i, hblk0 + g, 0)),
                      pl.BlockSpec((None, None, ff, dc), lambda g, j, i, idx: (layer, i, 0, j)),
                      pl.BlockSpec((None, 1, dc), lambda g, j, i, idx: (mod_row_of_group(g), 0, gblk + j)),
                      pl.BlockSpec(memory_space=pl.ANY)],
            out_specs=pl.BlockSpec(memory_space=pl.ANY),
            scratch_shapes=[pltpu.VMEM((group_len // 8, nlt, 8, LANES), F32),
                            pltpu.VMEM((cap // 8, nlt, 8, LANES), F32),
                            pltpu.SemaphoreType.DMA((2,))]),
        out_shape=jax.ShapeDtypeStruct((out_rows // 8, 8, d), F32),
        input_output_aliases={5: 0},
        compiler_params=_cparams("arbitrary", "arbitrary", "arbitrary"),
        name="moe_down",
    )((idx_flat >> 3) * (8 * nlt) + (idx_flat & 7), hid, gate_col, w_down, mod3, x.reshape(out_rows // 8, 8, d))
    return out.reshape(out_rows, d)


def _layer(x, mod_l, layer, last, dims, w_in, na_q_gain, na_k_gain, na_rel_bias, ml_conv, ml_gate_bias,
           ml_out_gain, gm_v_gain, gm_ws, gm_bias, w_branch, w_out, g_norm1, g_norm2, w_router, w_e_gate, w_e_up,
           w_e_down, rope_tabs):
    b, l, lc, n_lat = dims["B"], dims["L"], dims["Lc"], dims["n_lat"]
    m, d = x.shape
    rows = n_lat if last else m
    mod3 = mod_l.reshape(mod_l.shape[0], 1, mod_l.shape[1])
    n_exp = w_router.shape[1]

    h = _prenorm(x, g_norm1, mod3, 0, 1, dims, m, False)
    p = _in_proj(h, w_in, layer)
    b_g = jnp.pad(ml_gate_bias.reshape(1, N_ML_GATES), ((0, 0), (0, LANES - N_ML_GATES)))
    gates = _ml_gates(h, w_in, layer, b_g)
    gates_t = gates[:, :N_ML_GATES].T

    kr = min(NA_ROWS, l // GRID_W)
    ya = _na_latent(p, na_q_gain, na_k_gain, _na_bias_table(na_rel_bias, kr, l // GRID_W), dims)
    if not last:
        ya = jnp.concatenate([ya, _na_context(p, na_q_gain, na_k_gain, dims)], axis=0)

    q3, kt3, v3 = _ml_prep(p, ml_conv, rope_tabs, dims)
    hs = _mlstm_scan(q3, kt3, v3, gates_t, dims)
    ym = _ml_out(hs, p, ml_out_gain, rows)
    yg = _gmlp(p, gm_v_gain, gm_ws, gm_bias, rows)

    z = _branch(ya, ym, yg, w_branch, layer, p, d, rows)
    x1 = _outproj(z, w_out, layer, x, mod3, 2, dims, rows)

    h2 = _prenorm(x1, g_norm2, mod3, 3, 4, dims, rows, True)
    aff = _affinity(h2, w_router, rows)
    cap_l = max(1, min(l, (CAPACITY_FACTOR * l) // n_exp))
    idx_l, gate_l = _select(aff, n_exp, b, l, 0, cap_l)
    row_l = idx_l + (jnp.arange(b, dtype=I32) * l)[:, None, None]
    src = [jnp.swapaxes(row_l, 0, 1).reshape(n_exp, b * cap_l)]
    gsel = [jnp.swapaxes(gate_l, 0, 1).reshape(n_exp, b * cap_l)]
    if not last:
        cap_c = max(1, min(lc, (CAPACITY_FACTOR * lc) // n_exp))
        idx_c, gate_c = _select(aff, n_exp, b, lc, n_lat // lc, cap_c)
        row_c = idx_c + (n_lat + jnp.arange(b, dtype=I32) * lc)[:, None, None]
        src.append(jnp.swapaxes(row_c, 0, 1).reshape(n_exp, b * cap_c))
        gsel.append(jnp.swapaxes(gate_c, 0, 1).reshape(n_exp, b * cap_c))
    src = jnp.concatenate(src, axis=1)
    n_rows = src.shape[1]
    gate_col = jnp.concatenate(gsel, axis=1).reshape(n_exp, n_rows, 1)
    hid = _moe_up(src.reshape(-1), h2, w_e_gate, w_e_up, layer, n_rows)
    x2 = _moe_down(idx_l.reshape(-1), hid, gate_col, w_e_down, layer, x1, mod3, 5, b, l, cap_l,
                   0, 0, lambda g: g, rows)
    if not last:
        idx_cc = jnp.swapaxes(idx_c + (jnp.arange(b, dtype=I32) * lc)[:, None, None], 0, 1)
        x2 = _moe_down(idx_cc.reshape(-1), hid, gate_col, w_e_down, layer, x2, mod3, 5, 1, b * lc, b * cap_c,
                       b * cap_l, n_lat, lambda g: b, rows)
    return x2


def kernel(x, c, ctx, c_ctx, w_ada, b_ada, g_norm1, g_norm2, w_in, na_q_gain, na_k_gain, na_rel_bias, ml_conv,
           ml_gate_bias, ml_out_gain, gm_v_gain, gm_ws, gm_bias, w_branch, w_out, w_router, w_e_gate, w_e_up,
           w_e_down):
    b, l, d = x.shape
    lc = ctx.shape[1]
    depth = w_ada.shape[0]
    dims = {"B": b, "L": l, "Lc": lc, "n_lat": b * l}
    assert b + 1 <= 8 and l % ROW_TILE == 0 and lc % ROW_TILE == 0 and (b * l) % MM_TM == 0 and (b * lc) % MM_TM == 0

    c8 = jnp.concatenate([c, c_ctx[None, :], jnp.zeros((8 - b - 1, d), F32)], axis=0)
    mod = _ada(c8, w_ada, b_ada)
    xs = jnp.concatenate([x.reshape(b * l, d), ctx.reshape(b * lc, d)], axis=0)
    rope_tabs = _rope_tables(l, ROW_TILE)
    w_in_t = jnp.swapaxes(w_in, 1, 2)
    for i in range(depth):
        xs = _layer(xs, mod[i], i, i == depth - 1, dims, w_in_t, na_q_gain[i], na_k_gain[i], na_rel_bias[i],
                    ml_conv[i], ml_gate_bias[i], ml_out_gain[i], gm_v_gain[i], gm_ws[i], gm_bias[i], w_branch,
                    w_out, g_norm1[i], g_norm2[i], w_router[i], w_e_gate, w_e_up, w_e_down, rope_tabs)
    return xs[:b * l].reshape(b, l, d)
```

```python
import functools

import numpy as np
import jax
import jax.numpy as jnp
from jax import lax
from jax.experimental import pallas as pl
from jax.experimental.pallas import tpu as pltpu

F32 = jnp.float32
BF16 = jnp.bfloat16
I32 = jnp.int32

GRID_W = 64
NA_HEADS, NA_DH, NA_ROWS, NA_COLS = 8, 128, 8, 16
NA_W = NA_HEADS * NA_DH
ML_HEADS, ML_DK, ML_DV, ML_CHUNK = 8, 128, 256, 128
ML_QK, ML_V = ML_HEADS * ML_DK, ML_HEADS * ML_DV
GM_GROUPS, GM_CPG, GM_CHUNK = 8, 128, 128
GM_W = GM_GROUPS * GM_CPG
N_BRANCH = 3
CAPACITY_FACTOR = 2
ROPE_BASE = 10000.0
EPS = 1e-6
NEG_INF = -1e30

VMEM_LIMIT_BYTES = 56 * 1024 * 1024
LANES = 128
ROW_TILE = 256
MM_TM = 512
SCATTER_BATCH = 8
GATHER_UNROLL = 8

OFF_NA_Q, OFF_NA_K, OFF_NA_V = 0, NA_W, 2 * NA_W
OFF_ML_QK = 3 * NA_W
OFF_ML_V = OFF_ML_QK + 2 * ML_QK
OFF_ML_O = OFF_ML_V + ML_V
OFF_ML_GATES = OFF_ML_O + ML_V
N_ML_GATES = 4 * ML_HEADS
OFF_GM_U = OFF_ML_GATES
OFF_GM_V = OFF_GM_U + GM_W
OFF_GATE = OFF_GM_V + GM_W
HALF_W = 1024


def _wide_row_tile(rows):
    tm = rows // 8
    return tm if (rows % 8 == 0 and tm % 16 == 0 and tm >= MM_TM) else MM_TM


def _cparams(*sem):
    return pltpu.CompilerParams(dimension_semantics=sem, vmem_limit_bytes=VMEM_LIMIT_BYTES)


def _nt_dot(a, b):
    return lax.dot_general(a, b, (((1,), (1,)), ((), ())), preferred_element_type=F32)


def _ada_kernel(c_ref, w_ref, b_ref, o_ref):
    a = jax.nn.silu(c_ref[...]).astype(BF16)
    o_ref[...] = jnp.dot(a, w_ref[...].astype(BF16), preferred_element_type=F32) + b_ref[...]


def _ada(c8, w_ada, b_ada):
    depth, d, n = w_ada.shape
    tn = 512
    return pl.pallas_call(
        _ada_kernel,
        grid=(depth, n // tn),
        in_specs=[pl.BlockSpec((8, d), lambda l, j: (0, 0)),
                  pl.BlockSpec((None, d, tn), lambda l, j: (l, 0, j)),
                  pl.BlockSpec((None, 1, tn), lambda l, j: (l, 0, j))],
        out_specs=pl.BlockSpec((None, 8, tn), lambda l, j: (l, 0, j)),
        out_shape=jax.ShapeDtypeStruct((depth, 8, n), F32),
        compiler_params=_cparams("parallel", "parallel"),
        name="ada_mod",
    )(c8, w_ada, b_ada.reshape(depth, 1, n))


def _mod_row(i, tm, n_lat, seq, n_batch):
    r = i * tm
    return jnp.where(r >= n_lat, n_batch, r // seq)


HI16 = np.uint32(0xFFFF0000)


def _pack_bf16_pairs(y):
    half = y.shape[1] // 2
    bits = lax.bitcast_convert_type(y.astype(BF16).astype(F32), jnp.uint32)
    return (bits[:, half:] & HI16) | (bits[:, :half] >> 16)


def _unpack_bf16_pairs(packed):
    lo = lax.bitcast_convert_type(packed << 16, F32).astype(BF16)
    hi = lax.bitcast_convert_type(packed & HI16, F32).astype(BF16)
    return jnp.concatenate([lo, hi], axis=1)


def _prenorm_kernel(x_ref, g_ref, sh_ref, sc_ref, o_ref, *, packed):
    x = x_ref[...]
    y = x * lax.rsqrt(jnp.mean(x * x, axis=-1, keepdims=True) + EPS)
    y = y * g_ref[...]
    y = y * (1 + sc_ref[...]) + sh_ref[...]
    o_ref[...] = _pack_bf16_pairs(y) if packed else y.astype(o_ref.dtype)


def _prenorm(x, g, mod3, k_shift, k_scale, dims, rows, packed):
    d = x.shape[1]
    tm = ROW_TILE
    d_out = d // 2 if packed else d
    mrow = functools.partial(_mod_row, tm=tm, n_lat=dims["n_lat"], seq=dims["L"], n_batch=dims["B"])
    return pl.pallas_call(
        functools.partial(_prenorm_kernel, packed=packed),
        grid=(rows // tm,),
        in_specs=[pl.BlockSpec((tm, d), lambda i: (i, 0)),
                  pl.BlockSpec((1, d), lambda i: (0, 0)),
                  pl.BlockSpec((None, 1, d), lambda i: (mrow(i), 0, k_shift)),
                  pl.BlockSpec((None, 1, d), lambda i: (mrow(i), 0, k_scale))],
        out_specs=pl.BlockSpec((tm, d_out), lambda i: (i, 0)),
        out_shape=jax.ShapeDtypeStruct((rows, d_out), jnp.uint32 if packed else BF16),
        compiler_params=_cparams("parallel"),
        name="prenorm",
    )(x, g.reshape(1, d), mod3, mod3)


IN_TN = 1024
IN_MAX_CHUNKS = 16


def _inproj_kernel(a_ref, wt_hbm, o_ref, wp_ref, stage_ref, sem, *, layer, n_panels, n_aligned, shift, n_chunks):
    p, i = pl.program_id(0), pl.program_id(1)
    tn = o_ref.shape[1]
    rows_c = tn // n_chunks
    building = jnp.logical_and(p < n_panels, i < n_chunks)

    def chunk_copy(c):
        r0 = p * tn + jnp.where(p >= n_aligned, shift, 0) + c * rows_c
        src = wt_hbm.at[layer, pl.ds(pl.multiple_of(r0, 8), rows_c), :]
        return pltpu.make_async_copy(src, stage_ref.at[c % 2], sem.at[c % 2])

    @pl.when(jnp.logical_and(building, i == 0))
    def _():
        chunk_copy(0).start()

    @pl.when(building)
    def _():
        chunk_copy(i).wait()

        @pl.when(i + 1 < n_chunks)
        def _():
            chunk_copy(i + 1).start()

        wp_ref[p % 2, pl.ds(pl.multiple_of(i * rows_c, rows_c), rows_c), :] = stage_ref[i % 2].astype(BF16)

    @pl.when(p > 0)
    def _():
        o_ref[...] = _nt_dot(a_ref[...], wp_ref[(p + 1) % 2])


def _in_proj(h, w_in_t, layer):
    m, k = h.shape
    n_out = w_in_t.shape[1] - N_ML_GATES
    tn = IN_TN
    tm = _wide_row_tile(m)
    n_panels = n_out // tn
    n_chunks = IN_MAX_CHUNKS
    while n_chunks > m // tm:
        n_chunks //= 2
    assert OFF_ML_GATES % tn == 0 and n_out % tn == 0 and N_ML_GATES % 8 == 0 and n_chunks >= 1
    kern = functools.partial(_inproj_kernel, layer=layer, n_panels=n_panels, n_aligned=OFF_ML_GATES // tn,
                             shift=N_ML_GATES, n_chunks=n_chunks)
    return pl.pallas_call(
        kern,
        grid=(n_panels + 1, m // tm),
        in_specs=[pl.BlockSpec((tm, k), lambda p, i: (i, 0)),
                  pl.BlockSpec(memory_space=pl.ANY)],
        out_specs=pl.BlockSpec((tm, tn), lambda p, i: (jnp.where(p == 0, 0, i), jnp.maximum(p - 1, 0))),
        out_shape=jax.ShapeDtypeStruct((m, n_out), F32),
        scratch_shapes=[pltpu.VMEM((2, tn, k), BF16), pltpu.VMEM((2, tn // n_chunks, k), F32),
                        pltpu.SemaphoreType.DMA((2,))],
        compiler_params=_cparams("arbitrary", "arbitrary"),
        name="in_proj",
    )(h, w_in_t)


def _gates_kernel(a_ref, w_ref, b_ref, o_ref):
    g = _nt_dot(a_ref[...], w_ref[...].astype(BF16)) + b_ref[...]
    col = lax.broadcasted_iota(I32, g.shape, 1)
    is_forget = ((col // ML_HEADS) % 2) == 1
    o_ref[...] = jnp.where(is_forget, jax.nn.log_sigmoid(g), g)


def _ml_gates(h, w_in_t, layer, b_g):
    m, k = h.shape
    tm = MM_TM
    return pl.pallas_call(
        _gates_kernel,
        grid=(m // tm,),
        in_specs=[pl.BlockSpec((tm, k), lambda i: (i, 0)),
                  pl.BlockSpec((None, LANES, k), lambda i: (layer, OFF_ML_GATES // LANES, 0)),
                  pl.BlockSpec((1, LANES), lambda i: (0, 0))],
        out_specs=pl.BlockSpec((tm, LANES), lambda i: (i, 0)),
        out_shape=jax.ShapeDtypeStruct((m, LANES), F32),
        compiler_params=_cparams("parallel"),
        name="ml_gates",
    )(h, w_in_t, b_g)


def _head_rms(x, g):
    return x * lax.rsqrt(jnp.mean(x * x, axis=-1, keepdims=True) + EPS) * g


def _na_kernel(q_ref, k_ref, v_ref, kc_ref, vc_ref, qg_ref, kg_ref, bias_ref, o_ref,
               qn_ref, kn_ref, vn_ref, *, rows, kr):
    w = GRID_W
    ku = min(kr + 1, rows)
    nku = ku * w
    scale = NA_DH ** -0.5
    qn_ref[...] = _head_rms(q_ref[...], qg_ref[...]).astype(BF16)
    kn_ref[...] = _head_rms(k_ref[...], kg_ref[...]).astype(BF16)
    vn_ref[...] = v_ref[...].astype(BF16)
    kc = _head_rms(kc_ref[...], kg_ref[...]).astype(BF16)
    vc = vc_ref[...].astype(BF16)

    def body(it, carry):
        ra = 2 * it
        r0a = jnp.clip(ra - kr // 2, 0, rows - kr)
        r0b = jnp.clip(ra + 1 - kr // 2, 0, rows - kr)
        u0 = jnp.minimum(r0a, rows - ku)
        va = 2 * (r0a - ra + (kr - 1)) + (r0a - u0)
        vb = 2 * (r0b - (ra + 1) + (kr - 1)) + (r0b - u0)
        q2 = qn_ref[pl.ds(pl.multiple_of(ra * w, 2 * w), 2 * w), :]
        kw = kn_ref[pl.ds(pl.multiple_of(u0 * w, w), nku), :]
        vw = vn_ref[pl.ds(pl.multiple_of(u0 * w, w), nku), :]
        bias = jnp.concatenate([bias_ref[va], bias_ref[vb]], axis=0)
        s_lat = _nt_dot(q2, kw) * scale + bias
        s_ctx = _nt_dot(q2, kc) * scale
        m = jnp.maximum(jnp.max(s_lat, axis=-1, keepdims=True), jnp.max(s_ctx, axis=-1, keepdims=True))
        p_lat = jnp.exp(s_lat - m)
        p_ctx = jnp.exp(s_ctx - m)
        den = jnp.sum(p_lat, axis=-1, keepdims=True) + jnp.sum(p_ctx, axis=-1, keepdims=True)
        out = (jnp.dot(p_lat.astype(BF16), vw, preferred_element_type=F32)
               + jnp.dot(p_ctx.astype(BF16), vc, preferred_element_type=F32))
        o_ref[pl.ds(pl.multiple_of(ra * w, 2 * w), 2 * w), :] = (out / den).astype(o_ref.dtype)
        return carry

    lax.fori_loop(0, rows // 2, body, 0, unroll=4)


def _na_bias_table(rel_bias, kr, rows):
    w = GRID_W
    nk = kr * w
    ku = min(kr + 1, rows)
    n_dc = 2 * NA_COLS - 1
    dc = np.clip(np.arange(w)[None, :] - np.arange(w)[:, None] + NA_COLS - 1, 0, n_dc - 1)
    onehot = jnp.asarray((dc[:, :, None] == np.arange(n_dc)).astype(np.float32))
    toep = jnp.einsum("hrd,ckd->hrck", rel_bias.astype(F32), onehot, precision=lax.Precision.HIGHEST)
    r_lo = NA_ROWS - kr
    tab = jnp.stack([toep[:, r_lo + off:r_lo + off + kr] for off in range(kr)], axis=1)
    tab = jnp.transpose(tab, (0, 1, 3, 2, 4)).reshape(rel_bias.shape[0], kr, w, nk)
    c0 = np.clip(np.arange(w) - NA_COLS // 2, 0, w - NA_COLS)[:, None]
    kcol = (np.arange(nk) % w)[None, :]
    in_win = jnp.asarray((kcol >= c0) & (kcol < c0 + NA_COLS))
    tab = jnp.where(in_win, tab, NEG_INF)
    pad = ku * w - nk
    shifted = [jnp.pad(tab, ((0, 0), (0, 0), (0, 0), (d * w, pad - d * w)), constant_values=NEG_INF)
               for d in range(pad // w + 1)]
    shifted = (shifted + shifted)[:2]
    return jnp.stack(shifted, axis=2).reshape(rel_bias.shape[0], 2 * kr, w, ku * w)


def _na_latent(p, q_gain, k_gain, bias_tab, dims):
    b, l, lc = dims["B"], dims["L"], dims["Lc"]
    rows = l // GRID_W
    kr = min(NA_ROWS, rows)
    nk = min(kr + 1, rows) * GRID_W
    hd = NA_DH
    assert rows % 2 == 0
    kern = functools.partial(_na_kernel, rows=rows, kr=kr)
    ctx_blk = (b * l) // lc
    return pl.pallas_call(
        kern,
        grid=(b, NA_HEADS),
        in_specs=[pl.BlockSpec((l, hd), lambda i, h: (i, OFF_NA_Q // hd + h)),
                  pl.BlockSpec((l, hd), lambda i, h: (i, OFF_NA_K // hd + h)),
                  pl.BlockSpec((l, hd), lambda i, h: (i, OFF_NA_V // hd + h)),
                  pl.BlockSpec((lc, hd), lambda i, h: (ctx_blk + i, OFF_NA_K // hd + h)),
                  pl.BlockSpec((lc, hd), lambda i, h: (ctx_blk + i, OFF_NA_V // hd + h)),
                  pl.BlockSpec((1, hd), lambda i, h: (0, 0)),
                  pl.BlockSpec((1, hd), lambda i, h: (0, 0)),
                  pl.BlockSpec((None, 2 * kr, GRID_W, nk), lambda i, h: (h, 0, 0, 0))],
        out_specs=pl.BlockSpec((l, hd), lambda i, h: (i, h)),
        out_shape=jax.ShapeDtypeStruct((b * l, NA_W), BF16),
        scratch_shapes=[pltpu.VMEM((l, hd), BF16), pltpu.VMEM((l, hd), BF16), pltpu.VMEM((l, hd), BF16)],
        compiler_params=_cparams("parallel", "parallel"),
        name="na_latent",
    )(p, p, p, p, p, q_gain.reshape(1, hd), k_gain.reshape(1, hd), bias_tab)


def _ctx_attn_kernel(q_ref, k_ref, v_ref, qg_ref, kg_ref, o_ref):
    q = _head_rms(q_ref[...], qg_ref[...]).astype(BF16)
    k = _head_rms(k_ref[...], kg_ref[...]).astype(BF16)
    s = _nt_dot(q, k) * (NA_DH ** -0.5)
    m = jnp.max(s, axis=-1, keepdims=True)
    pr = jnp.exp(s - m)
    den = jnp.sum(pr, axis=-1, keepdims=True)
    out = jnp.dot(pr.astype(BF16), v_ref[...].astype(BF16), preferred_element_type=F32)
    o_ref[...] = (out / den).astype(o_ref.dtype)


def _na_context(p, q_gain, k_gain, dims):
    b, l, lc = dims["B"], dims["L"], dims["Lc"]
    hd = NA_DH
    ctx_blk = (b * l) // lc
    return pl.pallas_call(
        _ctx_attn_kernel,
        grid=(b, NA_HEADS),
        in_specs=[pl.BlockSpec((lc, hd), lambda i, h: (ctx_blk + i, OFF_NA_Q // hd + h)),
                  pl.BlockSpec((lc, hd), lambda i, h: (ctx_blk + i, OFF_NA_K // hd + h)),
                  pl.BlockSpec((lc, hd), lambda i, h: (ctx_blk + i, OFF_NA_V // hd + h)),
                  pl.BlockSpec((1, hd), lambda i, h: (0, 0)),
                  pl.BlockSpec((1, hd), lambda i, h: (0, 0))],
        out_specs=pl.BlockSpec((lc, hd), lambda i, h: (i, h)),
        out_shape=jax.ShapeDtypeStruct((b * lc, NA_W), BF16),
        compiler_params=_cparams("parallel", "parallel"),
        name="na_context",
    )(p, p, p, q_gain.reshape(1, hd), k_gain.reshape(1, hd))


def _rope_tables(seq, pad_rows):
    nf = ML_DK // 4
    t = np.arange(seq)
    pos = np.stack([t // GRID_W, t % GRID_W], axis=-1).astype(np.float32)
    inv_freq = (ROPE_BASE ** (-np.arange(nf, dtype=np.float32) / nf)).astype(np.float32)
    lane = np.arange(ML_DK)
    axis, pair, f = lane // (2 * nf), (lane // nf) % 2, lane % nf
    ang = jnp.asarray(pos[:, axis]) * jnp.asarray(inv_freq[f])[None, :]
    cos, sin = jnp.cos(ang), jnp.sin(ang)
    s_lo = jnp.where(jnp.asarray(pair == 0)[None, :], -sin, 0.0)
    s_hi = jnp.where(jnp.asarray(pair == 1)[None, :], sin, 0.0)
    pad = lambda a, v: jnp.concatenate([a, jnp.full((pad_rows, ML_DK), v, F32)], axis=0)
    return pad(cos, 1.0), pad(s_lo, 0.0), pad(s_hi, 0.0)


def _mlprep_kernel(x_ref, xp_ref, xn_ref, v0_ref, v1_ref, w_ref, cos_ref, slo_ref, shi_ref,
                   q_ref, kt_ref, v_ref, *, n_lat, seq, seq_ctx):
    i, j = pl.program_id(0), pl.program_id(1)
    t = x_ref.shape[0]
    nf = ML_DK // 4
    r0 = i * t
    is_lat = r0 < n_lat
    starts = jnp.where(is_lat, r0 % seq == 0, (r0 - n_lat) % seq_ctx == 0)
    ends = jnp.where(is_lat, (r0 + t) % seq == 0, (r0 + t - n_lat) % seq_ctx == 0)
    row = lax.broadcasted_iota(I32, (t, ML_DK), 0)

    def rope(h):
        cs = slice(h * ML_DK, (h + 1) * ML_DK)
        x = x_ref[:, cs]
        prev_row = jnp.where(starts, 0.0, xp_ref[7:8, cs])
        next_row = jnp.where(ends, 0.0, xn_ref[0:1, cs])
        x_prev = jnp.where(row == 0, prev_row, pltpu.roll(x, 1, 0))
        x_next = jnp.where(row == t - 1, next_row, pltpu.roll(x, t - 1, 0))
        yh = jax.nn.silu(x_prev * w_ref[0:1, cs] + x * w_ref[1:2, cs] + x_next * w_ref[2:3, cs])
        return (yh * cos_ref[...] + pltpu.roll(yh, ML_DK - nf, 1) * slo_ref[...]
                + pltpu.roll(yh, nf, 1) * shi_ref[...])

    @pl.when(j == 0)
    def _():
        half = ML_HEADS // 2
        for h in range(ML_HEADS):
            q_ref[h] = rope(h).astype(q_ref.dtype)
            hv = h % half
            v_src = v0_ref if h < half else v1_ref
            v_ref[h] = v_src[:, hv * ML_DV:(hv + 1) * ML_DV].astype(v_ref.dtype)

    @pl.when(j == 1)
    def _():
        for h in range(ML_HEADS):
            kt_ref[h] = (rope(h) * (ML_DK ** -0.5)).T.astype(kt_ref.dtype)


def _ml_prep(p, conv_w, tables, dims):
    m = p.shape[0]
    t = ROW_TILE
    n_lat, l, lc = dims["n_lat"], dims["L"], dims["Lc"]
    hd = ML_DK
    wb = ML_QK
    col0 = OFF_ML_QK // wb
    nblk8 = m // 8
    lat_blocks = l // t

    def tab_idx(i, j):
        return (jnp.where(i * t < n_lat, i % lat_blocks, lat_blocks), 0)

    kern = functools.partial(_mlprep_kernel, n_lat=n_lat, seq=l, seq_ctx=lc)
    return pl.pallas_call(
        kern,
        grid=(m // t, 2),
        in_specs=[pl.BlockSpec((t, wb), lambda i, j: (i, col0 + j)),
                  pl.BlockSpec((8, wb), lambda i, j: (jnp.maximum(i * (t // 8) - 1, 0), col0 + j)),
                  pl.BlockSpec((8, wb), lambda i, j: (jnp.minimum((i + 1) * (t // 8), nblk8 - 1), col0 + j)),
                  pl.BlockSpec((t, HALF_W), lambda i, j: (i, OFF_ML_V // HALF_W)),
                  pl.BlockSpec((t, HALF_W), lambda i, j: (i, OFF_ML_V // HALF_W + 1)),
                  pl.BlockSpec((conv_w.shape[0], wb), lambda i, j: (0, j)),
                  pl.BlockSpec((t, hd), tab_idx),
                  pl.BlockSpec((t, hd), tab_idx),
                  pl.BlockSpec((t, hd), tab_idx)],
        out_specs=[pl.BlockSpec((ML_HEADS, t, ML_DK), lambda i, j: (0, i, 0)),
                   pl.BlockSpec((ML_HEADS, ML_DK, t), lambda i, j: (0, 0, i)),
                   pl.BlockSpec((ML_HEADS, t, ML_DV), lambda i, j: (0, i, 0))],
        out_shape=[jax.ShapeDtypeStruct((ML_HEADS, m, ML_DK), BF16),
                   jax.ShapeDtypeStruct((ML_HEADS, ML_DK, m), BF16),
                   jax.ShapeDtypeStruct((ML_HEADS, m, ML_DV), BF16)],
        compiler_params=_cparams("arbitrary", "arbitrary"),
        name="ml_prep",
    )(p, p, p, p, p, conv_w, *tables)


ML_AUG = ML_DV + LANES


def _mlstm_kernel(q_ref, kt_ref, v_ref, gr_ref, o_ref, c_ref, m_ref, cum_ref):
    d, s = pl.program_id(1), pl.program_id(2)
    t = ML_CHUNK
    nh = ML_HEADS

    @pl.when(s == 0)
    def _():
        c_ref[...] = jnp.zeros_like(c_ref)
        m_ref[...] = jnp.zeros_like(m_ref)

    fwd = d == 0
    ri = lax.broadcasted_iota(I32, (t, t), 0)
    ci = lax.broadcasted_iota(I32, (t, t), 1)
    tri = (ci - ri) * jnp.where(fwd, 1, -1) <= 0
    trif = tri.astype(F32)
    cum_ref[...] = lax.dot_general(gr_ref[...], trif, (((1,), (1,)), ((), ())), preferred_element_type=F32,
                                   precision=lax.Precision.HIGHEST)
    row_i = jnp.where(fwd, 0, 2 * nh)
    row_f = row_i + nh
    ones = jnp.ones((t, LANES), BF16)

    def head(h, carry):
        lir = gr_ref[pl.ds(row_i + h, 1), :]
        lfr = gr_ref[pl.ds(row_f + h, 1), :]
        br = cum_ref[pl.ds(row_f + h, 1), :]
        bc = jnp.sum(trif * lfr, axis=-1, keepdims=True)
        blh = jnp.sum(lfr, axis=-1, keepdims=True)
        m_h = m_ref[pl.ds(h, 1), 0:1]
        a = bc + m_h
        dlog = jnp.where(tri, bc - br + lir, NEG_INF)
        mj = jnp.maximum(a, jnp.max(dlog, axis=-1, keepdims=True))
        w_inter = jnp.exp(a - mj)
        qh = q_ref[h]
        kth = kt_ref[h]
        vh = v_ref[h]
        sm = jnp.dot(qh, kth, preferred_element_type=F32) * jnp.exp(dlog - mj)
        c_h = c_ref[h]
        qc = jnp.dot(qh, c_h.astype(BF16), preferred_element_type=F32)
        num = w_inter * qc[:, :ML_DV] + jnp.dot(sm.astype(BF16), vh, preferred_element_type=F32)
        den = w_inter * qc[:, ML_DV:ML_DV + 1] + jnp.sum(sm, axis=-1, keepdims=True)
        o_ref[h] = num / jnp.maximum(jnp.abs(den), jnp.exp(-mj))
        gl_r = blh - br + lir
        m_new = jnp.maximum(blh + m_h, jnp.max(gl_r, axis=-1, keepdims=True))
        sc = jnp.exp(blh + m_h - m_new)
        ktw = (kth.astype(F32) * jnp.exp(gl_r - m_new)).astype(BF16)
        v_aug = jnp.concatenate([vh, ones], axis=1)
        c_ref[h] = sc * c_h + jnp.dot(ktw, v_aug, preferred_element_type=F32)
        m_ref[pl.ds(h, 1), :] = jnp.broadcast_to(m_new, (1, LANES))
        return carry

    lax.fori_loop(0, nh, head, 0, unroll=8)


def _mlstm_scan(q3, kt3, v3, gates_t, dims):
    m = q3.shape[1]
    t = ML_CHUNK
    b, l, lc, n_lat = dims["B"], dims["L"], dims["Lc"], dims["n_lat"]
    ncl, ncc = l // t, lc // t
    ng = gates_t.shape[0]

    def chunk(i, d, s):
        in_ctx = s < ncc
        c_ctx = jnp.where(d == 0, s, ncc - 1 - s)
        s_lat = s - ncc
        c_lat = jnp.where(d == 0, s_lat, ncl - 1 - s_lat)
        return jnp.where(in_ctx, n_lat // t + i * ncc + c_ctx, i * ncl + c_lat)

    return pl.pallas_call(
        _mlstm_kernel,
        grid=(b, 2, ncc + ncl),
        in_specs=[pl.BlockSpec((ML_HEADS, t, ML_DK), lambda i, d, s: (0, chunk(i, d, s), 0)),
                  pl.BlockSpec((ML_HEADS, ML_DK, t), lambda i, d, s: (0, 0, chunk(i, d, s))),
                  pl.BlockSpec((ML_HEADS, t, ML_DV), lambda i, d, s: (0, chunk(i, d, s), 0)),
                  pl.BlockSpec((ng, t), lambda i, d, s: (0, chunk(i, d, s)))],
        out_specs=pl.BlockSpec((None, ML_HEADS, t, ML_DV), lambda i, d, s: (d, 0, chunk(i, d, s), 0)),
        out_shape=jax.ShapeDtypeStruct((2, ML_HEADS, m, ML_DV), F32),
        scratch_shapes=[pltpu.VMEM((ML_HEADS, ML_DK, ML_AUG), F32),
                        pltpu.VMEM((ML_HEADS, LANES), F32),
                        pltpu.VMEM((ng, t), F32)],
        compiler_params=_cparams("parallel", "parallel", "arbitrary"),
        name="mlstm_scan",
    )(q3, kt3, v3, gates_t)


def _mlout_kernel(hf_ref, hb_ref, og_ref, g_ref, y_ref):
    for h in range(HALF_W // ML_DV):
        sl = slice(h * ML_DV, (h + 1) * ML_DV)
        x = hf_ref[h] + hb_ref[h]
        y = x * lax.rsqrt(jnp.mean(x * x, axis=-1, keepdims=True) + EPS) * g_ref[:, sl]
        y_ref[:, sl] = (y * jax.nn.sigmoid(og_ref[:, sl])).astype(y_ref.dtype)


def _ml_out(hs, p, gain, rows):
    tm = ROW_TILE
    wb = HALF_W
    hb = wb // ML_DV
    return pl.pallas_call(
        _mlout_kernel,
        grid=(rows // tm, ML_V // wb),
        in_specs=[pl.BlockSpec((None, hb, tm, ML_DV), lambda i, j: (0, j, i, 0)),
                  pl.BlockSpec((None, hb, tm, ML_DV), lambda i, j: (1, j, i, 0)),
                  pl.BlockSpec((tm, wb), lambda i, j: (i, OFF_ML_O // wb + j)),
                  pl.BlockSpec((1, wb), lambda i, j: (0, j))],
        out_specs=pl.BlockSpec((tm, wb), lambda i, j: (i, j)),
        out_shape=jax.ShapeDtypeStruct((rows, ML_V), BF16),
        compiler_params=_cparams("parallel", "parallel"),
        name="ml_out",
    )(hs, hs, p, gain.reshape(1, ML_V))


def _gelu_exact(x):
    return 0.5 * x * (1.0 + lax.erf(x * np.float32(np.sqrt(0.5))))


def _gmlp_kernel(u_ref, v_ref, g_ref, ws_ref, bt_ref, o_ref):
    u = _gelu_exact(u_ref[...])
    v = _gelu_exact(v_ref[...])
    vn = (v * lax.rsqrt(jnp.mean(v * v, axis=-1, keepdims=True) + EPS) * g_ref[...]).astype(BF16)
    bt = bt_ref[...]
    for c in range(u.shape[0] // GM_CHUNK):
        rs = slice(c * GM_CHUNK, (c + 1) * GM_CHUNK)
        for g in range(GM_GROUPS):
            cs = slice(g * GM_CPG, (g + 1) * GM_CPG)
            mixed = jnp.dot(ws_ref[g].astype(BF16), vn[rs, cs], preferred_element_type=F32) + bt[:, g:g + 1]
            o_ref[rs, cs] = (u[rs, cs] * mixed).astype(o_ref.dtype)


def _gmlp(p, v_gain, ws, bias, rows):
    tm = ROW_TILE
    return pl.pallas_call(
        _gmlp_kernel,
        grid=(rows // tm,),
        in_specs=[pl.BlockSpec((tm, GM_W), lambda i: (i, OFF_GM_U // GM_W)),
                  pl.BlockSpec((tm, GM_W), lambda i: (i, OFF_GM_V // GM_W)),
                  pl.BlockSpec((1, GM_W), lambda i: (0, 0)),
                  pl.BlockSpec((GM_GROUPS, GM_CHUNK, GM_CHUNK), lambda i: (0, 0, 0)),
                  pl.BlockSpec((GM_CHUNK, GM_GROUPS), lambda i: (0, 0))],
        out_specs=pl.BlockSpec((tm, GM_W), lambda i: (i, 0)),
        out_shape=jax.ShapeDtypeStruct((rows, GM_W), BF16),
        compiler_params=_cparams("parallel"),
        name="gmlp",
    )(p, p, v_gain.reshape(1, GM_W), ws, bias.T)


def _branch_kernel(ya_ref, ym_ref, yg_ref, w_ref, g0_ref, g1_ref, g2_ref, o_ref, wp_ref):
    @pl.when(pl.program_id(1) == 0)
    def _():
        wp_ref[...] = w_ref[...].astype(BF16)

    dot = functools.partial(jnp.dot, preferred_element_type=F32)
    za = dot(ya_ref[...], wp_ref[0:NA_W, :])
    zm = dot(ym_ref[...], wp_ref[NA_W:NA_W + ML_V, :])
    zg = dot(yg_ref[...], wp_ref[NA_W + ML_V:, :])
    z = (jax.nn.sigmoid(g0_ref[...]) * za + jax.nn.sigmoid(g1_ref[...]) * zm
         + jax.nn.sigmoid(g2_ref[...]) * zg)
    o_ref[...] = z.astype(o_ref.dtype)


def _branch(ya, ym, yg, w_branch, layer, p, d_model, rows):
    tn = 512
    tm = 2 * MM_TM if rows % (2 * MM_TM) == 0 else MM_TM
    k = w_branch.shape[1]
    gate_blk = lambda b: (OFF_GATE + b * d_model) // tn
    return pl.pallas_call(
        _branch_kernel,
        grid=(d_model // tn, rows // tm),
        in_specs=[pl.BlockSpec((tm, NA_W), lambda j, i: (i, 0)),
                  pl.BlockSpec((tm, ML_V), lambda j, i: (i, 0)),
                  pl.BlockSpec((tm, GM_W), lambda j, i: (i, 0)),
                  pl.BlockSpec((None, k, tn), lambda j, i: (layer, 0, j)),
                  pl.BlockSpec((tm, tn), lambda j, i: (i, gate_blk(0) + j)),
                  pl.BlockSpec((tm, tn), lambda j, i: (i, gate_blk(1) + j)),
                  pl.BlockSpec((tm, tn), lambda j, i: (i, gate_blk(2) + j))],
        out_specs=pl.BlockSpec((tm, tn), lambda j, i: (i, j)),
        out_shape=jax.ShapeDtypeStruct((rows, d_model), BF16),
        scratch_shapes=[pltpu.VMEM((k, tn), BF16)],
        compiler_params=_cparams("parallel", "arbitrary"),
        name="branch_merge",
    )(ya, ym, yg, w_branch, p, p, p)


def _outproj_kernel(z_ref, w_ref, x_ref, gt_ref, o_ref, wp_ref):
    @pl.when(pl.program_id(1) == 0)
    def _():
        wp_ref[...] = w_ref[...].astype(BF16)

    y = jnp.dot(z_ref[...], wp_ref[...], preferred_element_type=F32)
    o_ref[...] = x_ref[...] + gt_ref[...] * y


def _outproj(z, w_out, layer, x, mod3, k_gate, dims, rows):
    k = z.shape[1]
    d = w_out.shape[2]
    tn = 512
    tm = 2 * MM_TM if (rows % (2 * MM_TM) == 0 and dims["L"] % (2 * MM_TM) == 0) else MM_TM
    mrow = functools.partial(_mod_row, tm=tm, n_lat=dims["n_lat"], seq=dims["L"], n_batch=dims["B"])
    gblk = k_gate * (d // tn)
    return pl.pallas_call(
        _outproj_kernel,
        grid=(d // tn, rows // tm),
        in_specs=[pl.BlockSpec((tm, k), lambda j, i: (i, 0)),
                  pl.BlockSpec((None, k, tn), lambda j, i: (layer, 0, j)),
                  pl.BlockSpec((tm, tn), lambda j, i: (i, j)),
                  pl.BlockSpec((None, 1, tn), lambda j, i: (mrow(i), 0, gblk + j))],
        out_specs=pl.BlockSpec((tm, tn), lambda j, i: (i, j)),
        out_shape=jax.ShapeDtypeStruct((rows, d), F32),
        scratch_shapes=[pltpu.VMEM((k, tn), BF16)],
        compiler_params=_cparams("parallel", "arbitrary"),
        name="out_proj",
    )(z, w_out, x, mod3)


def _affinity_kernel(h_ref, w_ref, o_ref, *, n_exp):
    logits = jnp.dot(_unpack_bf16_pairs(h_ref[...]), w_ref[...], preferred_element_type=F32)
    col = lax.broadcasted_iota(I32, logits.shape, 1)
    logits = jnp.where(col < n_exp, logits, NEG_INF)
    o_ref[...] = jax.nn.softmax(logits, axis=-1)


def _affinity(h2, w_router, rows):
    d, e = w_router.shape
    tm = MM_TM
    w_pad = jnp.pad(w_router, ((0, 0), (0, LANES - e))).astype(BF16)
    return pl.pallas_call(
        functools.partial(_affinity_kernel, n_exp=e),
        grid=(rows // tm,),
        in_specs=[pl.BlockSpec((tm, d // 2), lambda i: (i, 0)),
                  pl.BlockSpec((d, LANES), lambda i: (0, 0))],
        out_specs=pl.BlockSpec((tm, LANES), lambda i: (i, 0)),
        out_shape=jax.ShapeDtypeStruct((rows, LANES), F32),
        compiler_params=_cparams("parallel"),
        name="router_affinity",
    )(h2, w_pad)


def _cumsum_rows(x01, blk):
    n = x01.shape[0]
    ri = lax.broadcasted_iota(I32, (blk, blk), 0)
    ci = lax.broadcasted_iota(I32, (blk, blk), 1)
    tril = (ci <= ri).astype(BF16)
    parts = []
    carry = jnp.zeros((1, x01.shape[1]), F32)
    for j in range(n // blk):
        cs = jnp.dot(tril, x01[j * blk:(j + 1) * blk, :].astype(BF16), preferred_element_type=F32) + carry
        parts.append(cs)
        carry = cs[blk - 1:blk, :]
    return jnp.concatenate(parts, axis=0) if len(parts) > 1 else parts[0]


def _select_kernel(aff_ref, idx_ref, gate_ref, rank_ref, *, cap, tb, n_exp):
    aff = aff_ref[...]
    n = aff.shape[0]
    key = lax.bitcast_convert_type(aff, I32)
    capf = jnp.float32(cap)

    def search(i, thr):
        cand = thr | jnp.left_shift(jnp.int32(1), 30 - i)
        cnt = jnp.sum((key >= cand).astype(F32), axis=0, keepdims=True)
        return jnp.where(cnt >= capf, cand, thr)

    thr = lax.fori_loop(0, 31, search, jnp.zeros((1, LANES), I32))
    above = key > thr
    tied = key == thr
    need = capf - jnp.sum(above.astype(F32), axis=0, keepdims=True)
    tied_f = tied.astype(F32)
    tie_rank = _cumsum_rows(tied_f, tb) - tied_f
    sel = above | (tied & (tie_rank < need))
    incl = _cumsum_rows(sel.astype(F32), tb)
    rank_ref[0] = incl
    rank_ref[1] = jnp.where(sel, incl, -1.0)
    rank_ref[2] = aff
    slot = lax.broadcasted_iota(I32, (tb, cap), 1).astype(F32)

    for ex in range(n_exp):
        def block(j, acc):
            acc_i, acc_g = acc
            rows = pl.ds(pl.multiple_of(j * tb, tb), tb)
            inc = rank_ref[0, rows, ex:ex + 1]
            inc_sel = rank_ref[1, rows, ex:ex + 1]
            a = rank_ref[2, rows, ex:ex + 1]
            acc_i = acc_i + jnp.sum(jnp.where(inc <= slot, 1.0, 0.0), axis=0, keepdims=True)
            acc_g = acc_g + jnp.sum(jnp.where(inc_sel == slot + 1.0, a, 0.0), axis=0, keepdims=True)
            return acc_i, acc_g

        zero = jnp.zeros((1, cap), F32)
        acc_i, acc_g = lax.fori_loop(0, n // tb, block, (zero, zero))
        idx_ref[ex:ex + 1, :] = acc_i.astype(I32)
        gate_ref[ex:ex + 1, :] = acc_g


def _select(aff, e, n_groups, group_len, blk0, cap):
    tb = min(256, group_len)
    kern = functools.partial(_select_kernel, cap=cap, tb=tb, n_exp=e)
    return pl.pallas_call(
        kern,
        grid=(n_groups,),
        in_specs=[pl.BlockSpec((group_len, LANES), lambda g: (blk0 + g, 0))],
        out_specs=[pl.BlockSpec((None, e, cap), lambda g: (g, 0, 0)),
                   pl.BlockSpec((None, e, cap), lambda g: (g, 0, 0))],
        out_shape=[jax.ShapeDtypeStruct((n_groups, e, cap), I32),
                   jax.ShapeDtypeStruct((n_groups, e, cap), F32)],
        scratch_shapes=[pltpu.VMEM((3, group_len, LANES), F32)],
        compiler_params=_cparams("parallel"),
        name="router_select",
    )(aff)


def _moe_up_kernel(idx_ref, h_hbm, wg_ref, wu_ref, o_ref, xs_ref, xb_ref, sem, *, n_rows, n_exp, n_f):
    e, f = pl.program_id(0), pl.program_id(1)
    per_step = n_rows // n_f

    def row_copy(ex, i):
        slot = ex % 2
        src = idx_ref[ex * n_rows + i]
        return pltpu.make_async_copy(h_hbm.at[pl.ds(src, 1), :], xs_ref.at[slot, pl.ds(i, 1), :], sem.at[slot])

    def start_rows(ex, lo, n):
        def start(blk, c):
            for r in range(GATHER_UNROLL):
                row_copy(ex, lo + blk * GATHER_UNROLL + r).start(priority=r % 2)
            return c
        lax.fori_loop(0, n // GATHER_UNROLL, start, 0)

    @pl.when(jnp.logical_and(e == 0, f == 0))
    def _():
        start_rows(0, 0, n_rows)

    @pl.when(f == 0)
    def _():
        def wait(i, c):
            row_copy(e, i).wait()
            return c
        lax.fori_loop(0, n_rows, wait, 0, unroll=4)
        xb_ref[...] = _unpack_bf16_pairs(xs_ref[e % 2])

    @pl.when(e + 1 < n_exp)
    def _():
        start_rows(e + 1, f * per_step, per_step)

    xb = xb_ref[...]
    a = jnp.dot(xb, wg_ref[...].astype(BF16), preferred_element_type=F32)
    u = jnp.dot(xb, wu_ref[...].astype(BF16), preferred_element_type=F32)
    o_ref[...] = (jax.nn.silu(a) * u).astype(o_ref.dtype)


def _moe_up(idx_flat, h2_packed, w_gate, w_up, layer, n_rows):
    _, e, d, ff = w_gate.shape
    tf = 256
    n_f = ff // tf
    assert n_rows % (n_f * GATHER_UNROLL) == 0
    kern = functools.partial(_moe_up_kernel, n_rows=n_rows, n_exp=e, n_f=n_f)
    return pl.pallas_call(
        kern,
        grid_spec=pltpu.PrefetchScalarGridSpec(
            num_scalar_prefetch=1,
            grid=(e, n_f),
            in_specs=[pl.BlockSpec(memory_space=pl.ANY),
                      pl.BlockSpec((None, None, d, tf), lambda i, f, idx: (layer, i, 0, f)),
                      pl.BlockSpec((None, None, d, tf), lambda i, f, idx: (layer, i, 0, f))],
            out_specs=pl.BlockSpec((None, n_rows, tf), lambda i, f, idx: (i, 0, f)),
            scratch_shapes=[pltpu.VMEM((2, n_rows, d // 2), jnp.uint32), pltpu.VMEM((n_rows, d), BF16),
                            pltpu.SemaphoreType.DMA((2,))]),
        out_shape=jax.ShapeDtypeStruct((e, n_rows, ff), BF16),
        compiler_params=_cparams("arbitrary", "arbitrary"),
        name="moe_up",
    )(idx_flat, h2_packed, w_gate, w_up)


def _moe_down_kernel(idx_ref, hid_ref, gcol_ref, wd_ref, gt_ref, x_hbm, o_hbm, acc_ref, y_ref, sem, *,
                     cap, n_exp, group_len, row0, dc):
    g, j, e = pl.program_id(0), pl.program_id(1), pl.program_id(2)
    nlt = dc // LANES
    sub = 8
    rows = pl.ds(pl.multiple_of((row0 + g * group_len) // sub, group_len // sub), group_len // sub)

    def tile_copy(c, to_vmem):
        hbm = (x_hbm if to_vmem else o_hbm).at[rows, :, pl.ds(pl.multiple_of(j * dc + c * LANES, LANES), LANES)]
        vmem = acc_ref.at[:, c]
        return pltpu.make_async_copy(hbm, vmem, sem.at[0]) if to_vmem else pltpu.make_async_copy(vmem, hbm, sem.at[1])

    @pl.when(e == 0)
    def _():
        for c in range(nlt):
            tile_copy(c, True).start()

    chunk = min(cap, 128)
    w = wd_ref[...].astype(BF16)
    gt = gt_ref[...]
    for rc in range(cap // chunk):
        rs = slice(rc * chunk, (rc + 1) * chunk)
        y = jnp.dot(hid_ref[rs, :], w, preferred_element_type=F32) * gcol_ref[rs, :] * gt
        for rr in range(chunk // sub):
            for c in range(nlt):
                y_ref[rc * chunk // sub + rr, c] = y[rr * sub:(rr + 1) * sub, c * LANES:(c + 1) * LANES]

    @pl.when(e == 0)
    def _():
        for c in range(nlt):
            tile_copy(c, True).wait()

    base = (g * n_exp + e) * cap
    acc_flat = acc_ref.reshape(group_len * nlt, LANES)
    y_flat = y_ref.reshape(cap * nlt, LANES)

    def add_rows(bi, carry):
        i0 = bi * SCATTER_BATCH
        starts = [idx_ref[base + i0 + r] for r in range(SCATTER_BATCH)]
        acc = [acc_flat[pl.ds(s, nlt, stride=sub), :] for s in starts]
        for r in range(SCATTER_BATCH):
            yr = y_flat[pl.ds(i0 * nlt + r, nlt, stride=sub), :]
            acc_flat[pl.ds(starts[r], nlt, stride=sub), :] = acc[r] + yr
        return carry

    lax.fori_loop(0, cap // SCATTER_BATCH, add_rows, 0)

    @pl.when(e == n_exp - 1)
    def _():
        for c in range(nlt):
            tile_copy(c, False).start()
        for c in range(nlt):
            tile_copy(c, False).wait()


def _moe_down(idx_flat, hid, gate_col, w_down, layer, x, mod3, k_gate, n_groups, group_len, cap,
              hid_row0, x_row0, mod_row_of_group, out_rows):
    _, e, ff, d = w_down.shape
    assert SCATTER_BATCH == 8 and cap % 8 == 0 and x_row0 % group_len == 0 and out_rows % 8 == 0
    dc = 1024
    nlt = dc // LANES
    kern = functools.partial(_moe_down_kernel, cap=cap, n_exp=e, group_len=group_len, row0=x_row0, dc=dc)
    hblk0 = hid_row0 // cap
    gblk = k_gate * (d // dc)
    out = pl.pallas_call(
        kern,
        grid_spec=pltpu.PrefetchScalarGridSpec(
            num_scalar_prefetch=1,
            grid=(n_groups, d // dc, e),
            in_specs=[pl.BlockSpec((None, cap, ff), lambda g, j, i, idx: (i, hblk0 + g, 0)),
                      pl.BlockSpec((None, cap, 1), lambda g, j, i, idx: (i, hblk0 + g, 0)),
                      pl.BlockSpec((None, None, ff, dc), lambda g, j, i, idx: (layer, i, 0, j)),
                      pl.BlockSpec((None, 1, dc), lambda g, j, i, idx: (mod_row_of_group(g), 0, gblk + j)),
                      pl.BlockSpec(memory_space=pl.ANY)],
            out_specs=pl.BlockSpec(memory_space=pl.ANY),
            scratch_shapes=[pltpu.VMEM((group_len // 8, nlt, 8, LANES), F32),
                            pltpu.VMEM((cap // 8, nlt, 8, LANES), F32),
                            pltpu.SemaphoreType.DMA((2,))]),
        out_shape=jax.ShapeDtypeStruct((out_rows // 8, 8, d), F32),
        input_output_aliases={5: 0},
        compiler_params=_cparams("arbitrary", "arbitrary", "arbitrary"),
        name="moe_down",
    )((idx_flat >> 3) * (8 * nlt) + (idx_flat & 7), hid, gate_col, w_down, mod3, x.reshape(out_rows // 8, 8, d))
    return out.reshape(out_rows, d)


def _layer(x, mod_l, layer, last, dims, w_in, na_q_gain, na_k_gain, na_rel_bias, ml_conv, ml_gate_bias,
           ml_out_gain, gm_v_gain, gm_ws, gm_bias, w_branch, w_out, g_norm1, g_norm2, w_router, w_e_gate, w_e_up,
           w_e_down, rope_tabs):
    b, l, lc, n_lat = dims["B"], dims["L"], dims["Lc"], dims["n_lat"]
    m, d = x.shape
    rows = n_lat if last else m
    mod3 = mod_l.reshape(mod_l.shape[0], 1, mod_l.shape[1])
    n_exp = w_router.shape[1]

    h = _prenorm(x, g_norm1, mod3, 0, 1, dims, m, False)
    p = _in_proj(h, w_in, layer)
    b_g = jnp.pad(ml_gate_bias.reshape(1, N_ML_GATES), ((0, 0), (0, LANES - N_ML_GATES)))
    gates = _ml_gates(h, w_in, layer, b_g)
    gates_t = gates[:, :N_ML_GATES].T

    kr = min(NA_ROWS, l // GRID_W)
    ya = _na_latent(p, na_q_gain, na_k_gain, _na_bias_table(na_rel_bias, kr, l // GRID_W), dims)
    if not last:
        ya = jnp.concatenate([ya, _na_context(p, na_q_gain, na_k_gain, dims)], axis=0)

    q3, kt3, v3 = _ml_prep(p, ml_conv, rope_tabs, dims)
    hs = _mlstm_scan(q3, kt3, v3, gates_t, dims)
    ym = _ml_out(hs, p, ml_out_gain, rows)
    yg = _gmlp(p, gm_v_gain, gm_ws, gm_bias, rows)

    z = _branch(ya, ym, yg, w_branch, layer, p, d, rows)
    x1 = _outproj(z, w_out, layer, x, mod3, 2, dims, rows)

    h2 = _prenorm(x1, g_norm2, mod3, 3, 4, dims, rows, True)
    aff = _affinity(h2, w_router, rows)
    cap_l = max(1, min(l, (CAPACITY_FACTOR * l) // n_exp))
    idx_l, gate_l = _select(aff, n_exp, b, l, 0, cap_l)
    row_l = idx_l + (jnp.arange(b, dtype=I32) * l)[:, None, None]
    src = [jnp.swapaxes(row_l, 0, 1).reshape(n_exp, b * cap_l)]
    gsel = [jnp.swapaxes(gate_l, 0, 1).reshape(n_exp, b * cap_l)]
    if not last:
        cap_c = max(1, min(lc, (CAPACITY_FACTOR * lc) // n_exp))
        idx_c, gate_c = _select(aff, n_exp, b, lc, n_lat // lc, cap_c)
        row_c = idx_c + (n_lat + jnp.arange(b, dtype=I32) * lc)[:, None, None]
        src.append(jnp.swapaxes(row_c, 0, 1).reshape(n_exp, b * cap_c))
        gsel.append(jnp.swapaxes(gate_c, 0, 1).reshape(n_exp, b * cap_c))
    src = jnp.concatenate(src, axis=1)
    n_rows = src.shape[1]
    gate_col = jnp.concatenate(gsel, axis=1).reshape(n_exp, n_rows, 1)
    hid = _moe_up(src.reshape(-1), h2, w_e_gate, w_e_up, layer, n_rows)
    x2 = _moe_down(idx_l.reshape(-1), hid, gate_col, w_e_down, layer, x1, mod3, 5, b, l, cap_l,
                   0, 0, lambda g: g, rows)
    if not last:
        idx_cc = jnp.swapaxes(idx_c + (jnp.arange(b, dtype=I32) * lc)[:, None, None], 0, 1)
        x2 = _moe_down(idx_cc.reshape(-1), hid, gate_col, w_e_down, layer, x2, mod3, 5, 1, b * lc, b * cap_c,
                       b * cap_l, n_lat, lambda g: b, rows)
    return x2


def kernel(x, c, ctx, c_ctx, w_ada, b_ada, g_norm1, g_norm2, w_in, na_q_gain, na_k_gain, na_rel_bias, ml_conv,
           ml_gate_bias, ml_out_gain, gm_v_gain, gm_ws, gm_bias, w_branch, w_out, w_router, w_e_gate, w_e_up,
           w_e_down):
    b, l, d = x.shape
    lc = ctx.shape[1]
    depth = w_ada.shape[0]
    dims = {"B": b, "L": l, "Lc": lc, "n_lat": b * l}
    assert b + 1 <= 8 and l % ROW_TILE == 0 and lc % ROW_TILE == 0 and (b * l) % MM_TM == 0 and (b * lc) % MM_TM == 0

    c8 = jnp.concatenate([c, c_ctx[None, :], jnp.zeros((8 - b - 1, d), F32)], axis=0)
    mod = _ada(c8, w_ada, b_ada)
    xs = jnp.concatenate([x.reshape(b * l, d), ctx.reshape(b * lc, d)], axis=0)
    rope_tabs = _rope_tables(l, ROW_TILE)
    w_in_t = jnp.swapaxes(w_in, 1, 2)
    for i in range(depth):
        xs = _layer(xs, mod[i], i, i == depth - 1, dims, w_in_t, na_q_gain[i], na_k_gain[i], na_rel_bias[i],
                    ml_conv[i], ml_gate_bias[i], ml_out_gain[i], gm_v_gain[i], gm_ws[i], gm_bias[i], w_branch,
                    w_out, g_norm1[i], g_norm2[i], w_router[i], w_e_gate, w_e_up, w_e_down, rope_tabs)
    return xs[:b * l].reshape(b, l, d)
```
